```python
import jax, jax.numpy as jnp
from jax import lax
import numpy as np

D_MODEL = 2048
BATCH = 4
SEQ = 2048
DEPTH = 1

GRID_W = 64
CTX_LEN = 256
M_HEADS = 8
QK_DIM = 128
V_DIM = 256
QK_WIDTH = M_HEADS * QK_DIM
V_WIDTH = M_HEADS * V_DIM
CONV_W = 5
MLSTM_CHUNK = 128
SGU_GROUPS = 8
SGU_CHUNK = 128
SGU_WIDTH = 2048
SGU_GDIM = SGU_WIDTH // SGU_GROUPS
ROWS_PER_CHUNK = SGU_CHUNK // GRID_W
N_EXPERTS = 32
TOP_K = 4
D_EXPERT = 2048
SWIGLU_ALPHA = 1.702
SWIGLU_LIMIT = 7.0
MOE_BLOCK = 256
EPS = 1e-6
N_MOD = 6

OFF_Q = 0
OFF_K = OFF_Q + QK_WIDTH
OFF_V = OFF_K + QK_WIDTH
OFF_GATES = OFF_V + V_WIDTH
OFF_O = OFF_GATES + 4 * M_HEADS
OFF_U = OFF_O + V_WIDTH
OFF_SV = OFF_U + SGU_WIDTH
OFF_MERGE = OFF_SV + SGU_WIDTH
IN_COLS = OFF_MERGE + 2 * D_MODEL

kernel_name = 'hybrid_mlstm_chunkmlp_moe_dit_block'


def rmsnorm(x, g):
    xf = x.astype(jnp.float32)
    y = xf * lax.rsqrt(jnp.mean(xf * xf, axis=-1, keepdims=True) + EPS)
    return (y * g).astype(x.dtype)


def modulate(h, shift, scale):
    return h * (1 + scale) + shift


def heads(a, d):
    B, T, _ = a.shape
    return a.reshape(B, T, M_HEADS, d).transpose(0, 2, 1, 3)


def centred_dwconv(x, w, b):
    C = x.shape[-1]
    y = lax.conv_general_dilated(x, w[:, None, :].astype(x.dtype), (1,), [(CONV_W // 2, CONV_W // 2)],
                                 dimension_numbers=('NWC', 'WIO', 'NWC'), feature_group_count=C)
    return y + b


def in_projection(h, w_in, gate_b, conv_w, conv_b, full):
    B, T, _ = h.shape
    z = h @ (w_in if full else w_in[:, :OFF_O])
    qk = jax.nn.silu(centred_dwconv(z[..., :OFF_V], conv_w, conv_b))
    q = heads(qk[..., :QK_WIDTH], QK_DIM) * (QK_DIM ** -0.5)
    k = heads(qk[..., QK_WIDTH:], QK_DIM)
    v = heads(z[..., OFF_V:OFF_GATES], V_DIM)
    gates = z[..., OFF_GATES:OFF_O].astype(jnp.float32).reshape(B, T, 4, M_HEADS) + gate_b
    gates = gates.transpose(0, 2, 3, 1)
    mix = (q, k, v, gates)
    if not full:
        return mix, None
    rest = (z[..., OFF_O:OFF_U], z[..., OFF_U:OFF_SV], z[..., OFF_SV:OFF_MERGE], z[..., OFF_MERGE:])
    return mix, rest


def to_chunks(a):
    B, H, T = a.shape[:3]
    a = a.reshape(B, H, T // MLSTM_CHUNK, MLSTM_CHUNK, *a.shape[3:])
    return jnp.moveaxis(a, 2, 0)


def mlstm_chunked(q, k, v, i_pre, f_pre):
    B, H, T, _ = q.shape
    L = MLSTM_CHUNK
    tri = jnp.tril(jnp.ones((L, L), dtype=bool))
    log_f = jax.nn.log_sigmoid(f_pre)

    def step(carry, blk):
        C, n, m = carry
        qb, kb, vb, ib, lfb = blk
        b = jnp.cumsum(lfb, axis=-1)
        d_log = jnp.where(tri, b[..., :, None] - b[..., None, :] + ib[..., None, :], -jnp.inf)
        inter = b + m[..., None]
        m_t = jnp.maximum(inter, jnp.max(d_log, axis=-1))
        w_inter = jnp.exp(inter - m_t)
        s = jnp.einsum('bhtd,bhsd->bhts', qb, kb).astype(jnp.float32) * jnp.exp(d_log - m_t[..., None])
        num = w_inter[..., None] * jnp.einsum('bhtd,bhde->bhte', qb, C) + jnp.einsum('bhts,bhse->bhte', s, vb)
        den = w_inter * jnp.einsum('bhtd,bhd->bht', qb, n) + jnp.sum(s, axis=-1)
        h = num / jnp.maximum(jnp.abs(den), jnp.exp(-m_t))[..., None]
        b_end = b[..., -1]
        log_w = b_end[..., None] - b + ib
        m_new = jnp.maximum(b_end + m, jnp.max(log_w, axis=-1))
        decay = jnp.exp(b_end + m - m_new)
        w = jnp.exp(log_w - m_new[..., None])
        C = decay[..., None, None] * C + jnp.einsum('bhs,bhsd,bhse->bhde', w, kb, vb)
        n = decay[..., None] * n + jnp.einsum('bhs,bhsd->bhd', w, kb)
        return (C, n, m_new), h

    state0 = (jnp.zeros((B, H, QK_DIM, V_DIM), jnp.float32),
              jnp.zeros((B, H, QK_DIM), jnp.float32),
              jnp.zeros((B, H), jnp.float32))
    blocks = (to_chunks(q), to_chunks(k), to_chunks(v), to_chunks(i_pre), to_chunks(log_f))
    _, hs = lax.scan(step, state0, blocks)
    return jnp.moveaxis(hs, 0, 2).reshape(B, H, T, V_DIM)


def mlstm_bidirectional(mix_ctx, mix_lat):
    qc, kc, vc, gc = mix_ctx
    ql, kl, vl, gl = mix_lat
    Tc = qc.shape[2]
    cat = lambda a, b: jnp.concatenate([a, b], axis=2)
    rev = lambda a: jnp.flip(a, axis=2)
    h_f = mlstm_chunked(cat(qc, ql), cat(kc, kl), cat(vc, vl), cat(gc[:, 0], gl[:, 0]), cat(gc[:, 1], gl[:, 1]))
    h_b = mlstm_chunked(cat(rev(qc), rev(ql)), cat(rev(kc), rev(kl)), cat(rev(vc), rev(vl)),
                        cat(rev(gc[:, 2]), rev(gl[:, 2])), cat(rev(gc[:, 3]), rev(gl[:, 3])))
    h_lat = h_f[:, :, Tc:] + rev(h_b[:, :, Tc:])
    h_ctx = h_f[:, :, :Tc] + rev(h_b[:, :, :Tc])
    return h_lat, h_ctx


def mlstm_readout(h, o, g):
    B, H, T, _ = h.shape
    h = rmsnorm(h.transpose(0, 2, 1, 3), g.reshape(H, V_DIM))
    return h.reshape(B, T, V_WIDTH) * jax.nn.sigmoid(o)


def chunk_mlp(u, sv, n_chunks, g, sgu_w, sgu_b):
    B, T, _ = u.shape
    u = jax.nn.gelu(u)
    sv = rmsnorm(jax.nn.gelu(sv), g)
    vc = sv.reshape(B, n_chunks, SGU_CHUNK, SGU_GROUPS, SGU_GDIM)
    z = jnp.einsum('gts,bnsgd->bntgd', sgu_w, vc) + sgu_b.T[None, None, :, :, None]
    return u * z.reshape(B, T, SGU_WIDTH)


def token_mixer(h_lat, h_ctx, n_lat_chunks, with_ctx_out, w_in, gate_b, conv_w, conv_b, mlstm_norm_g,
                sgu_norm_g, sgu_w, sgu_b, proj_a, proj_b, w_out):
    mix_lat, rest_lat = in_projection(h_lat, w_in, gate_b, conv_w, conv_b, True)
    mix_ctx, rest_ctx = in_projection(h_ctx, w_in, gate_b, conv_w, conv_b, with_ctx_out)
    hm_lat, hm_ctx = mlstm_bidirectional(mix_ctx, mix_lat)

    def merge(hm, rest, n_chunks, dtype):
        o, u, sv, mg = rest
        y_a = mlstm_readout(hm, o, mlstm_norm_g).astype(dtype) @ proj_a
        y_b = chunk_mlp(u, sv, n_chunks, sgu_norm_g, sgu_w, sgu_b) @ proj_b
        g_a = jax.nn.sigmoid(mg[..., :D_MODEL])
        g_b = jax.nn.sigmoid(mg[..., D_MODEL:])
        return (g_a * y_a + g_b * y_b) @ w_out

    out_lat = merge(hm_lat, rest_lat, n_lat_chunks, h_lat.dtype)
    out_ctx = merge(hm_ctx, rest_ctx, h_ctx.shape[1] // SGU_CHUNK, h_ctx.dtype) if with_ctx_out else None
    return out_lat, out_ctx


def clamped_swiglu(hb):
    glu = jnp.minimum(hb[..., :D_EXPERT], SWIGLU_LIMIT)
    lin = jnp.clip(hb[..., D_EXPERT:], -SWIGLU_LIMIT, SWIGLU_LIMIT)
    return glu * jax.nn.sigmoid(SWIGLU_ALPHA * glu) * (lin + 1)


def moe_ffn(xt, router_w, router_b, w1, b1, w2, b2):
    N = xt.shape[0]
    logits = (xt @ router_w).astype(jnp.float32) + router_b
    top_val, top_idx = lax.top_k(logits, TOP_K)
    gates = jax.nn.softmax(top_val, axis=-1)
    M = N * TOP_K
    eid = top_idx.reshape(M)
    tok = jnp.arange(M, dtype=jnp.int32) // TOP_K
    gate = gates.reshape(M)
    order = jnp.argsort(eid)
    e_sorted = eid[order]
    counts = jnp.zeros((N_EXPERTS,), jnp.int32).at[eid].add(1)
    padded = (counts + MOE_BLOCK - 1) // MOE_BLOCK * MOE_BLOCK
    start_sorted = jnp.cumsum(counts) - counts
    pad_end = jnp.cumsum(padded)
    start_pad = pad_end - padded
    dest = start_pad[e_sorted] + jnp.arange(M, dtype=jnp.int32) - start_sorted[e_sorted]
    n_blocks = -(-M // MOE_BLOCK) + N_EXPERTS
    P = n_blocks * MOE_BLOCK
    buf_tok = jnp.zeros((P,), jnp.int32).at[dest].set(tok[order])
    buf_gate = jnp.zeros((P,), jnp.float32).at[dest].set(gate[order])
    block_start = jnp.arange(n_blocks, dtype=jnp.int32) * MOE_BLOCK
    block_e = jnp.minimum(jnp.searchsorted(pad_end, block_start, side='right'), N_EXPERTS - 1)

    def body(acc, blk):
        tok_b, gate_b, e = blk
        hb = xt[tok_b] @ w1[e] + b1[e]
        y = clamped_swiglu(hb) @ w2[e] + b2[e]
        return acc.at[tok_b].add(gate_b[:, None] * y.astype(jnp.float32)), None

    acc, _ = lax.scan(body, jnp.zeros(xt.shape, jnp.float32),
                      (buf_tok.reshape(n_blocks, MOE_BLOCK), buf_gate.reshape(n_blocks, MOE_BLOCK), block_e))
    return acc.astype(xt.dtype)


def setup_inputs(seed: int = 0) -> dict:
    key = jax.random.key(seed)
    ks = jax.random.split(key, 28)
    nrm = lambda k, shape, s: jax.random.normal(k, shape, jnp.float32) * s
    f_bias = jnp.linspace(3.0, 6.0, M_HEADS)
    zh = jnp.zeros((M_HEADS,), jnp.float32)
    gate_base = jnp.stack([zh, f_bias, zh, f_bias])
    return {
        'x': nrm(ks[0], (BATCH, SEQ, D_MODEL), 1.0),
        'c': nrm(ks[1], (BATCH, D_MODEL), 1.0),
        'ctx': nrm(ks[2], (BATCH, CTX_LEN, D_MODEL), 1.0),
        'c_ctx': nrm(ks[3], (D_MODEL,), 1.0),
        'mod_w': nrm(ks[4], (DEPTH, D_MODEL, N_MOD * D_MODEL), 0.5 * D_MODEL ** -0.5),
        'mod_b': nrm(ks[5], (DEPTH, N_MOD * D_MODEL), 0.02),
        'norm1_g': 1.0 + nrm(ks[6], (DEPTH, D_MODEL), 0.1),
        'norm2_g': 1.0 + nrm(ks[7], (DEPTH, D_MODEL), 0.1),
        'w_in': nrm(ks[8], (DEPTH, D_MODEL, IN_COLS), D_MODEL ** -0.5),
        'gate_b': gate_base[None] + nrm(ks[9], (DEPTH, 4, M_HEADS), 0.1),
        'conv_w': nrm(ks[10], (DEPTH, CONV_W, 2 * QK_WIDTH), CONV_W ** -0.5),
        'conv_b': nrm(ks[11], (DEPTH, 2 * QK_WIDTH), 0.02),
        'mlstm_norm_g': 1.0 + nrm(ks[12], (DEPTH, V_WIDTH), 0.1),
        'sgu_norm_g': 1.0 + nrm(ks[13], (DEPTH, SGU_WIDTH), 0.1),
        'sgu_w': nrm(ks[14], (DEPTH, SGU_GROUPS, SGU_CHUNK, SGU_CHUNK), SGU_CHUNK ** -0.5),
        'sgu_b': 1.0 + nrm(ks[15], (DEPTH, SGU_GROUPS, SGU_CHUNK), 0.1),
        'proj_a': nrm(ks[16], (DEPTH, V_WIDTH, D_MODEL), V_WIDTH ** -0.5),
        'proj_b': nrm(ks[17], (DEPTH, SGU_WIDTH, D_MODEL), SGU_WIDTH ** -0.5),
        'w_out': nrm(ks[18], (DEPTH, D_MODEL, D_MODEL), D_MODEL ** -0.5),
        'router_w': nrm(ks[19], (DEPTH, D_MODEL, N_EXPERTS), D_MODEL ** -0.5),
        'router_b': nrm(ks[20], (DEPTH, N_EXPERTS), 0.01),
        'exp_w1': nrm(ks[21], (DEPTH, N_EXPERTS, D_MODEL, 2 * D_EXPERT), D_MODEL ** -0.5),
        'exp_b1': nrm(ks[22], (DEPTH, N_EXPERTS, 2 * D_EXPERT), 0.01),
        'exp_w2': nrm(ks[23], (DEPTH, N_EXPERTS, D_EXPERT, D_MODEL), D_EXPERT ** -0.5),
        'exp_b2': nrm(ks[24], (DEPTH, N_EXPERTS, D_MODEL), 0.01),
        'final_g': 1.0 + nrm(ks[25], (D_MODEL,), 0.1),
    }


def reference(x, c, ctx, c_ctx, mod_w, mod_b, norm1_g, norm2_g, w_in, gate_b, conv_w, conv_b,
              mlstm_norm_g, sgu_norm_g, sgu_w, sgu_b, proj_a, proj_b, w_out, router_w, router_b,
              exp_w1, exp_b1, exp_w2, exp_b2, final_g):
    B, T, _ = x.shape
    rows = T // GRID_W
    n_lat_chunks = rows // ROWS_PER_CHUNK
    for l in range(DEPTH):
        last = l == DEPTH - 1
        mod_lat = (jax.nn.silu(c) @ mod_w[l] + mod_b[l]).reshape(B, 1, N_MOD, D_MODEL)
        mod_ctx = (jax.nn.silu(c_ctx) @ mod_w[l] + mod_b[l]).reshape(1, 1, N_MOD, D_MODEL)
        h_lat = modulate(rmsnorm(x, norm1_g[l]), mod_lat[:, :, 0], mod_lat[:, :, 1])
        h_ctx = modulate(rmsnorm(ctx, norm1_g[l]), mod_ctx[:, :, 0], mod_ctx[:, :, 1])
        out_lat, out_ctx = token_mixer(h_lat, h_ctx, n_lat_chunks, not last, w_in[l], gate_b[l], conv_w[l],
                                       conv_b[l], mlstm_norm_g[l], sgu_norm_g[l], sgu_w[l], sgu_b[l],
                                       proj_a[l], proj_b[l], w_out[l])
        x = x + mod_lat[:, :, 2] * out_lat
        h2 = modulate(rmsnorm(x, norm2_g[l]), mod_lat[:, :, 3], mod_lat[:, :, 4])
        y2 = moe_ffn(h2.reshape(B * T, D_MODEL), router_w[l], router_b[l], exp_w1[l], exp_b1[l],
                     exp_w2[l], exp_b2[l]).reshape(B, T, D_MODEL)
        x = x + mod_lat[:, :, 5] * y2
        if not last:
            Tc = ctx.shape[1]
            ctx = ctx + mod_ctx[:, :, 2] * out_ctx
            h2c = modulate(rmsnorm(ctx, norm2_g[l]), mod_ctx[:, :, 3], mod_ctx[:, :, 4])
            y2c = moe_ffn(h2c.reshape(B * Tc, D_MODEL), router_w[l], router_b[l], exp_w1[l], exp_b1[l],
                          exp_w2[l], exp_b2[l]).reshape(B, Tc, D_MODEL)
            ctx = ctx + mod_ctx[:, :, 5] * y2c
    return rmsnorm(x, final_g)
```

```python
import functools

import jax
import jax.numpy as jnp
from jax import lax
from jax.experimental import pallas as pl
from jax.experimental.pallas import tpu as pltpu

F32 = jnp.float32
BF16 = jnp.bfloat16
I32 = jnp.int32
U32 = jnp.uint32

EPS = 1e-6
M_HEADS = 8
QK_DIM = 128
V_DIM = 256
CONV_W = 5
CHUNK = 128
SGU_GROUPS = 8
N_EXPERTS = 32
TOP_K = 4
SWIGLU_ALPHA = 1.702
SWIGLU_LIMIT = 7.0
N_MOD = 6

V7X_VMEM_LIMIT_BYTES = 56 * 1024 * 1024
LANES = 128
MOE_TILE = 512
MOE_HALF = 256
MOE_TN = 1024


def _cparams(sem):
    return pltpu.CompilerParams(dimension_semantics=sem, vmem_limit_bytes=V7X_VMEM_LIMIT_BYTES)


def _sigmoid(x):
    return 1.0 / (1.0 + jnp.exp(-x))


def _gelu_tanh(x):
    return 0.5 * x * (1.0 + jnp.tanh(0.7978845608028654 * (x + 0.044715 * (x * x * x))))


def _split3(a):
    a1 = a.astype(BF16)
    r1 = a - a1.astype(F32)
    a2 = r1.astype(BF16)
    a3 = (r1 - a2.astype(F32)).astype(BF16)
    return a1, a2, a3


def _pack_pair(lo, hi):
    lo_b = lax.bitcast_convert_type(lo.astype(BF16).astype(F32), U32)
    hi_b = lax.bitcast_convert_type(hi.astype(BF16).astype(F32), U32)
    return (lo_b >> 16) | hi_b


def _unpack_pair(w):
    lo = lax.bitcast_convert_type(w << 16, F32)
    hi = lax.bitcast_convert_type(w & jnp.uint32(0xFFFF0000), F32)
    return lo, hi


def _mod_kernel(s_ref, w_ref, b_ref, o_ref):
    s = s_ref[...]
    s = s * _sigmoid(s)
    o_ref[...] = jnp.dot(s.astype(BF16), w_ref[...].astype(BF16), preferred_element_type=F32) + b_ref[...]


def _mod(cc, mod_w, mod_b):
    d, n = mod_w.shape
    tn = 1024
    return pl.pallas_call(
        _mod_kernel,
        grid=(n // tn,),
        in_specs=[pl.BlockSpec((8, d), lambda j: (0, 0)),
                  pl.BlockSpec((d, tn), lambda j: (0, j)),
                  pl.BlockSpec((1, tn), lambda j: (0, j))],
        out_specs=pl.BlockSpec((8, tn), lambda j: (0, j)),
        out_shape=jax.ShapeDtypeStruct((8, n), F32),
        compiler_params=_cparams(("arbitrary",)),
        name="mod",
    )(cc, mod_w, mod_b.reshape(1, n))


def _inproj_kernel(x_ref, shift_ref, scale_ref, g_ref, wg_ref, gb_ref, w_ref, z_ref, gates_ref, h_scr, *, tiles_per_seg):
    j = pl.program_id(1)

    @pl.when(j == 0)
    def _():
        x = x_ref[...]
        ms = jnp.mean(x * x, axis=-1, keepdims=True)
        h = (x * lax.rsqrt(ms + EPS) * g_ref[...]) * (1.0 + scale_ref[...]) + shift_ref[...]
        h1, h2, _ = _split3(h)
        h_scr[...] = h1
        w1, w2, _ = _split3(wg_ref[...])
        gates_ref[...] = (jnp.dot(h1, w1, preferred_element_type=F32)
                          + jnp.dot(h1, w2, preferred_element_type=F32)
                          + jnp.dot(h2, w1, preferred_element_type=F32)) + gb_ref[...]

    z = jnp.dot(h_scr[...], w_ref[...], preferred_element_type=F32)
    seg = j // tiles_per_seg

    @pl.when(seg < 2)
    def _():
        z_ref[...] = z.astype(BF16)

    @pl.when((seg == 3) | (seg == 4))
    def _():
        z_ref[...] = _gelu_tanh(z).astype(BF16)

    @pl.when((seg == 2) | (seg >= 5))
    def _():
        z_ref[...] = _sigmoid(z).astype(BF16)


def _inproj(x2d, mod3, mod_row_of_tile, norm_g, w_gates, gate_b, w_main, n_cols, tm):
    r, d = x2d.shape
    tn = 512
    kern = functools.partial(_inproj_kernel, tiles_per_seg=d // tn)
    return pl.pallas_call(
        kern,
        grid=(r // tm, n_cols // tn),
        in_specs=[pl.BlockSpec((tm, d), lambda i, j: (i, 0)),
                  pl.BlockSpec((None, 1, d), lambda i, j: (mod_row_of_tile(i), 0, 0)),
                  pl.BlockSpec((None, 1, d), lambda i, j: (mod_row_of_tile(i), 0, 1)),
                  pl.BlockSpec((1, d), lambda i, j: (0, 0)),
                  pl.BlockSpec((d, LANES), lambda i, j: (0, 0)),
                  pl.BlockSpec((1, LANES), lambda i, j: (0, 0)),
                  pl.BlockSpec((d, tn), lambda i, j: (0, j))],
        out_specs=[pl.BlockSpec((tm, tn), lambda i, j: (i, j)),
                   pl.BlockSpec((tm, LANES), lambda i, j: (i, 0))],
        out_shape=[jax.ShapeDtypeStruct((r, n_cols), BF16),
                   jax.ShapeDtypeStruct((r, LANES), F32)],
        scratch_shapes=[pltpu.VMEM((tm, d), BF16)],
        compiler_params=_cparams(("arbitrary", "arbitrary")),
        name="inproj",
    )(x2d, mod3, mod3, norm_g, w_gates, gate_b, w_main)


def _conv_kernel(x_ref, w_ref, b_ref, o_ref, *, q_tiles):
    c = pl.program_id(1)
    x = x_ref[...].astype(F32)
    t = x.shape[0]
    rows = lax.broadcasted_iota(I32, x.shape, 0)
    w = w_ref[...]
    acc = x * w[CONV_W // 2:CONV_W // 2 + 1, :] + b_ref[...]
    for dlt in range(-(CONV_W // 2), CONV_W // 2 + 1):
        if dlt == 0:
            continue
        xs = pltpu.roll(x, shift=(-dlt) % t, axis=0)
        valid = (rows + dlt >= 0) & (rows + dlt < t)
        acc = acc + jnp.where(valid, xs, 0.0) * w[dlt + CONV_W // 2:dlt + CONV_W // 2 + 1, :]
    y = acc * _sigmoid(acc)
    scale = jnp.where(c < q_tiles, QK_DIM ** -0.5, 1.0).astype(F32)
    o_ref[...] = (y * scale).astype(BF16)


def _conv(z3, conv_w, conv_b):
    b, t, _ = z3.shape
    width = conv_w.shape[1]
    tc = 256
    kern = functools.partial(_conv_kernel, q_tiles=(width // 2) // tc)
    return pl.pallas_call(
        kern,
        grid=(b, width // tc),
        in_specs=[pl.BlockSpec((None, t, tc), lambda i, c: (i, 0, c)),
                  pl.BlockSpec((CONV_W, tc), lambda i, c: (0, c)),
                  pl.BlockSpec((1, tc), lambda i, c: (0, c))],
        out_specs=pl.BlockSpec((None, t, tc), lambda i, c: (i, 0, c)),
        out_shape=jax.ShapeDtypeStruct((b, t, width), BF16),
        compiler_params=_cparams(("arbitrary", "arbitrary")),
        name="conv",
    )(z3, conv_w, conv_b.reshape(1, width))


def _mlstm_kernel(qkc_f, qkl_f, vc_f, vl_f, gc_f, gl_f, qkc_b, qkl_b, vc_b, vl_b, gc_b, gl_b,
                  hf_ref, hb_ref, c_scr, m_scr, *, n_ctx_chunks):
    s = pl.program_id(1)

    @pl.when(s == 0)
    def _():
        c_scr[...] = jnp.zeros_like(c_scr)
        m_scr[...] = jnp.zeros_like(m_scr)

    is_ctx = s < n_ctx_chunks
    rows = lax.broadcasted_iota(I32, (CHUNK, CHUNK), 0)
    lanes = lax.broadcasted_iota(I32, (CHUNK, CHUNK), 1)
    ones_col = jnp.where(lanes == 0, 1.0, 0.0).astype(BF16)
    dn_nt = (((1,), (1,)), ((), ()))
    dn_tn = (((0,), (0,)), ((), ()))
    qk_w = M_HEADS * QK_DIM

    dirs = ((qkc_f, qkl_f, vc_f, vl_f, gc_f, gl_f, hf_ref), (qkc_b, qkl_b, vc_b, vl_b, gc_b, gl_b, hb_ref))
    for d, (qkc, qkl, vc, vl, gc, gl, out_ref) in enumerate(dirs):
        qk = jnp.where(is_ctx, qkc[...], qkl[...])
        v = jnp.where(is_ctx, vc[...], vl[...])
        g = jnp.where(is_ctx, gc[...], gl[...])
        mask = (lanes <= rows) if d == 0 else (lanes >= rows)
        tri = jnp.where(mask, 1.0, 0.0).astype(BF16)
        ls = jnp.minimum(g, 0.0) - jnp.log(1.0 + jnp.exp(-jnp.abs(g)))
        l1, l2, l3 = _split3(ls)
        bcol = (jnp.dot(tri, l1, preferred_element_type=F32) + jnp.dot(tri, l2, preferred_element_type=F32)
                + jnp.dot(tri, l3, preferred_element_type=F32))
        brow = bcol.T
        grow = g.T
        for h in range(M_HEADS):
            col_i = 2 * M_HEADS * d + h
            col_f = col_i + M_HEADS
            idx = d * M_HEADS + h
            bc = bcol[:, col_f:col_f + 1]
            br = brow[col_f:col_f + 1, :]
            ir = grow[col_i:col_i + 1, :]
            ic = g[:, col_i:col_i + 1]
            m = m_scr[idx][0:1, 0:1]
            dlog = jnp.where(mask, bc - br + ir, -jnp.inf)
            inter = bc + m
            m_t = jnp.maximum(inter, jnp.max(dlog, axis=1, keepdims=True))
            w_inter = jnp.exp(inter - m_t)
            q = qk[:, h * QK_DIM:(h + 1) * QK_DIM]
            k = qk[:, qk_w + h * QK_DIM:qk_w + (h + 1) * QK_DIM]
            vaug = jnp.concatenate([v[:, h * V_DIM:(h + 1) * V_DIM], ones_col], axis=1)
            sc = lax.dot_general(q, k, dn_nt, preferred_element_type=F32) * jnp.exp(dlog - m_t)
            caug = c_scr[idx]
            num = (w_inter * jnp.dot(q, caug.astype(BF16), preferred_element_type=F32)
                   + jnp.dot(sc.astype(BF16), vaug, preferred_element_type=F32))
            den = num[:, V_DIM:V_DIM + 1]
            hout = num[:, :V_DIM] / jnp.maximum(jnp.abs(den), jnp.exp(-m_t))
            b_end = br[:, CHUNK - 1:CHUNK] if d == 0 else br[:, 0:1]
            m_new = jnp.maximum(b_end + m, jnp.max(b_end - br + ir, axis=1, keepdims=True))
            decay = jnp.exp(b_end + m - m_new)
            w_col = jnp.exp(b_end - bc + ic - m_new)
            kw = (k.astype(F32) * w_col).astype(BF16)
            c_scr[idx] = decay * caug + lax.dot_general(kw, vaug, dn_tn, preferred_element_type=F32)
            m_scr[idx] = jnp.broadcast_to(m_new, m_scr.shape[1:])

            @pl.when(jnp.logical_not(is_ctx))
            def _():
                out_ref[:, h * V_DIM:(h + 1) * V_DIM] = hout


def _mlstm(qk_ctx, qk_lat, z_ctx3, z_lat3, g_ctx3, g_lat3):
    b, tc, _ = qk_ctx.shape
    t = qk_lat.shape[1]
    nc, nl = tc // CHUNK, t // CHUNK
    vw = M_HEADS * V_DIM
    qkw = 2 * M_HEADS * QK_DIM

    f_ctx = lambda i, s: (i, jnp.minimum(s, nc - 1), 0)
    f_lat = lambda i, s: (i, jnp.clip(s - nc, 0, nl - 1), 0)
    b_ctx = lambda i, s: (i, jnp.maximum(nc - 1 - s, 0), 0)
    b_lat = lambda i, s: (i, jnp.clip(nc + nl - 1 - s, 0, nl - 1), 0)
    v_of = lambda f: (lambda i, s: (f(i, s)[0], f(i, s)[1], 1))

    def specs(fc, fl):
        return [pl.BlockSpec((None, CHUNK, qkw), fc), pl.BlockSpec((None, CHUNK, qkw), fl),
                pl.BlockSpec((None, CHUNK, vw), v_of(fc)), pl.BlockSpec((None, CHUNK, vw), v_of(fl)),
                pl.BlockSpec((None, CHUNK, LANES), fc), pl.BlockSpec((None, CHUNK, LANES), fl)]

    kern = functools.partial(_mlstm_kernel, n_ctx_chunks=nc)
    args = (qk_ctx, qk_lat, z_ctx3, z_lat3, g_ctx3, g_lat3)
    return pl.pallas_call(
        kern,
        grid=(b, nc + nl),
        in_specs=specs(f_ctx, f_lat) + specs(b_ctx, b_lat),
        out_specs=[pl.BlockSpec((None, CHUNK, vw), f_lat), pl.BlockSpec((None, CHUNK, vw), b_lat)],
        out_shape=[jax.ShapeDtypeStruct((b, t, vw), F32), jax.ShapeDtypeStruct((b, t, vw), F32)],
        scratch_shapes=[pltpu.VMEM((2 * M_HEADS, QK_DIM, V_DIM + LANES), F32),
                        pltpu.VMEM((2 * M_HEADS, 8, LANES), F32)],
        compiler_params=_cparams(("arbitrary", "arbitrary")),
        name="mlstm",
    )(*args, *args)


def _merge_kernel(hf_ref, hb_ref, o_ref, u_ref, sv_ref, ga_ref, gb_ref, gm_ref, gs_ref, sw_ref, sb_ref,
                  pa_ref, pb_ref, out_ref, a_scr, b_scr):
    j = pl.program_id(1)

    @pl.when(j == 0)
    def _():
        tm = hf_ref.shape[0]
        for h in range(M_HEADS):
            sl = slice(h * V_DIM, (h + 1) * V_DIM)
            hs = hf_ref[:, sl] + hb_ref[:, sl]
            ms = jnp.mean(hs * hs, axis=-1, keepdims=True)
            y = hs * lax.rsqrt(ms + EPS) * gm_ref[:, sl]
            a_scr[:, sl] = (y * o_ref[:, sl].astype(F32)).astype(BF16)
        sv = sv_ref[...].astype(F32)
        ms = jnp.mean(sv * sv, axis=-1, keepdims=True)
        svn = (sv * lax.rsqrt(ms + EPS) * gs_ref[...]).astype(BF16)
        gw = sv.shape[1] // SGU_GROUPS
        for c in range(tm // CHUNK):
            rs = slice(c * CHUNK, (c + 1) * CHUNK)
            for gi in range(SGU_GROUPS):
                cs = slice(gi * gw, (gi + 1) * gw)
                zz = jnp.dot(sw_ref[gi], svn[rs, cs], preferred_element_type=F32) + sb_ref[:, gi:gi + 1]
                b_scr[rs, cs] = (u_ref[rs, cs].astype(F32) * zz).astype(BF16)

    ya = jnp.dot(a_scr[...], pa_ref[...], preferred_element_type=F32)
    yb = jnp.dot(b_scr[...], pb_ref[...], preferred_element_type=F32)
    out_ref[...] = (ga_ref[...].astype(F32) * ya + gb_ref[...].astype(F32) * yb).astype(BF16)


def _merge(hf, hb, z, gm, gs, sgu_w, sgu_bt, pa, pb):
    r, d = hf.shape
    tm, tn = 256, 512
    seg = d // tn
    row = lambda k: (lambda i, j: (i, k))
    return pl.pallas_call(
        _merge_kernel,
        grid=(r // tm, d // tn),
        in_specs=[pl.BlockSpec((tm, d), row(0)), pl.BlockSpec((tm, d), row(0)),
                  pl.BlockSpec((tm, d), row(2)), pl.BlockSpec((tm, d), row(3)), pl.BlockSpec((tm, d), row(4)),
                  pl.BlockSpec((tm, tn), lambda i, j: (i, 5 * seg + j)),
                  pl.BlockSpec((tm, tn), lambda i, j: (i, 6 * seg + j)),
                  pl.BlockSpec((1, d), lambda i, j: (0, 0)), pl.BlockSpec((1, d), lambda i, j: (0, 0)),
                  pl.BlockSpec(sgu_w.shape, lambda i, j: (0, 0, 0)),
                  pl.BlockSpec(sgu_bt.shape, lambda i, j: (0, 0)),
                  pl.BlockSpec((d, tn), lambda i, j: (0, j)), pl.BlockSpec((d, tn), lambda i, j: (0, j))],
        out_specs=pl.BlockSpec((tm, tn), lambda i, j: (i, j)),
        out_shape=jax.ShapeDtypeStruct((r, d), BF16),
        scratch_shapes=[pltpu.VMEM((tm, d), BF16), pltpu.VMEM((tm, d), BF16)],
        compiler_params=_cparams(("arbitrary", "arbitrary")),
        name="merge",
    )(hf, hb, z, z, z, z, z, gm, gs, sgu_w, sgu_bt, pa, pb)


def _outproj_kernel(m_ref, w_ref, x_ref, gate_ref, shift_ref, scale_ref, g_ref, rw_ref, rb_ref,
                    x1_ref, h2_ref, lg_ref):
    out = jnp.dot(m_ref[...], w_ref[...], preferred_element_type=F32)
    x1 = x_ref[...] + gate_ref[...] * out
    x1_ref[...] = x1
    ms = jnp.mean(x1 * x1, axis=-1, keepdims=True)
    h2 = (x1 * lax.rsqrt(ms + EPS) * g_ref[...]) * (1.0 + scale_ref[...]) + shift_ref[...]
    hi, lo, _ = _split3(h2)
    r1, r2, _ = _split3(rw_ref[...])
    dn_nt = (((1,), (1,)), ((), ()))
    lg_ref[...] = (lax.dot_general(r1, hi, dn_nt, preferred_element_type=F32)
                   + lax.dot_general(r1, lo, dn_nt, preferred_element_type=F32)
                   + lax.dot_general(r2, hi, dn_nt, preferred_element_type=F32)) + rb_ref[...]
    half = h2.shape[1] // 2
    hf = hi.astype(F32)
    h2_ref[...] = _pack_pair(hf[:, :half], hf[:, half:])


def _outproj(merged, w_out, x2d, mod3, mod_row_of_tile, norm_g, rw_t, rb_col):
    r, d = x2d.shape
    tm = 256
    ne = rw_t.shape[0]
    modspec = lambda k: pl.BlockSpec((None, 1, d), lambda i: (mod_row_of_tile(i), 0, k))
    return pl.pallas_call(
        _outproj_kernel,
        grid=(r // tm,),
        in_specs=[pl.BlockSpec((tm, d), lambda i: (i, 0)),
                  pl.BlockSpec((d, d), lambda i: (0, 0)),
                  pl.BlockSpec((tm, d), lambda i: (i, 0)),
                  modspec(2), modspec(3), modspec(4),
                  pl.BlockSpec((1, d), lambda i: (0, 0)),
                  pl.BlockSpec((ne, d), lambda i: (0, 0)),
                  pl.BlockSpec((ne, 1), lambda i: (0, 0))],
        out_specs=[pl.BlockSpec((tm, d), lambda i: (i, 0)),
                   pl.BlockSpec((tm, d // 2), lambda i: (i, 0)),
                   pl.BlockSpec((ne, tm), lambda i: (0, i))],
        out_shape=[jax.ShapeDtypeStruct((r, d), F32),
                   jax.ShapeDtypeStruct((r, d // 2), U32),
                   jax.ShapeDtypeStruct((ne, r), F32)],
        compiler_params=_cparams(("arbitrary",)),
        name="outproj",
    )(merged, w_out, x2d, mod3, mod3, mod3, norm_g, rw_t, rb_col)


def _route_kernel(lg_ref, eid_ref, gate_ref, rank_ref, cnt_ref, carry_scr):
    i = pl.program_id(0)

    @pl.when(i == 0)
    def _():
        carry_scr[...] = jnp.zeros_like(carry_scr)

    l = lg_ref[...]
    ne, tm = l.shape
    e_iota = lax.broadcasted_iota(I32, (ne, tm), 0)
    vals, onehots = [], []
    for k in range(TOP_K):
        mx = jnp.max(l, axis=0, keepdims=True)
        idx = jnp.min(jnp.where(l == mx, e_iota, ne), axis=0, keepdims=True)
        oh = e_iota == idx
        l = jnp.where(oh, -jnp.inf, l)
        vals.append(mx)
        onehots.append(oh)
        eid_ref[k:k + 1, :] = idx
    ex = [jnp.exp(vk - vals[0]) for vk in vals]
    tot = ex[0] + ex[1] + ex[2] + ex[3]
    for k in range(TOP_K):
        gate_ref[k:k + 1, :] = ex[k] / tot

    oh_all = jnp.zeros((ne, tm), F32)
    for oh in onehots:
        oh_all = oh_all + jnp.where(oh, 1.0, 0.0)
    oh_all = oh_all.astype(BF16)
    r_i = lax.broadcasted_iota(I32, (LANES, LANES), 0)
    c_i = lax.broadcasted_iota(I32, (LANES, LANES), 1)
    tri_excl = jnp.where(r_i < c_i, 1.0, 0.0).astype(BF16)
    ones_m = jnp.ones((LANES, LANES), BF16)
    carry = carry_scr[...]
    for blk in range(tm // LANES):
        sl = slice(blk * LANES, (blk + 1) * LANES)
        ohb = oh_all[:, sl]
        cum = jnp.dot(ohb, tri_excl, preferred_element_type=F32) + carry
        for k in range(TOP_K):
            rk = jnp.sum(jnp.where(onehots[k][:, sl], cum, 0.0), axis=0, keepdims=True)
            rank_ref[k:k + 1, sl] = rk.astype(I32)
        carry = carry + jnp.dot(ohb, ones_m, preferred_element_type=F32)
    carry_scr[...] = carry
    cnt_ref[...] = carry


def _route(logits_t):
    ne, r = logits_t.shape
    tm = 1024
    return pl.pallas_call(
        _route_kernel,
        grid=(r // tm,),
        in_specs=[pl.BlockSpec((ne, tm), lambda i: (0, i))],
        out_specs=[pl.BlockSpec((TOP_K, tm), lambda i: (0, i)),
                   pl.BlockSpec((TOP_K, tm), lambda i: (0, i)),
                   pl.BlockSpec((TOP_K, tm), lambda i: (0, i)),
                   pl.BlockSpec((ne, LANES), lambda i: (0, 0))],
        out_shape=[jax.ShapeDtypeStruct((TOP_K, r), I32),
                   jax.ShapeDtypeStruct((TOP_K, r), F32),
                   jax.ShapeDtypeStruct((TOP_K, r), I32),
                   jax.ShapeDtypeStruct((ne, LANES), F32)],
        scratch_shapes=[pltpu.VMEM((ne, LANES), F32)],
        compiler_params=_cparams(("arbitrary",)),
        name="route",
    )(logits_t)


def _row_copy(src_ref, dst_ref, sidx_ref, didx_ref, sem, j):
    return pltpu.make_async_copy(src_ref.at[pl.ds(sidx_ref[j], 1), :], dst_ref.at[pl.ds(didx_ref[j], 1), :], sem)


def _permute_kernel(sidx_ref, didx_ref, src_ref, *rest, chunk):
    dst_ref, sem = rest[-2], rest[-1]
    base = pl.program_id(0) * chunk

    def start(i, carry):
        _row_copy(src_ref, dst_ref, sidx_ref, didx_ref, sem, base + i).start()
        return carry

    def wait(i, carry):
        _row_copy(src_ref, dst_ref, sidx_ref, didx_ref, sem, base + i).wait()
        return carry

    lax.fori_loop(0, chunk, start, 0)
    lax.fori_loop(0, chunk, wait, 0)


def _permute_rows(src, src_idx, dst_idx, n_dst_rows, init=None):
    n = src_idx.shape[0]
    chunk = 2048
    width = src.shape[1]
    any_spec = pl.BlockSpec(memory_space=pl.ANY)
    operands = [src_idx, dst_idx, src] + ([init] if init is not None else [])
    grid_spec = pltpu.PrefetchScalarGridSpec(
        num_scalar_prefetch=2,
        grid=(n // chunk,),
        in_specs=[any_spec] * (len(operands) - 2),
        out_specs=any_spec,
        scratch_shapes=[pltpu.SemaphoreType.DMA(())],
    )
    return pl.pallas_call(
        functools.partial(_permute_kernel, chunk=chunk),
        grid_spec=grid_spec,
        out_shape=jax.ShapeDtypeStruct((n_dst_rows, width), src.dtype),
        input_output_aliases={3: 0} if init is not None else {},
        compiler_params=_cparams(("arbitrary",)),
        name="permute_rows",
    )(*operands)


def _moe_up_kernel(blk_ref, oblk_ref, e_ref, nv_ref, new_ref, xs_ref, wg_ref, wl_ref, bg_ref, bl_ref, act_ref,
                   wg_bf, wl_bf):
    it = pl.program_id(1)

    @pl.when(new_ref[it] == 1)
    def _():
        wg_bf[...] = wg_ref[...].astype(BF16)
        wl_bf[...] = wl_ref[...].astype(BF16)

    nv = nv_ref[it]
    for half in range(MOE_TILE // MOE_HALF):
        rs = slice(half * MOE_HALF, (half + 1) * MOE_HALF)

        @pl.when(nv > half * MOE_HALF)
        def _():
            lo, hi = _unpack_pair(xs_ref[rs, :])
            x = jnp.concatenate([lo.astype(BF16), hi.astype(BF16)], axis=1)
            hg = jnp.dot(x, wg_bf[...], preferred_element_type=F32) + bg_ref[...]
            hl = jnp.dot(x, wl_bf[...], preferred_element_type=F32) + bl_ref[...]
            glu = jnp.minimum(hg, SWIGLU_LIMIT)
            lin = jnp.clip(hl, -SWIGLU_LIMIT, SWIGLU_LIMIT)
            act_ref[rs, :] = (glu * _sigmoid(SWIGLU_ALPHA * glu) * (lin + 1.0)).astype(BF16)

        @pl.when(nv <= half * MOE_HALF)
        def _():
            act_ref[rs, :] = jnp.zeros((MOE_HALF, act_ref.shape[1]), BF16)


def _moe_down_kernel(blk_ref, oblk_ref, e_ref, nv_ref, new_ref, a_ref, w_ref, b_ref, y_ref, w_bf):
    it = pl.program_id(1)

    @pl.when(new_ref[it] == 1)
    def _():
        w_bf[...] = w_ref[...].astype(BF16)

    nv = nv_ref[it]
    hw = y_ref.shape[1]
    for half in range(MOE_TILE // MOE_HALF):
        rs = slice(half * MOE_HALF, (half + 1) * MOE_HALF)

        @pl.when(nv > half * MOE_HALF)
        def _():
            y = jnp.dot(a_ref[rs, :], w_bf[...], preferred_element_type=F32) + b_ref[...]
            y_ref[rs, :] = _pack_pair(y[:, :hw], y[:, hw:])

        @pl.when(nv <= half * MOE_HALF)
        def _():
            y_ref[rs, :] = jnp.zeros((MOE_HALF, hw), U32)


def _moe_up(items, xs, w1, b1):
    n_items = items[0].shape[0]
    ne, d, two_f = w1.shape
    f = two_f // 2
    nj = f // MOE_TN
    p = xs.shape[0]
    b1r = b1.reshape(ne, 1, two_f)
    grid_spec = pltpu.PrefetchScalarGridSpec(
        num_scalar_prefetch=len(items),
        grid=(nj, n_items),
        in_specs=[pl.BlockSpec((MOE_TILE, d // 2), lambda j, it, blk, oblk, e, nv, new: (blk[it], 0)),
                  pl.BlockSpec((None, d, MOE_TN), lambda j, it, blk, oblk, e, nv, new: (e[it], 0, j)),
                  pl.BlockSpec((None, d, MOE_TN), lambda j, it, blk, oblk, e, nv, new: (e[it], 0, nj + j)),
                  pl.BlockSpec((None, 1, MOE_TN), lambda j, it, blk, oblk, e, nv, new: (e[it], 0, j)),
                  pl.BlockSpec((None, 1, MOE_TN), lambda j, it, blk, oblk, e, nv, new: (e[it], 0, nj + j))],
        out_specs=pl.BlockSpec((MOE_TILE, MOE_TN), lambda j, it, blk, oblk, e, nv, new: (oblk[it], j)),
        scratch_shapes=[pltpu.VMEM((d, MOE_TN), BF16), pltpu.VMEM((d, MOE_TN), BF16)],
    )
    return pl.pallas_call(
        _moe_up_kernel,
        grid_spec=grid_spec,
        out_shape=jax.ShapeDtypeStruct((p, f), BF16),
        compiler_params=_cparams(("arbitrary", "arbitrary")),
        name="moe_up",
    )(*items, xs, w1, w1, b1r, b1r)


def _moe_down(items, act, w2, b2):
    n_items = items[0].shape[0]
    ne, f, d = w2.shape
    nj = d // MOE_TN
    p = act.shape[0]
    b2r = b2.reshape(ne, 1, d)
    grid_spec = pltpu.PrefetchScalarGridSpec(
        num_scalar_prefetch=len(items),
        grid=(nj, n_items),
        in_specs=[pl.BlockSpec((MOE_TILE, f), lambda j, it, blk, oblk, e, nv, new: (blk[it], 0)),
                  pl.BlockSpec((None, f, MOE_TN), lambda j, it, blk, oblk, e, nv, new: (e[it], 0, j)),
                  pl.BlockSpec((None, 1, MOE_TN), lambda j, it, blk, oblk, e, nv, new: (e[it], 0, j))],
        out_specs=pl.BlockSpec((MOE_TILE, MOE_TN // 2), lambda j, it, blk, oblk, e, nv, new: (oblk[it], j)),
        scratch_shapes=[pltpu.VMEM((f, MOE_TN), BF16)],
    )
    return pl.pallas_call(
        _moe_down_kernel,
        grid_spec=grid_spec,
        out_shape=jax.ShapeDtypeStruct((p, d // 2), U32),
        compiler_params=_cparams(("arbitrary", "arbitrary")),
        name="moe_down",
    )(*items, act, w2, b2r)


def _final_kernel(yk_ref, gates_ref, x1_ref, gate_ref, g_ref, o_ref):
    tm, d = x1_ref.shape
    q = d // 4
    acc = jnp.zeros((tm, d), F32)
    for k in range(TOP_K):
        w = yk_ref[k]
        lo0, hi0 = _unpack_pair(w[:, :q])
        lo1, hi1 = _unpack_pair(w[:, q:])
        y = jnp.concatenate([lo0, hi0, lo1, hi1], axis=1)
        acc = acc + gates_ref[:, k:k + 1] * y
    xf = x1_ref[...] + gate_ref[...] * acc
    ms = jnp.mean(xf * xf, axis=-1, keepdims=True)
    o_ref[...] = xf * lax.rsqrt(ms + EPS) * g_ref[...]


def _final(yk, gates_tk, x1, mod3, mod_row_of_tile, final_g):
    r, d = x1.shape
    tm = 256
    return pl.pallas_call(
        _final_kernel,
        grid=(r // tm,),
        in_specs=[pl.BlockSpec((TOP_K, tm, d // 2), lambda i: (0, i, 0)),
                  pl.BlockSpec((tm, TOP_K), lambda i: (i, 0)),
                  pl.BlockSpec((tm, d), lambda i: (i, 0)),
                  pl.BlockSpec((None, 1, d), lambda i: (mod_row_of_tile(i), 0, 5)),
                  pl.BlockSpec((1, d), lambda i: (0, 0))],
        out_specs=pl.BlockSpec((tm, d), lambda i: (i, 0)),
        out_shape=jax.ShapeDtypeStruct((r, d), F32),
        compiler_params=_cparams(("arbitrary",)),
        name="final",
    )(yk, gates_tk, x1, mod3, final_g)


def _moe_schedule(counts, n_items):
    ne = counts.shape[0]
    padded = (counts + MOE_TILE - 1) // MOE_TILE * MOE_TILE
    pad_end = jnp.cumsum(padded)
    start_pad = pad_end - padded
    n_real = pad_end[-1] // MOE_TILE
    b = jnp.arange(n_items, dtype=I32)
    blk = jnp.minimum(b, jnp.maximum(n_real - 1, 0))
    blk_start = blk * MOE_TILE
    e_of = jnp.minimum(jnp.sum((pad_end[None, :] <= blk_start[:, None]).astype(I32), axis=1), ne - 1)
    nv = jnp.clip(counts[e_of] - (blk_start - start_pad[e_of]), 0, MOE_TILE)
    nv = jnp.where(b < n_real, nv, 0)
    prev_e = jnp.concatenate([jnp.full((1,), -1, I32), e_of[:-1]])
    new = ((e_of != prev_e) | (b == 0)).astype(I32)
    return start_pad, (blk.astype(I32), b, e_of.astype(I32), nv.astype(I32), new)


def kernel(x, c, ctx, c_ctx, mod_w, mod_b, norm1_g, norm2_g, w_in, gate_b, conv_w, conv_b, mlstm_norm_g,
           sgu_norm_g, sgu_w, sgu_b, proj_a, proj_b, w_out, router_w, router_b, exp_w1, exp_b1, exp_w2,
           exp_b2, final_g):
    bsz, t, d = x.shape
    tc = ctx.shape[1]
    depth = mod_w.shape[0]
    assert depth == 1, "single-layer block"
    l = 0
    r = bsz * t
    qkw = 2 * M_HEADS * QK_DIM
    vw = M_HEADS * V_DIM
    off_gates = qkw + vw
    n_gates = 4 * M_HEADS
    off_o = off_gates + n_gates

    cc = jnp.zeros((8, d), F32).at[:bsz].set(c).at[bsz].set(c_ctx)
    mod3 = _mod(cc, mod_w[l], mod_b[l]).reshape(8, 1, N_MOD * d)

    w_l = w_in[l]
    w_main = jnp.concatenate([w_l[:, :off_gates], w_l[:, off_o:]], axis=1).astype(BF16)
    w_gates = jnp.pad(w_l[:, off_gates:off_o], ((0, 0), (0, LANES - n_gates)))
    gb = jnp.pad(gate_b[l].reshape(1, n_gates), ((0, 0), (0, LANES - n_gates)))
    g1 = norm1_g[l].reshape(1, d)

    tm_in = 1024
    lat_row = lambda i: i // (t // tm_in)
    z_lat, g_lat = _inproj(x.reshape(r, d), mod3, lat_row, g1, w_gates, gb, w_main, w_main.shape[1], tm_in)
    z_ctx, g_ctx = _inproj(ctx.reshape(bsz * tc, d), mod3, lambda i: bsz, g1, w_gates, gb, w_main, off_gates, tc)

    z_lat3 = z_lat.reshape(bsz, t, z_lat.shape[1])
    z_ctx3 = z_ctx.reshape(bsz, tc, z_ctx.shape[1])
    qk_lat = _conv(z_lat3, conv_w[l], conv_b[l])
    qk_ctx = _conv(z_ctx3, conv_w[l], conv_b[l])

    hf, hb = _mlstm(qk_ctx, qk_lat, z_ctx3, z_lat3, g_ctx.reshape(bsz, tc, LANES), g_lat.reshape(bsz, t, LANES))

    merged = _merge(hf.reshape(r, vw), hb.reshape(r, vw), z_lat,
                    mlstm_norm_g[l].reshape(1, vw), sgu_norm_g[l].reshape(1, -1),
                    sgu_w[l].astype(BF16), sgu_b[l].T, proj_a[l].astype(BF16), proj_b[l].astype(BF16))

    tm_out = 256
    x1, h2p, logits_t = _outproj(merged, w_out[l].astype(BF16), x.reshape(r, d), mod3,
                                 lambda i: i // (t // tm_out), norm2_g[l].reshape(1, d),
                                 router_w[l].T, router_b[l].reshape(-1, 1))

    eid, gates, rank, cnt = _route(logits_t)
    counts = cnt[:, 0].astype(I32)
    n_items = (r * TOP_K) // MOE_TILE + N_EXPERTS
    start_pad, items = _moe_schedule(counts, n_items)
    slot = (start_pad[eid] + rank).reshape(-1)
    tok = jnp.tile(jnp.arange(r, dtype=I32), TOP_K)
    p_rows = n_items * MOE_TILE

    xs = _permute_rows(h2p, tok, slot, p_rows, init=jnp.zeros((p_rows, d // 2), U32))
    act = _moe_up(items, xs, exp_w1[l], exp_b1[l])
    y = _moe_down(items, act, exp_w2[l], exp_b2[l])
    yk = _permute_rows(y, slot, jnp.arange(r * TOP_K, dtype=I32), r * TOP_K)

    out = _final(yk.reshape(TOP_K, r, d // 2), gates.T, x1, mod3, lambda i: i // (t // tm_out), final_g.reshape(1, d))
    return out.reshape(bsz, t, d)
```

```python
import functools

import jax
import jax.numpy as jnp
from jax import lax
from jax.experimental import pallas as pl
from jax.experimental.pallas import tpu as pltpu

F32 = jnp.float32
BF16 = jnp.bfloat16
I32 = jnp.int32
U32 = jnp.uint32

EPS = 1e-6
M_HEADS = 8
QK_DIM = 128
V_DIM = 256
CONV_W = 5
CHUNK = 128
SGU_GROUPS = 8
N_EXPERTS = 32
TOP_K = 4
SWIGLU_ALPHA = 1.702
SWIGLU_LIMIT = 7.0
N_MOD = 6

V7X_VMEM_LIMIT_BYTES = 56 * 1024 * 1024
LANES = 128
MOE_TILE = 512
MOE_HALF = 256
MOE_TN = 1024


def _cparams(sem):
    return pltpu.CompilerParams(dimension_semantics=sem, vmem_limit_bytes=V7X_VMEM_LIMIT_BYTES)


def _sigmoid(x):
    return 1.0 / (1.0 + jnp.exp(-x))


def _gelu_tanh(x):
    return 0.5 * x * (1.0 + jnp.tanh(0.7978845608028654 * (x + 0.044715 * (x * x * x))))


def _split3(a):
    a1 = a.astype(BF16)
    r1 = a - a1.astype(F32)
    a2 = r1.astype(BF16)
    a3 = (r1 - a2.astype(F32)).astype(BF16)
    return a1, a2, a3


def _pack_pair(lo, hi):
    lo_b = lax.bitcast_convert_type(lo.astype(BF16).astype(F32), U32)
    hi_b = lax.bitcast_convert_type(hi.astype(BF16).astype(F32), U32)
    return (lo_b >> 16) | hi_b


def _unpack_pair(w):
    lo = lax.bitcast_convert_type(w << 16, F32)
    hi = lax.bitcast_convert_type(w & jnp.uint32(0xFFFF0000), F32)
    return lo, hi


TOKEN_ROWS = 8


def _store_token_major(ref, row0, words):
    n = words.shape[0]
    for s in range(TOKEN_ROWS):
        ref[pl.ds(row0 * TOKEN_ROWS + s, n, stride=TOKEN_ROWS), :] = words[:, s * LANES:(s + 1) * LANES]


def _load_token_major(ref, row0, n, lead=None):
    out = []
    for s in range(TOKEN_ROWS):
        idx = pl.ds(row0 * TOKEN_ROWS + s, n, stride=TOKEN_ROWS)
        out.append(ref[idx, :] if lead is None else ref[lead, idx, :])
    return out


def _mod_kernel(s_ref, w_ref, b_ref, o_ref):
    s = s_ref[...]
    s = s * _sigmoid(s)
    o_ref[...] = jnp.dot(s.astype(BF16), w_ref[...].astype(BF16), preferred_element_type=F32) + b_ref[...]


def _mod(cc, mod_w, mod_b):
    d, n = mod_w.shape
    tn = 1024
    return pl.pallas_call(
        _mod_kernel,
        grid=(n // tn,),
        in_specs=[pl.BlockSpec((8, d), lambda j: (0, 0)),
                  pl.BlockSpec((d, tn), lambda j: (0, j)),
                  pl.BlockSpec((1, tn), lambda j: (0, j))],
        out_specs=pl.BlockSpec((8, tn), lambda j: (0, j)),
        out_shape=jax.ShapeDtypeStruct((8, n), F32),
        compiler_params=_cparams(("arbitrary",)),
        name="mod",
    )(cc, mod_w, mod_b.reshape(1, n))


def _inproj_kernel(x_ref, shift_ref, scale_ref, g_ref, wg_ref, gb_ref, w_ref, z_ref, gates_ref, h_scr, *, tiles_per_seg):
    j = pl.program_id(1)

    @pl.when(j == 0)
    def _():
        x = x_ref[...]
        ms = jnp.mean(x * x, axis=-1, keepdims=True)
        h = (x * lax.rsqrt(ms + EPS) * g_ref[...]) * (1.0 + scale_ref[...]) + shift_ref[...]
        h1, h2, _ = _split3(h)
        h_scr[...] = h1
        w1, w2, _ = _split3(wg_ref[...])
        gates_ref[...] = (jnp.dot(h1, w1, preferred_element_type=F32)
                          + jnp.dot(h1, w2, preferred_element_type=F32)
                          + jnp.dot(h2, w1, preferred_element_type=F32)) + gb_ref[...]

    z = jnp.dot(h_scr[...], w_ref[...], preferred_element_type=F32)
    seg = j // tiles_per_seg

    @pl.when(seg < 2)
    def _():
        z_ref[...] = z.astype(BF16)

    @pl.when((seg == 3) | (seg == 4))
    def _():
        z_ref[...] = _gelu_tanh(z).astype(BF16)

    @pl.when((seg == 2) | (seg >= 5))
    def _():
        z_ref[...] = _sigmoid(z).astype(BF16)


def _inproj(x2d, mod3, mod_row_of_tile, norm_g, w_gates, gate_b, w_main, n_cols, tm):
    r, d = x2d.shape
    tn = 512
    kern = functools.partial(_inproj_kernel, tiles_per_seg=d // tn)
    return pl.pallas_call(
        kern,
        grid=(r // tm, n_cols // tn),
        in_specs=[pl.BlockSpec((tm, d), lambda i, j: (i, 0)),
                  pl.BlockSpec((None, 1, d), lambda i, j: (mod_row_of_tile(i), 0, 0)),
                  pl.BlockSpec((None, 1, d), lambda i, j: (mod_row_of_tile(i), 0, 1)),
                  pl.BlockSpec((1, d), lambda i, j: (0, 0)),
                  pl.BlockSpec((d, LANES), lambda i, j: (0, 0)),
                  pl.BlockSpec((1, LANES), lambda i, j: (0, 0)),
                  pl.BlockSpec((d, tn), lambda i, j: (0, j))],
        out_specs=[pl.BlockSpec((tm, tn), lambda i, j: (i, j)),
                   pl.BlockSpec((tm, LANES), lambda i, j: (i, 0))],
        out_shape=[jax.ShapeDtypeStruct((r, n_cols), BF16),
                   jax.ShapeDtypeStruct((r, LANES), F32)],
        scratch_shapes=[pltpu.VMEM((tm, d), BF16)],
        compiler_params=_cparams(("arbitrary", "arbitrary")),
        name="inproj",
    )(x2d, mod3, mod3, norm_g, w_gates, gate_b, w_main)


def _conv_kernel(x_ref, w_ref, b_ref, o_ref, *, q_tiles):
    c = pl.program_id(1)
    x = x_ref[...].astype(F32)
    t = x.shape[0]
    rows = lax.broadcasted_iota(I32, x.shape, 0)
    w = w_ref[...]
    acc = x * w[CONV_W // 2:CONV_W // 2 + 1, :] + b_ref[...]
    for dlt in range(-(CONV_W // 2), CONV_W // 2 + 1):
        if dlt == 0:
            continue
        xs = pltpu.roll(x, shift=(-dlt) % t, axis=0)
        valid = (rows + dlt >= 0) & (rows + dlt < t)
        acc = acc + jnp.where(valid, xs, 0.0) * w[dlt + CONV_W // 2:dlt + CONV_W // 2 + 1, :]
    y = acc * _sigmoid(acc)
    scale = jnp.where(c < q_tiles, QK_DIM ** -0.5, 1.0).astype(F32)
    o_ref[...] = (y * scale).astype(BF16)


def _conv(z3, conv_w, conv_b):
    b, t, _ = z3.shape
    width = conv_w.shape[1]
    tc = 256
    kern = functools.partial(_conv_kernel, q_tiles=(width // 2) // tc)
    return pl.pallas_call(
        kern,
        grid=(b, width // tc),
        in_specs=[pl.BlockSpec((None, t, tc), lambda i, c: (i, 0, c)),
                  pl.BlockSpec((CONV_W, tc), lambda i, c: (0, c)),
                  pl.BlockSpec((1, tc), lambda i, c: (0, c))],
        out_specs=pl.BlockSpec((None, t, tc), lambda i, c: (i, 0, c)),
        out_shape=jax.ShapeDtypeStruct((b, t, width), BF16),
        compiler_params=_cparams(("arbitrary", "arbitrary")),
        name="conv",
    )(z3, conv_w, conv_b.reshape(1, width))


def _mlstm_kernel(qkc_f, qkl_f, vc_f, vl_f, gc_f, gl_f, qkc_b, qkl_b, vc_b, vl_b, gc_b, gl_b,
                  hf_ref, hb_ref, c_scr, m_scr, *, n_ctx_chunks):
    s = pl.program_id(1)

    @pl.when(s == 0)
    def _():
        c_scr[...] = jnp.zeros_like(c_scr)
        m_scr[...] = jnp.zeros_like(m_scr)

    is_ctx = s < n_ctx_chunks
    rows = lax.broadcasted_iota(I32, (CHUNK, CHUNK), 0)
    lanes = lax.broadcasted_iota(I32, (CHUNK, CHUNK), 1)
    ones_col = jnp.where(lanes == 0, 1.0, 0.0).astype(BF16)
    dn_nt = (((1,), (1,)), ((), ()))
    dn_tn = (((0,), (0,)), ((), ()))
    qk_w = M_HEADS * QK_DIM

    dirs = ((qkc_f, qkl_f, vc_f, vl_f, gc_f, gl_f, hf_ref), (qkc_b, qkl_b, vc_b, vl_b, gc_b, gl_b, hb_ref))
    for d, (qkc, qkl, vc, vl, gc, gl, out_ref) in enumerate(dirs):
        qk = jnp.where(is_ctx, qkc[...], qkl[...])
        v = jnp.where(is_ctx, vc[...], vl[...])
        g = jnp.where(is_ctx, gc[...], gl[...])
        mask = (lanes <= rows) if d == 0 else (lanes >= rows)
        tri = jnp.where(mask, 1.0, 0.0).astype(BF16)
        ls = jnp.minimum(g, 0.0) - jnp.log(1.0 + jnp.exp(-jnp.abs(g)))
        l1, l2, l3 = _split3(ls)
        bcol = (jnp.dot(tri, l1, preferred_element_type=F32) + jnp.dot(tri, l2, preferred_element_type=F32)
                + jnp.dot(tri, l3, preferred_element_type=F32))
        brow = bcol.T
        grow = g.T
        for h in range(M_HEADS):
            col_i = 2 * M_HEADS * d + h
            col_f = col_i + M_HEADS
            idx = d * M_HEADS + h
            bc = bcol[:, col_f:col_f + 1]
            br = brow[col_f:col_f + 1, :]
            ir = grow[col_i:col_i + 1, :]
            ic = g[:, col_i:col_i + 1]
            m = m_scr[idx][0:1, 0:1]
            dlog = jnp.where(mask, bc - br + ir, -jnp.inf)
            inter = bc + m
            m_t = jnp.maximum(inter, jnp.max(dlog, axis=1, keepdims=True))
            w_inter = jnp.exp(inter - m_t)
            q = qk[:, h * QK_DIM:(h + 1) * QK_DIM]
            k = qk[:, qk_w + h * QK_DIM:qk_w + (h + 1) * QK_DIM]
            vaug = jnp.concatenate([v[:, h * V_DIM:(h + 1) * V_DIM], ones_col], axis=1)
            sc = lax.dot_general(q, k, dn_nt, preferred_element_type=F32) * jnp.exp(dlog - m_t)
            caug = c_scr[idx]
            num = (w_inter * jnp.dot(q, caug.astype(BF16), preferred_element_type=F32)
                   + jnp.dot(sc.astype(BF16), vaug, preferred_element_type=F32))
            den = num[:, V_DIM:V_DIM + 1]
            hout = num[:, :V_DIM] / jnp.maximum(jnp.abs(den), jnp.exp(-m_t))
            b_end = br[:, CHUNK - 1:CHUNK] if d == 0 else br[:, 0:1]
            m_new = jnp.maximum(b_end + m, jnp.max(b_end - br + ir, axis=1, keepdims=True))
            decay = jnp.exp(b_end + m - m_new)
            w_col = jnp.exp(b_end - bc + ic - m_new)
            kw = (k.astype(F32) * w_col).astype(BF16)
            c_scr[idx] = decay * caug + lax.dot_general(kw, vaug, dn_tn, preferred_element_type=F32)
            m_scr[idx] = jnp.broadcast_to(m_new, m_scr.shape[1:])
            out_ref[:, h * V_DIM:(h + 1) * V_DIM] = hout


def _mlstm(qk_ctx, qk_lat, z_ctx3, z_lat3, g_ctx3, g_lat3):
    b, tc, _ = qk_ctx.shape
    t = qk_lat.shape[1]
    nc, nl = tc // CHUNK, t // CHUNK
    vw = M_HEADS * V_DIM
    qkw = 2 * M_HEADS * QK_DIM

    f_ctx = lambda i, s: (i, jnp.minimum(s, nc - 1), 0)
    f_lat = lambda i, s: (i, jnp.clip(s - nc, 0, nl - 1), 0)
    b_ctx = lambda i, s: (i, jnp.maximum(nc - 1 - s, 0), 0)
    b_lat = lambda i, s: (i, jnp.clip(nc + nl - 1 - s, 0, nl - 1), 0)
    v_of = lambda f: (lambda i, s: (f(i, s)[0], f(i, s)[1], 1))

    def specs(fc, fl):
        return [pl.BlockSpec((None, CHUNK, qkw), fc), pl.BlockSpec((None, CHUNK, qkw), fl),
                pl.BlockSpec((None, CHUNK, vw), v_of(fc)), pl.BlockSpec((None, CHUNK, vw), v_of(fl)),
                pl.BlockSpec((None, CHUNK, LANES), fc), pl.BlockSpec((None, CHUNK, LANES), fl)]

    kern = functools.partial(_mlstm_kernel, n_ctx_chunks=nc)
    args = (qk_ctx, qk_lat, z_ctx3, z_lat3, g_ctx3, g_lat3)
    return pl.pallas_call(
        kern,
        grid=(b, nc + nl),
        in_specs=specs(f_ctx, f_lat) + specs(b_ctx, b_lat),
        out_specs=[pl.BlockSpec((None, CHUNK, vw), f_lat), pl.BlockSpec((None, CHUNK, vw), b_lat)],
        out_shape=[jax.ShapeDtypeStruct((b, t, vw), F32), jax.ShapeDtypeStruct((b, t, vw), F32)],
        scratch_shapes=[pltpu.VMEM((2 * M_HEADS, QK_DIM, V_DIM + LANES), F32),
                        pltpu.VMEM((2 * M_HEADS, 8, LANES), F32)],
        compiler_params=_cparams(("arbitrary", "arbitrary")),
        name="mlstm",
    )(*args, *args)


def _merge_kernel(hf_ref, hb_ref, o_ref, u_ref, sv_ref, ga_ref, gb_ref, gm_ref, gs_ref, sw_ref, sb_ref,
                  pa_ref, pb_ref, out_ref, a_scr, b_scr):
    j = pl.program_id(1)

    @pl.when(j == 0)
    def _():
        tm = hf_ref.shape[0]
        for h in range(M_HEADS):
            sl = slice(h * V_DIM, (h + 1) * V_DIM)
            hs = hf_ref[:, sl] + hb_ref[:, sl]
            ms = jnp.mean(hs * hs, axis=-1, keepdims=True)
            y = hs * lax.rsqrt(ms + EPS) * gm_ref[:, sl]
            a_scr[:, sl] = (y * o_ref[:, sl].astype(F32)).astype(BF16)
        sv = sv_ref[...].astype(F32)
        ms = jnp.mean(sv * sv, axis=-1, keepdims=True)
        svn = (sv * lax.rsqrt(ms + EPS) * gs_ref[...]).astype(BF16)
        gw = sv.shape[1] // SGU_GROUPS
        for c in range(tm // CHUNK):
            rs = slice(c * CHUNK, (c + 1) * CHUNK)
            for gi in range(SGU_GROUPS):
                cs = slice(gi * gw, (gi + 1) * gw)
                zz = jnp.dot(sw_ref[gi], svn[rs, cs], preferred_element_type=F32) + sb_ref[:, gi:gi + 1]
                b_scr[rs, cs] = (u_ref[rs, cs].astype(F32) * zz).astype(BF16)

    ya = jnp.dot(a_scr[...], pa_ref[...], preferred_element_type=F32)
    yb = jnp.dot(b_scr[...], pb_ref[...], preferred_element_type=F32)
    out_ref[...] = (ga_ref[...].astype(F32) * ya + gb_ref[...].astype(F32) * yb).astype(BF16)


def _merge(hf, hb, z, gm, gs, sgu_w, sgu_bt, pa, pb):
    r, d = hf.shape
    tm, tn = 256, 512
    seg = d // tn
    row = lambda k: (lambda i, j: (i, k))
    return pl.pallas_call(
        _merge_kernel,
        grid=(r // tm, d // tn),
        in_specs=[pl.BlockSpec((tm, d), row(0)), pl.BlockSpec((tm, d), row(0)),
                  pl.BlockSpec((tm, d), row(2)), pl.BlockSpec((tm, d), row(3)), pl.BlockSpec((tm, d), row(4)),
                  pl.BlockSpec((tm, tn), lambda i, j: (i, 5 * seg + j)),
                  pl.BlockSpec((tm, tn), lambda i, j: (i, 6 * seg + j)),
                  pl.BlockSpec((1, d), lambda i, j: (0, 0)), pl.BlockSpec((1, d), lambda i, j: (0, 0)),
                  pl.BlockSpec(sgu_w.shape, lambda i, j: (0, 0, 0)),
                  pl.BlockSpec(sgu_bt.shape, lambda i, j: (0, 0)),
                  pl.BlockSpec((d, tn), lambda i, j: (0, j)), pl.BlockSpec((d, tn), lambda i, j: (0, j))],
        out_specs=pl.BlockSpec((tm, tn), lambda i, j: (i, j)),
        out_shape=jax.ShapeDtypeStruct((r, d), BF16),
        scratch_shapes=[pltpu.VMEM((tm, d), BF16), pltpu.VMEM((tm, d), BF16)],
        compiler_params=_cparams(("arbitrary", "arbitrary")),
        name="merge",
    )(hf, hb, z, z, z, z, z, gm, gs, sgu_w, sgu_bt, pa, pb)


def _outproj_kernel(m_ref, w_ref, x_ref, gate_ref, shift_ref, scale_ref, g_ref, rw_ref, rb_ref,
                    x1_ref, h2_ref, lg_ref):
    out = jnp.dot(m_ref[...], w_ref[...], preferred_element_type=F32)
    x1 = x_ref[...] + gate_ref[...] * out
    x1_ref[...] = x1
    ms = jnp.mean(x1 * x1, axis=-1, keepdims=True)
    h2 = (x1 * lax.rsqrt(ms + EPS) * g_ref[...]) * (1.0 + scale_ref[...]) + shift_ref[...]
    hi, lo, _ = _split3(h2)
    r1, r2, _ = _split3(rw_ref[...])
    dn_nt = (((1,), (1,)), ((), ()))
    lg_ref[...] = (lax.dot_general(r1, hi, dn_nt, preferred_element_type=F32)
                   + lax.dot_general(r1, lo, dn_nt, preferred_element_type=F32)
                   + lax.dot_general(r2, hi, dn_nt, preferred_element_type=F32)) + rb_ref[...]
    half = h2.shape[1] // 2
    hf = hi.astype(F32)
    _store_token_major(h2_ref, 0, _pack_pair(hf[:, :half], hf[:, half:]))


def _outproj(merged, w_out, x2d, mod3, mod_row_of_tile, norm_g, rw_t, rb_col):
    r, d = x2d.shape
    tm = 256
    ne = rw_t.shape[0]
    modspec = lambda k: pl.BlockSpec((None, 1, d), lambda i: (mod_row_of_tile(i), 0, k))
    return pl.pallas_call(
        _outproj_kernel,
        grid=(r // tm,),
        in_specs=[pl.BlockSpec((tm, d), lambda i: (i, 0)),
                  pl.BlockSpec((d, d), lambda i: (0, 0)),
                  pl.BlockSpec((tm, d), lambda i: (i, 0)),
                  modspec(2), modspec(3), modspec(4),
                  pl.BlockSpec((1, d), lambda i: (0, 0)),
                  pl.BlockSpec((ne, d), lambda i: (0, 0)),
                  pl.BlockSpec((ne, 1), lambda i: (0, 0))],
        out_specs=[pl.BlockSpec((tm, d), lambda i: (i, 0)),
                   pl.BlockSpec((tm * TOKEN_ROWS, LANES), lambda i: (i, 0)),
                   pl.BlockSpec((ne, tm), lambda i: (0, i))],
        out_shape=[jax.ShapeDtypeStruct((r, d), F32),
                   jax.ShapeDtypeStruct((r * TOKEN_ROWS, LANES), U32),
                   jax.ShapeDtypeStruct((ne, r), F32)],
        compiler_params=_cparams(("arbitrary",)),
        name="outproj",
    )(merged, w_out, x2d, mod3, mod3, mod3, norm_g, rw_t, rb_col)


def _route_kernel(lg_ref, eid_ref, gate_ref, rank_ref, cnt_ref, carry_scr):
    i = pl.program_id(0)

    @pl.when(i == 0)
    def _():
        carry_scr[...] = jnp.zeros_like(carry_scr)

    l = lg_ref[...]
    ne, tm = l.shape
    e_iota = lax.broadcasted_iota(I32, (ne, tm), 0)
    vals, onehots = [], []
    for k in range(TOP_K):
        mx = jnp.max(l, axis=0, keepdims=True)
        idx = jnp.min(jnp.where(l == mx, e_iota, ne), axis=0, keepdims=True)
        oh = e_iota == idx
        l = jnp.where(oh, -jnp.inf, l)
        vals.append(mx)
        onehots.append(oh)
        eid_ref[k:k + 1, :] = idx
    ex = [jnp.exp(vk - vals[0]) for vk in vals]
    tot = ex[0] + ex[1] + ex[2] + ex[3]
    for k in range(TOP_K):
        gate_ref[k:k + 1, :] = ex[k] / tot

    oh_all = jnp.zeros((ne, tm), F32)
    for oh in onehots:
        oh_all = oh_all + jnp.where(oh, 1.0, 0.0)
    oh_all = oh_all.astype(BF16)
    r_i = lax.broadcasted_iota(I32, (LANES, LANES), 0)
    c_i = lax.broadcasted_iota(I32, (LANES, LANES), 1)
    tri_excl = jnp.where(r_i < c_i, 1.0, 0.0).astype(BF16)
    ones_m = jnp.ones((LANES, LANES), BF16)
    carry = carry_scr[...]
    for blk in range(tm // LANES):
        sl = slice(blk * LANES, (blk + 1) * LANES)
        ohb = oh_all[:, sl]
        cum = jnp.dot(ohb, tri_excl, preferred_element_type=F32) + carry
        for k in range(TOP_K):
            rk = jnp.sum(jnp.where(onehots[k][:, sl], cum, 0.0), axis=0, keepdims=True)
            rank_ref[k:k + 1, sl] = rk.astype(I32)
        carry = carry + jnp.dot(ohb, ones_m, preferred_element_type=F32)
    carry_scr[...] = carry
    cnt_ref[...] = carry


def _route(logits_t):
    ne, r = logits_t.shape
    tm = 1024
    return pl.pallas_call(
        _route_kernel,
        grid=(r // tm,),
        in_specs=[pl.BlockSpec((ne, tm), lambda i: (0, i))],
        out_specs=[pl.BlockSpec((TOP_K, tm), lambda i: (0, i)),
                   pl.BlockSpec((TOP_K, tm), lambda i: (0, i)),
                   pl.BlockSpec((TOP_K, tm), lambda i: (0, i)),
                   pl.BlockSpec((ne, LANES), lambda i: (0, 0))],
        out_shape=[jax.ShapeDtypeStruct((TOP_K, r), I32),
                   jax.ShapeDtypeStruct((TOP_K, r), F32),
                   jax.ShapeDtypeStruct((TOP_K, r), I32),
                   jax.ShapeDtypeStruct((ne, LANES), F32)],
        scratch_shapes=[pltpu.VMEM((ne, LANES), F32)],
        compiler_params=_cparams(("arbitrary",)),
        name="route",
    )(logits_t)


def _slot_kernel(eid_ref, rank_ref, cnt_ref, slot_ref):
    ne = cnt_ref.shape[0]
    nblk = jnp.floor((cnt_ref[...] + (MOE_TILE - 1.0)) * (1.0 / MOE_TILE))
    r_i = lax.broadcasted_iota(I32, (ne, ne), 0)
    c_i = lax.broadcasted_iota(I32, (ne, ne), 1)
    tri = jnp.where(c_i < r_i, 1.0, 0.0).astype(BF16)
    start = jnp.dot(tri, nblk.astype(BF16), preferred_element_type=F32) * MOE_TILE
    eid = eid_ref[...]
    acc = rank_ref[...]
    for e in range(ne):
        acc = acc + jnp.where(eid == e, start[e:e + 1, 0:1].astype(I32), 0)
    slot_ref[...] = acc


def _slots(eid, rank, cnt):
    return pl.pallas_call(
        _slot_kernel,
        out_shape=jax.ShapeDtypeStruct(eid.shape, I32),
        compiler_params=pltpu.CompilerParams(vmem_limit_bytes=V7X_VMEM_LIMIT_BYTES),
        name="slots",
    )(eid, rank, cnt)


def _row_copy(src_ref, dst_ref, sidx_ref, didx_ref, sem, j):
    s0 = pl.multiple_of(sidx_ref[j] * TOKEN_ROWS, TOKEN_ROWS)
    d0 = pl.multiple_of(didx_ref[j] * TOKEN_ROWS, TOKEN_ROWS)
    return pltpu.make_async_copy(src_ref.at[pl.ds(s0, TOKEN_ROWS), :], dst_ref.at[pl.ds(d0, TOKEN_ROWS), :], sem)


def _permute_kernel(sidx_ref, didx_ref, src_ref, *rest, chunk):
    dst_ref, sem = rest[-2], rest[-1]
    base = pl.program_id(0) * chunk

    def start(i, carry):
        _row_copy(src_ref, dst_ref, sidx_ref, didx_ref, sem, base + i).start()
        return carry

    def wait(i, carry):
        _row_copy(src_ref, dst_ref, sidx_ref, didx_ref, sem, base + i).wait()
        return carry

    lax.fori_loop(0, chunk, start, 0)
    lax.fori_loop(0, chunk, wait, 0)


def _permute_rows(src, src_idx, dst_idx, n_dst_rows, init=None):
    n = src_idx.shape[0]
    chunk = 2048
    width = src.shape[1]
    n_dst_rows = n_dst_rows * TOKEN_ROWS
    any_spec = pl.BlockSpec(memory_space=pl.ANY)
    operands = [src_idx, dst_idx, src] + ([init] if init is not None else [])
    grid_spec = pltpu.PrefetchScalarGridSpec(
        num_scalar_prefetch=2,
        grid=(n // chunk,),
        in_specs=[any_spec] * (len(operands) - 2),
        out_specs=any_spec,
        scratch_shapes=[pltpu.SemaphoreType.DMA(())],
    )
    return pl.pallas_call(
        functools.partial(_permute_kernel, chunk=chunk),
        grid_spec=grid_spec,
        out_shape=jax.ShapeDtypeStruct((n_dst_rows, width), src.dtype),
        input_output_aliases={3: 0} if init is not None else {},
        compiler_params=_cparams(("arbitrary",)),
        name="permute_rows",
    )(*operands)


def _moe_up_kernel(blk_ref, oblk_ref, e_ref, nv_ref, new_ref, xs_ref, wg_ref, wl_ref, bg_ref, bl_ref, act_ref,
                   wg_bf, wl_bf):
    it = pl.program_id(1)

    @pl.when(new_ref[it] == 1)
    def _():
        wg_bf[...] = wg_ref[...].astype(BF16)
        wl_bf[...] = wl_ref[...].astype(BF16)

    nv = nv_ref[it]
    for half in range(MOE_TILE // MOE_HALF):
        rs = slice(half * MOE_HALF, (half + 1) * MOE_HALF)

        @pl.when(nv > half * MOE_HALF)
        def _():
            pairs = [_unpack_pair(w) for w in _load_token_major(xs_ref, half * MOE_HALF, MOE_HALF)]
            x = jnp.concatenate([p[0].astype(BF16) for p in pairs] + [p[1].astype(BF16) for p in pairs], axis=1)
            hg = jnp.dot(x, wg_bf[...], preferred_element_type=F32) + bg_ref[...]
            hl = jnp.dot(x, wl_bf[...], preferred_element_type=F32) + bl_ref[...]
            glu = jnp.minimum(hg, SWIGLU_LIMIT)
            lin = jnp.clip(hl, -SWIGLU_LIMIT, SWIGLU_LIMIT)
            act_ref[rs, :] = (glu * _sigmoid(SWIGLU_ALPHA * glu) * (lin + 1.0)).astype(BF16)

        @pl.when(nv <= half * MOE_HALF)
        def _():
            act_ref[rs, :] = jnp.zeros((MOE_HALF, act_ref.shape[1]), BF16)


def _moe_down_kernel(blk_ref, oblk_ref, e_ref, nv_ref, new_ref, a_ref, w_ref, b_ref, y_ref, w_bf):
    it = pl.program_id(0)

    @pl.when(new_ref[it] == 1)
    def _():
        w_bf[...] = w_ref[...].astype(BF16)

    nv = nv_ref[it]
    hw = w_ref.shape[1] // 2
    for half in range(MOE_TILE // MOE_HALF):
        rs = slice(half * MOE_HALF, (half + 1) * MOE_HALF)

        @pl.when(nv > half * MOE_HALF)
        def _():
            y = jnp.dot(a_ref[rs, :], w_bf[...], preferred_element_type=F32) + b_ref[...]
            _store_token_major(y_ref, half * MOE_HALF, _pack_pair(y[:, :hw], y[:, hw:]))

        @pl.when(nv <= half * MOE_HALF)
        def _():
            y_ref[half * MOE_HALF * TOKEN_ROWS:(half + 1) * MOE_HALF * TOKEN_ROWS, :] = jnp.zeros(
                (MOE_HALF * TOKEN_ROWS, LANES), U32)


def _moe_up(items, xs, w1, b1):
    n_items = items[0].shape[0]
    ne, d, two_f = w1.shape
    f = two_f // 2
    nj = f // MOE_TN
    p = xs.shape[0] // TOKEN_ROWS
    b1r = b1.reshape(ne, 1, two_f)
    grid_spec = pltpu.PrefetchScalarGridSpec(
        num_scalar_prefetch=len(items),
        grid=(nj, n_items),
        in_specs=[pl.BlockSpec((MOE_TILE * TOKEN_ROWS, LANES), lambda j, it, blk, oblk, e, nv, new: (blk[it], 0)),
                  pl.BlockSpec((None, d, MOE_TN), lambda j, it, blk, oblk, e, nv, new: (e[it], 0, j)),
                  pl.BlockSpec((None, d, MOE_TN), lambda j, it, blk, oblk, e, nv, new: (e[it], 0, nj + j)),
                  pl.BlockSpec((None, 1, MOE_TN), lambda j, it, blk, oblk, e, nv, new: (e[it], 0, j)),
                  pl.BlockSpec((None, 1, MOE_TN), lambda j, it, blk, oblk, e, nv, new: (e[it], 0, nj + j))],
        out_specs=pl.BlockSpec((MOE_TILE, MOE_TN), lambda j, it, blk, oblk, e, nv, new: (oblk[it], j)),
        scratch_shapes=[pltpu.VMEM((d, MOE_TN), BF16), pltpu.VMEM((d, MOE_TN), BF16)],
    )
    return pl.pallas_call(
        _moe_up_kernel,
        grid_spec=grid_spec,
        out_shape=jax.ShapeDtypeStruct((p, f), BF16),
        compiler_params=_cparams(("arbitrary", "arbitrary")),
        name="moe_up",
    )(*items, xs, w1, w1, b1r, b1r)


def _moe_down(items, act, w2, b2):
    n_items = items[0].shape[0]
    ne, f, d = w2.shape
    p = act.shape[0]
    b2r = b2.reshape(ne, 1, d)
    grid_spec = pltpu.PrefetchScalarGridSpec(
        num_scalar_prefetch=len(items),
        grid=(n_items,),
        in_specs=[pl.BlockSpec((MOE_TILE, f), lambda it, blk, oblk, e, nv, new: (blk[it], 0)),
                  pl.BlockSpec((None, f, d), lambda it, blk, oblk, e, nv, new: (e[it], 0, 0)),
                  pl.BlockSpec((None, 1, d), lambda it, blk, oblk, e, nv, new: (e[it], 0, 0))],
        out_specs=pl.BlockSpec((MOE_TILE * TOKEN_ROWS, LANES), lambda it, blk, oblk, e, nv, new: (oblk[it], 0)),
        scratch_shapes=[pltpu.VMEM((f, d), BF16)],
    )
    return pl.pallas_call(
        _moe_down_kernel,
        grid_spec=grid_spec,
        out_shape=jax.ShapeDtypeStruct((p * TOKEN_ROWS, LANES), U32),
        compiler_params=_cparams(("arbitrary",)),
        name="moe_down",
    )(*items, act, w2, b2r)


def _final_kernel(yk_ref, gates_ref, x1_ref, gate_ref, g_ref, o_ref):
    tm, d = x1_ref.shape
    acc = jnp.zeros((tm, d), F32)
    for k in range(TOP_K):
        pairs = [_unpack_pair(w) for w in _load_token_major(yk_ref, 0, tm, lead=k)]
        y = jnp.concatenate([p[0] for p in pairs] + [p[1] for p in pairs], axis=1)
        acc = acc + gates_ref[:, k:k + 1] * y
    xf = x1_ref[...] + gate_ref[...] * acc
    ms = jnp.mean(xf * xf, axis=-1, keepdims=True)
    o_ref[...] = xf * lax.rsqrt(ms + EPS) * g_ref[...]


def _final(yk, gates_tk, x1, mod3, mod_row_of_tile, final_g):
    r, d = x1.shape
    tm = 256
    return pl.pallas_call(
        _final_kernel,
        grid=(r // tm,),
        in_specs=[pl.BlockSpec((TOP_K, tm * TOKEN_ROWS, LANES), lambda i: (0, i, 0)),
                  pl.BlockSpec((tm, TOP_K), lambda i: (i, 0)),
                  pl.BlockSpec((tm, d), lambda i: (i, 0)),
                  pl.BlockSpec((None, 1, d), lambda i: (mod_row_of_tile(i), 0, 5)),
                  pl.BlockSpec((1, d), lambda i: (0, 0))],
        out_specs=pl.BlockSpec((tm, d), lambda i: (i, 0)),
        out_shape=jax.ShapeDtypeStruct((r, d), F32),
        compiler_params=_cparams(("arbitrary",)),
        name="final",
    )(yk, gates_tk, x1, mod3, final_g)


def _moe_schedule(counts, n_items):
    ne = counts.shape[0]
    padded = (counts + MOE_TILE - 1) // MOE_TILE * MOE_TILE
    pad_end = jnp.cumsum(padded)
    start_pad = pad_end - padded
    n_real = pad_end[-1] // MOE_TILE
    b = jnp.arange(n_items, dtype=I32)
    blk = jnp.minimum(b, jnp.maximum(n_real - 1, 0))
    blk_start = blk * MOE_TILE
    e_of = jnp.minimum(jnp.sum((pad_end[None, :] <= blk_start[:, None]).astype(I32), axis=1), ne - 1)
    nv = jnp.clip(counts[e_of] - (blk_start - start_pad[e_of]), 0, MOE_TILE)
    nv = jnp.where(b < n_real, nv, 0)
    prev_e = jnp.concatenate([jnp.full((1,), -1, I32), e_of[:-1]])
    new = ((e_of != prev_e) | (b == 0)).astype(I32)
    return start_pad, (blk.astype(I32), b, e_of.astype(I32), nv.astype(I32), new)


def kernel(x, c, ctx, c_ctx, mod_w, mod_b, norm1_g, norm2_g, w_in, gate_b, conv_w, conv_b, mlstm_norm_g,
           sgu_norm_g, sgu_w, sgu_b, proj_a, proj_b, w_out, router_w, router_b, exp_w1, exp_b1, exp_w2,
           exp_b2, final_g):
    bsz, t, d = x.shape
    tc = ctx.shape[1]
    depth = mod_w.shape[0]
    assert depth == 1, "single-layer block"
    assert d // 2 == TOKEN_ROWS * LANES, "token-major packing holds one (8,128) word tile per row"
    l = 0
    r = bsz * t
    qkw = 2 * M_HEADS * QK_DIM
    vw = M_HEADS * V_DIM
    off_gates = qkw + vw
    n_gates = 4 * M_HEADS
    off_o = off_gates + n_gates

    cc = jnp.zeros((8, d), F32).at[:bsz].set(c).at[bsz].set(c_ctx)
    mod3 = _mod(cc, mod_w[l], mod_b[l]).reshape(8, 1, N_MOD * d)

    w_l = w_in[l]
    w_main = jnp.concatenate([w_l[:, :off_gates], w_l[:, off_o:]], axis=1).astype(BF16)
    w_gates = jnp.pad(w_l[:, off_gates:off_o], ((0, 0), (0, LANES - n_gates)))
    gb = jnp.pad(gate_b[l].reshape(1, n_gates), ((0, 0), (0, LANES - n_gates)))
    g1 = norm1_g[l].reshape(1, d)

    tm_in = 1024
    lat_row = lambda i: i // (t // tm_in)
    z_lat, g_lat = _inproj(x.reshape(r, d), mod3, lat_row, g1, w_gates, gb, w_main, w_main.shape[1], tm_in)
    z_ctx, g_ctx = _inproj(ctx.reshape(bsz * tc, d), mod3, lambda i: bsz, g1, w_gates, gb, w_main, off_gates, tc)

    z_lat3 = z_lat.reshape(bsz, t, z_lat.shape[1])
    z_ctx3 = z_ctx.reshape(bsz, tc, z_ctx.shape[1])
    qk_lat = _conv(z_lat3, conv_w[l], conv_b[l])
    qk_ctx = _conv(z_ctx3, conv_w[l], conv_b[l])

    hf, hb = _mlstm(qk_ctx, qk_lat, z_ctx3, z_lat3, g_ctx.reshape(bsz, tc, LANES), g_lat.reshape(bsz, t, LANES))

    merged = _merge(hf.reshape(r, vw), hb.reshape(r, vw), z_lat,
                    mlstm_norm_g[l].reshape(1, vw), sgu_norm_g[l].reshape(1, -1),
                    sgu_w[l].astype(BF16), sgu_b[l].T, proj_a[l].astype(BF16), proj_b[l].astype(BF16))

    tm_out = 256
    x1, h2p, logits_t = _outproj(merged, w_out[l].astype(BF16), x.reshape(r, d), mod3,
                                 lambda i: i // (t // tm_out), norm2_g[l].reshape(1, d),
                                 router_w[l].T, router_b[l].reshape(-1, 1))

    eid, gates, rank, cnt = _route(logits_t)
    counts = cnt[:, 0].astype(I32)
    n_items = (r * TOP_K) // MOE_TILE + N_EXPERTS
    _, items = _moe_schedule(counts, n_items)
    slot = _slots(eid, rank, cnt).reshape(-1)
    tok = jnp.tile(jnp.arange(r, dtype=I32), TOP_K)
    p_rows = n_items * MOE_TILE

    xs = _permute_rows(h2p, tok, slot, p_rows, init=jnp.zeros((p_rows * TOKEN_ROWS, LANES), U32))
    act = _moe_up(items, xs, exp_w1[l], exp_b1[l])
    y = _moe_down(items, act, exp_w2[l], exp_b2[l])
    yk = _permute_rows(y, slot, jnp.arange(r * TOP_K, dtype=I32), r * TOP_K)

    out = _final(yk.reshape(TOP_K, r * TOKEN_ROWS, LANES), gates.T, x1, mod3, lambda i: i // (t // tm_out),
                 final_g.reshape(1, d))
    return out.reshape(bsz, t, d)
```

```python
import functools

import jax
import jax.numpy as jnp
from jax import lax
from jax.experimental import pallas as pl
from jax.experimental.pallas import tpu as pltpu

F32 = jnp.float32
BF16 = jnp.bfloat16
I32 = jnp.int32
U32 = jnp.uint32

EPS = 1e-6
M_HEADS = 8
QK_DIM = 128
V_DIM = 256
CONV_W = 5
CHUNK = 128
SGU_GROUPS = 8
N_EXPERTS = 32
TOP_K = 4
SWIGLU_ALPHA = 1.702
SWIGLU_LIMIT = 7.0
N_MOD = 6

V7X_VMEM_LIMIT_BYTES = 56 * 1024 * 1024
LANES = 128
MOE_TILE = 512
MOE_HALF = 256
MOE_STEP = 128
MOE_TN = 1024


def _cparams(sem):
    return pltpu.CompilerParams(dimension_semantics=sem, vmem_limit_bytes=V7X_VMEM_LIMIT_BYTES)


def _sigmoid(x):
    return 1.0 / (1.0 + jnp.exp(-x))


def _gelu_tanh(x):
    return 0.5 * x * (1.0 + jnp.tanh(0.7978845608028654 * (x + 0.044715 * (x * x * x))))


def _split3(a):
    a1 = a.astype(BF16)
    r1 = a - a1.astype(F32)
    a2 = r1.astype(BF16)
    a3 = (r1 - a2.astype(F32)).astype(BF16)
    return a1, a2, a3


TOKEN_ROWS = 16


def _store_token_major(ref, row0, vals):
    n = vals.shape[0]
    for s in range(TOKEN_ROWS):
        ref[pl.ds(row0 * TOKEN_ROWS + s, n, stride=TOKEN_ROWS), :] = vals[:, s * LANES:(s + 1) * LANES]


def _load_token_major(ref, row0, n, lead=()):
    cols = [ref[(*lead, pl.ds(row0 * TOKEN_ROWS + s, n, stride=TOKEN_ROWS), slice(None))] for s in range(TOKEN_ROWS)]
    return jnp.concatenate(cols, axis=1)


def _mod_kernel(s_ref, w_ref, b_ref, o_ref):
    s = s_ref[...]
    s = s * _sigmoid(s)
    o_ref[...] = jnp.dot(s.astype(BF16), w_ref[...].astype(BF16), preferred_element_type=F32) + b_ref[...]


def _mod(cc, mod_w, mod_b):
    d, n = mod_w.shape
    tn = 1024
    return pl.pallas_call(
        _mod_kernel,
        grid=(n // tn,),
        in_specs=[pl.BlockSpec((8, d), lambda j: (0, 0)),
                  pl.BlockSpec((d, tn), lambda j: (0, j)),
                  pl.BlockSpec((1, tn), lambda j: (0, j))],
        out_specs=pl.BlockSpec((8, tn), lambda j: (0, j)),
        out_shape=jax.ShapeDtypeStruct((8, n), F32),
        compiler_params=_cparams(("arbitrary",)),
        name="mod",
    )(cc, mod_w, mod_b.reshape(1, n))


def _norm_kernel(x_ref, shift_ref, scale_ref, g_ref, wg_ref, gb_ref, h_ref, gates_ref):
    x = x_ref[...]
    ms = jnp.mean(x * x, axis=-1, keepdims=True)
    h = (x * lax.rsqrt(ms + EPS) * g_ref[...]) * (1.0 + scale_ref[...]) + shift_ref[...]
    h1, h2, _ = _split3(h)
    h_ref[...] = h1
    w1, w2, _ = _split3(wg_ref[...])
    gates_ref[...] = (jnp.dot(h1, w1, preferred_element_type=F32)
                      + jnp.dot(h1, w2, preferred_element_type=F32)
                      + jnp.dot(h2, w1, preferred_element_type=F32)) + gb_ref[...]


def _norm(x2d, mod3, mod_row_of_tile, norm_g, w_gates, gate_b, tm):
    r, d = x2d.shape
    return pl.pallas_call(
        _norm_kernel,
        grid=(r // tm,),
        in_specs=[pl.BlockSpec((tm, d), lambda i: (i, 0)),
                  pl.BlockSpec((None, 1, d), lambda i: (mod_row_of_tile(i), 0, 0)),
                  pl.BlockSpec((None, 1, d), lambda i: (mod_row_of_tile(i), 0, 1)),
                  pl.BlockSpec((1, d), lambda i: (0, 0)),
                  pl.BlockSpec((d, LANES), lambda i: (0, 0)),
                  pl.BlockSpec((1, LANES), lambda i: (0, 0))],
        out_specs=[pl.BlockSpec((tm, d), lambda i: (i, 0)),
                   pl.BlockSpec((tm, LANES), lambda i: (i, 0))],
        out_shape=[jax.ShapeDtypeStruct((r, d), BF16),
                   jax.ShapeDtypeStruct((r, LANES), F32)],
        compiler_params=_cparams(("arbitrary",)),
        name="norm",
    )(x2d, mod3, mod3, norm_g, w_gates, gate_b)


INPROJ_TN = 1024
INPROJ_ROWS = 256


def _inproj_kernel(h_ref, w_ref, wx_ref, z_ref, w_bf, *, n_aligned, shift, tiles_per_seg):
    j = pl.program_id(0)
    i = pl.program_id(1)
    tn = w_ref.shape[1]

    @pl.when((i == 0) & (j < n_aligned))
    def _():
        w_bf[...] = w_ref[...].astype(BF16)

    @pl.when((i == 0) & (j >= n_aligned))
    def _():
        rolled = pltpu.roll(w_ref[...], tn - shift, axis=1)
        wxr = pltpu.roll(wx_ref[...], LANES - shift, axis=1)
        lane = lax.broadcasted_iota(I32, wxr.shape, 1)
        last = jnp.where(lane >= LANES - shift, wxr, rolled[:, tn - LANES:])
        w_bf[...] = jnp.concatenate([rolled[:, :tn - LANES], last], axis=1).astype(BF16)

    seg = j // tiles_per_seg
    is_raw = seg < 2
    is_gelu = (seg == 3) | (seg == 4)
    c0 = 0.7978845608028654
    p0 = jnp.where(is_gelu, c0, 0.5).astype(F32)
    p1 = jnp.where(is_gelu, c0 * 0.044715, 0.0).astype(F32)
    q0 = jnp.where(is_gelu, 0.0, 1.0).astype(F32)
    q1 = jnp.where(is_gelu, 1.0, 0.0).astype(F32)
    w = w_bf[...]
    for c in range(h_ref.shape[0] // INPROJ_ROWS):
        rs = slice(c * INPROJ_ROWS, (c + 1) * INPROJ_ROWS)
        z = jnp.dot(h_ref[rs, :], w, preferred_element_type=F32)
        act = (0.5 * (q0 + q1 * z)) * (1.0 + jnp.tanh(z * (p0 + p1 * (z * z))))
        z_ref[rs, :] = jnp.where(is_raw, z, act).astype(BF16)


def _inproj(h, w_in, layer, n_aligned, shift, n_tiles, tm):
    r, d = h.shape
    tn = INPROJ_TN
    kern = functools.partial(_inproj_kernel, n_aligned=n_aligned, shift=shift, tiles_per_seg=d // tn)
    return pl.pallas_call(
        kern,
        grid=(n_tiles, r // tm),
        in_specs=[pl.BlockSpec((tm, d), lambda j, i: (i, 0)),
                  pl.BlockSpec((None, d, tn), lambda j, i: (layer, 0, j)),
                  pl.BlockSpec((None, d, LANES), lambda j, i: (layer, 0, (tn // LANES) * (j + 1)))],
        out_specs=pl.BlockSpec((tm, tn), lambda j, i: (i, j)),
        out_shape=jax.ShapeDtypeStruct((r, n_tiles * tn), BF16),
        scratch_shapes=[pltpu.VMEM((d, tn), BF16)],
        compiler_params=_cparams(("arbitrary", "arbitrary")),
        name="inproj",
    )(h, w_in, w_in)


def _conv_kernel(x_ref, w_ref, b_ref, o_ref, *, q_tiles):
    c = pl.program_id(1)
    x = x_ref[...].astype(F32)
    t = x.shape[0]
    rows = lax.broadcasted_iota(I32, x.shape, 0)
    w = w_ref[...]
    acc = x * w[CONV_W // 2:CONV_W // 2 + 1, :] + b_ref[...]
    for dlt in range(-(CONV_W // 2), CONV_W // 2 + 1):
        if dlt == 0:
            continue
        xs = pltpu.roll(x, shift=(-dlt) % t, axis=0)
        valid = (rows + dlt >= 0) & (rows + dlt < t)
        acc = acc + jnp.where(valid, xs, 0.0) * w[dlt + CONV_W // 2:dlt + CONV_W // 2 + 1, :]
    y = acc * _sigmoid(acc)
    scale = jnp.where(c < q_tiles, QK_DIM ** -0.5, 1.0).astype(F32)
    o_ref[...] = (y * scale).astype(BF16)


def _conv(z3, conv_w, conv_b):
    b, t, _ = z3.shape
    width = conv_w.shape[1]
    tc = 256
    kern = functools.partial(_conv_kernel, q_tiles=(width // 2) // tc)
    return pl.pallas_call(
        kern,
        grid=(b, width // tc),
        in_specs=[pl.BlockSpec((None, t, tc), lambda i, c: (i, 0, c)),
                  pl.BlockSpec((CONV_W, tc), lambda i, c: (0, c)),
                  pl.BlockSpec((1, tc), lambda i, c: (0, c))],
        out_specs=pl.BlockSpec((None, t, tc), lambda i, c: (i, 0, c)),
        out_shape=jax.ShapeDtypeStruct((b, t, width), BF16),
        compiler_params=_cparams(("arbitrary", "arbitrary")),
        name="conv",
    )(z3, conv_w, conv_b.reshape(1, width))


def _mlstm_kernel(qkc_f, qkl_f, vc_f, vl_f, gc_f, gl_f, qkc_b, qkl_b, vc_b, vl_b, gc_b, gl_b,
                  hf_ref, hb_ref, *state, n_ctx_chunks):
    c_scr, m_scr = state[:2 * M_HEADS], state[2 * M_HEADS:]
    s = pl.program_id(1)

    @pl.when(s == 0)
    def _():
        for ref in state:
            ref[...] = jnp.zeros_like(ref)

    is_ctx = s < n_ctx_chunks
    rows = lax.broadcasted_iota(I32, (CHUNK, CHUNK), 0)
    lanes = lax.broadcasted_iota(I32, (CHUNK, CHUNK), 1)
    ones_col = jnp.where(lanes == 0, 1.0, 0.0).astype(BF16)
    dn_nt = (((1,), (1,)), ((), ()))
    dn_tn = (((0,), (0,)), ((), ()))
    qk_w = M_HEADS * QK_DIM

    dirs = ((qkc_f, qkl_f, vc_f, vl_f, gc_f, gl_f, hf_ref), (qkc_b, qkl_b, vc_b, vl_b, gc_b, gl_b, hb_ref))
    for d, (qkc, qkl, vc, vl, gc, gl, out_ref) in enumerate(dirs):
        qk = jnp.where(is_ctx, qkc[...], qkl[...])
        v = jnp.where(is_ctx, vc[...], vl[...])
        g = jnp.where(is_ctx, gc[...], gl[...])
        mask = (lanes <= rows) if d == 0 else (lanes >= rows)
        tri = jnp.where(mask, 1.0, 0.0).astype(BF16)
        ls = jnp.minimum(g, 0.0) - jnp.log(1.0 + jnp.exp(-jnp.abs(g)))
        l1, l2, l3 = _split3(ls)
        bcol = (jnp.dot(tri, l1, preferred_element_type=F32) + jnp.dot(tri, l2, preferred_element_type=F32)
                + jnp.dot(tri, l3, preferred_element_type=F32))
        brow = bcol.T
        grow = g.T
        for h in range(M_HEADS):
            col_i = 2 * M_HEADS * d + h
            col_f = col_i + M_HEADS
            idx = d * M_HEADS + h
            bc = bcol[:, col_f:col_f + 1]
            br = brow[col_f:col_f + 1, :]
            ir = grow[col_i:col_i + 1, :]
            ic = g[:, col_i:col_i + 1]
            m = m_scr[idx][0:1, 0:1]
            dlog = jnp.where(mask, bc - br + ir, -jnp.inf)
            inter = bc + m
            m_t = jnp.maximum(inter, jnp.max(dlog, axis=1, keepdims=True))
            w_inter = jnp.exp(inter - m_t)
            q = qk[:, h * QK_DIM:(h + 1) * QK_DIM]
            k = qk[:, qk_w + h * QK_DIM:qk_w + (h + 1) * QK_DIM]
            vaug = jnp.concatenate([v[:, h * V_DIM:(h + 1) * V_DIM], ones_col], axis=1)
            sc = lax.dot_general(q, k, dn_nt, preferred_element_type=F32) * jnp.exp(dlog - m_t)
            caug = c_scr[idx][...]
            num = (w_inter * jnp.dot(q, caug.astype(BF16), preferred_element_type=F32)
                   + jnp.dot(sc.astype(BF16), vaug, preferred_element_type=F32))
            den = num[:, V_DIM:V_DIM + 1]
            hout = num[:, :V_DIM] / jnp.maximum(jnp.abs(den), jnp.exp(-m_t))
            b_end = br[:, CHUNK - 1:CHUNK] if d == 0 else br[:, 0:1]
            m_new = jnp.maximum(b_end + m, jnp.max(b_end - br + ir, axis=1, keepdims=True))
            decay = jnp.exp(b_end + m - m_new)
            w_col = jnp.exp(b_end - bc + ic - m_new)
            kw = (k.astype(F32) * w_col).astype(BF16)
            c_scr[idx][...] = decay * caug + lax.dot_general(kw, vaug, dn_tn, preferred_element_type=F32)
            m_scr[idx][...] = jnp.broadcast_to(m_new, m_scr[idx].shape)
            out_ref[:, h * V_DIM:(h + 1) * V_DIM] = hout.astype(out_ref.dtype)


def _mlstm(qk_ctx, qk_lat, z_ctx3, z_lat3, g_ctx3, g_lat3):
    b, tc, _ = qk_ctx.shape
    t = qk_lat.shape[1]
    nc, nl = tc // CHUNK, t // CHUNK
    vw = M_HEADS * V_DIM
    qkw = 2 * M_HEADS * QK_DIM

    f_ctx = lambda i, s: (i, jnp.minimum(s, nc - 1), 0)
    f_lat = lambda i, s: (i, jnp.clip(s - nc, 0, nl - 1), 0)
    b_ctx = lambda i, s: (i, jnp.maximum(nc - 1 - s, 0), 0)
    b_lat = lambda i, s: (i, jnp.clip(nc + nl - 1 - s, 0, nl - 1), 0)
    v_of = lambda f: (lambda i, s: (f(i, s)[0], f(i, s)[1], 1))

    def specs(fc, fl):
        return [pl.BlockSpec((None, CHUNK, qkw), fc), pl.BlockSpec((None, CHUNK, qkw), fl),
                pl.BlockSpec((None, CHUNK, vw), v_of(fc)), pl.BlockSpec((None, CHUNK, vw), v_of(fl)),
                pl.BlockSpec((None, CHUNK, LANES), fc), pl.BlockSpec((None, CHUNK, LANES), fl)]

    kern = functools.partial(_mlstm_kernel, n_ctx_chunks=nc)
    args = (qk_ctx, qk_lat, z_ctx3, z_lat3, g_ctx3, g_lat3)
    return pl.pallas_call(
        kern,
        grid=(b, nc + nl),
        in_specs=specs(f_ctx, f_lat) + specs(b_ctx, b_lat),
        out_specs=[pl.BlockSpec((None, CHUNK, vw), f_lat), pl.BlockSpec((None, CHUNK, vw), b_lat)],
        out_shape=[jax.ShapeDtypeStruct((b, t, vw), BF16), jax.ShapeDtypeStruct((b, t, vw), BF16)],
        scratch_shapes=([pltpu.VMEM((QK_DIM, V_DIM + LANES), F32)] * (2 * M_HEADS)
                        + [pltpu.VMEM((8, LANES), F32)] * (2 * M_HEADS)),
        compiler_params=_cparams(("arbitrary", "arbitrary")),
        name="mlstm",
    )(*args, *args)


MERGE_ROWS = 256


def _merge_kernel(hf_ref, hb_ref, o_ref, u_ref, sv_ref, ga_ref, gb_ref, gm_ref, gs_ref, sw_ref, sb_ref,
                  pa_ref, pb_ref, out_ref):
    tm, d = out_ref.shape
    gw = d // SGU_GROUPS
    for grp in range(tm // MERGE_ROWS):
        r0 = grp * MERGE_ROWS
        rs = slice(r0, r0 + MERGE_ROWS)
        a_parts = []
        for h in range(M_HEADS):
            sl = slice(h * V_DIM, (h + 1) * V_DIM)
            hs = hf_ref[rs, sl].astype(F32) + hb_ref[rs, sl].astype(F32)
            ms = jnp.mean(hs * hs, axis=-1, keepdims=True)
            y = hs * lax.rsqrt(ms + EPS) * gm_ref[:, sl]
            a_parts.append((y * o_ref[rs, sl].astype(F32)).astype(BF16))
        a = jnp.concatenate(a_parts, axis=1)
        sv = sv_ref[rs, :].astype(F32)
        ms = jnp.mean(sv * sv, axis=-1, keepdims=True)
        svn = (sv * lax.rsqrt(ms + EPS) * gs_ref[...]).astype(BF16)
        b_rows = []
        for c in range(MERGE_ROWS // CHUNK):
            cr = slice(c * CHUNK, (c + 1) * CHUNK)
            b_parts = []
            for gi in range(SGU_GROUPS):
                cs = slice(gi * gw, (gi + 1) * gw)
                zz = jnp.dot(sw_ref[gi], svn[cr, cs], preferred_element_type=F32) + sb_ref[:, gi:gi + 1]
                b_parts.append((u_ref[r0 + c * CHUNK:r0 + (c + 1) * CHUNK, cs].astype(F32) * zz).astype(BF16))
            b_rows.append(jnp.concatenate(b_parts, axis=1))
        b = jnp.concatenate(b_rows, axis=0)
        ya = jnp.dot(a, pa_ref[...], preferred_element_type=F32)
        yb = jnp.dot(b, pb_ref[...], preferred_element_type=F32)
        out_ref[rs, :] = (ga_ref[rs, :].astype(F32) * ya + gb_ref[rs, :].astype(F32) * yb).astype(BF16)


def _merge(hf, hb, z, gm, gs, sgu_w, sgu_bt, pa, pb):
    r, d = hf.shape
    tm = 512
    row = lambda k: (lambda i: (i, k))
    const = lambda shape: pl.BlockSpec(shape, lambda i: (0,) * len(shape), pipeline_mode=pl.Buffered(1))
    return pl.pallas_call(
        _merge_kernel,
        grid=(r // tm,),
        in_specs=[pl.BlockSpec((tm, d), row(0)), pl.BlockSpec((tm, d), row(0)),
                  pl.BlockSpec((tm, d), row(2)), pl.BlockSpec((tm, d), row(3)), pl.BlockSpec((tm, d), row(4)),
                  pl.BlockSpec((tm, d), row(5)), pl.BlockSpec((tm, d), row(6)),
                  const((1, d)), const((1, d)), const(sgu_w.shape), const(sgu_bt.shape),
                  const((d, d)), const((d, d))],
        out_specs=pl.BlockSpec((tm, d), row(0)),
        out_shape=jax.ShapeDtypeStruct((r, d), BF16),
        compiler_params=_cparams(("arbitrary",)),
        name="merge",
    )(hf, hb, z, z, z, z, z, gm, gs, sgu_w, sgu_bt, pa, pb)


def _outproj_kernel(m_ref, w_ref, x_ref, gate_ref, shift_ref, scale_ref, g_ref, rw_ref, rb_ref,
                    x1_ref, h2_ref, lg_ref):
    r1, r2, _ = _split3(rw_ref[...])
    dn_nt = (((1,), (1,)), ((), ()))
    for grp in range(x_ref.shape[0] // MERGE_ROWS):
        r0 = grp * MERGE_ROWS
        rs = slice(r0, r0 + MERGE_ROWS)
        out = jnp.dot(m_ref[rs, :], w_ref[...], preferred_element_type=F32)
        x1 = x_ref[rs, :] + gate_ref[...] * out
        x1_ref[rs, :] = x1
        ms = jnp.mean(x1 * x1, axis=-1, keepdims=True)
        h2 = (x1 * lax.rsqrt(ms + EPS) * g_ref[...]) * (1.0 + scale_ref[...]) + shift_ref[...]
        hi, lo, _ = _split3(h2)
        lg_ref[:, rs] = (lax.dot_general(r1, hi, dn_nt, preferred_element_type=F32)
                         + lax.dot_general(r1, lo, dn_nt, preferred_element_type=F32)
                         + lax.dot_general(r2, hi, dn_nt, preferred_element_type=F32)) + rb_ref[...]
        _store_token_major(h2_ref, r0, h2)


def _outproj(merged, w_out, x2d, mod3, mod_row_of_tile, norm_g, rw_t, rb_col, tm):
    r, d = x2d.shape
    ne = rw_t.shape[0]
    modspec = lambda k: pl.BlockSpec((None, 1, d), lambda i: (mod_row_of_tile(i), 0, k))
    return pl.pallas_call(
        _outproj_kernel,
        grid=(r // tm,),
        in_specs=[pl.BlockSpec((tm, d), lambda i: (i, 0)),
                  pl.BlockSpec((d, d), lambda i: (0, 0), pipeline_mode=pl.Buffered(1)),
                  pl.BlockSpec((tm, d), lambda i: (i, 0)),
                  modspec(2), modspec(3), modspec(4),
                  pl.BlockSpec((1, d), lambda i: (0, 0)),
                  pl.BlockSpec((ne, d), lambda i: (0, 0)),
                  pl.BlockSpec((ne, 1), lambda i: (0, 0))],
        out_specs=[pl.BlockSpec((tm, d), lambda i: (i, 0)),
                   pl.BlockSpec((tm * TOKEN_ROWS, LANES), lambda i: (i, 0)),
                   pl.BlockSpec((ne, tm), lambda i: (0, i))],
        out_shape=[jax.ShapeDtypeStruct((r, d), F32),
                   jax.ShapeDtypeStruct((r * TOKEN_ROWS, LANES), F32),
                   jax.ShapeDtypeStruct((ne, r), F32)],
        compiler_params=_cparams(("arbitrary",)),
        name="outproj",
    )(merged, w_out, x2d, mod3, mod3, mod3, norm_g, rw_t, rb_col)


def _route_kernel(lg_ref, eid_ref, gate_ref, rank_ref, cnt_ref, carry_scr):
    i = pl.program_id(0)

    @pl.when(i == 0)
    def _():
        carry_scr[...] = jnp.zeros_like(carry_scr)

    l = lg_ref[...]
    ne, tm = l.shape
    e_iota = lax.broadcasted_iota(I32, (ne, tm), 0)
    vals, onehots = [], []
    for k in range(TOP_K):
        mx = jnp.max(l, axis=0, keepdims=True)
        idx = jnp.min(jnp.where(l == mx, e_iota, ne), axis=0, keepdims=True)
        oh = e_iota == idx
        l = jnp.where(oh, -jnp.inf, l)
        vals.append(mx)
        onehots.append(oh)
        eid_ref[k:k + 1, :] = idx
    ex = [jnp.exp(vk - vals[0]) for vk in vals]
    tot = ex[0] + ex[1] + ex[2] + ex[3]
    for k in range(TOP_K):
        gate_ref[k:k + 1, :] = ex[k] / tot

    oh_all = jnp.zeros((ne, tm), F32)
    for oh in onehots:
        oh_all = oh_all + jnp.where(oh, 1.0, 0.0)
    oh_all = oh_all.astype(BF16)
    r_i = lax.broadcasted_iota(I32, (LANES, LANES), 0)
    c_i = lax.broadcasted_iota(I32, (LANES, LANES), 1)
    tri_excl = jnp.where(r_i < c_i, 1.0, 0.0).astype(BF16)
    ones_m = jnp.ones((LANES, LANES), BF16)
    carry = carry_scr[...]
    for blk in range(tm // LANES):
        sl = slice(blk * LANES, (blk + 1) * LANES)
        ohb = oh_all[:, sl]
        cum = jnp.dot(ohb, tri_excl, preferred_element_type=F32) + carry
        for k in range(TOP_K):
            rk = jnp.sum(jnp.where(onehots[k][:, sl], cum, 0.0), axis=0, keepdims=True)
            rank_ref[k:k + 1, sl] = rk.astype(I32)
        carry = carry + jnp.dot(ohb, ones_m, preferred_element_type=F32)
    carry_scr[...] = carry
    cnt_ref[...] = carry


def _route(logits_t):
    ne, r = logits_t.shape
    tm = 1024
    return pl.pallas_call(
        _route_kernel,
        grid=(r // tm,),
        in_specs=[pl.BlockSpec((ne, tm), lambda i: (0, i))],
        out_specs=[pl.BlockSpec((TOP_K, tm), lambda i: (0, i)),
                   pl.BlockSpec((TOP_K, tm), lambda i: (0, i)),
                   pl.BlockSpec((TOP_K, tm), lambda i: (0, i)),
                   pl.BlockSpec((ne, LANES), lambda i: (0, 0))],
        out_shape=[jax.ShapeDtypeStruct((TOP_K, r), I32),
                   jax.ShapeDtypeStruct((TOP_K, r), F32),
                   jax.ShapeDtypeStruct((TOP_K, r), I32),
                   jax.ShapeDtypeStruct((ne, LANES), F32)],
        scratch_shapes=[pltpu.VMEM((ne, LANES), F32)],
        compiler_params=_cparams(("arbitrary",)),
        name="route",
    )(logits_t)


def _slot_kernel(eid_ref, rank_ref, cnt_ref, slot_ref):
    ne = cnt_ref.shape[0]
    nblk = jnp.floor((cnt_ref[...] + (MOE_TILE - 1.0)) * (1.0 / MOE_TILE))
    r_i = lax.broadcasted_iota(I32, (ne, ne), 0)
    c_i = lax.broadcasted_iota(I32, (ne, ne), 1)
    tri = jnp.where(c_i < r_i, 1.0, 0.0).astype(BF16)
    start = jnp.dot(tri, nblk.astype(BF16), preferred_element_type=F32) * MOE_TILE
    eid = eid_ref[...]
    acc = rank_ref[...]
    for e in range(ne):
        acc = acc + jnp.where(eid == e, start[e:e + 1, 0:1].astype(I32), 0)
    slot_ref[...] = acc


def _slots(eid, rank, cnt):
    return pl.pallas_call(
        _slot_kernel,
        out_shape=jax.ShapeDtypeStruct(eid.shape, I32),
        compiler_params=pltpu.CompilerParams(vmem_limit_bytes=V7X_VMEM_LIMIT_BYTES),
        name="slots",
    )(eid, rank, cnt)


def _dispatch_copy(slot_ref, h_ref, xs_ref, sem, k, t, base, n_tok):
    d0 = pl.multiple_of(slot_ref[k * n_tok + base + t] * TOKEN_ROWS, TOKEN_ROWS)
    s0 = pl.multiple_of(t * TOKEN_ROWS, TOKEN_ROWS)
    return pltpu.make_async_copy(h_ref.at[pl.ds(s0, TOKEN_ROWS), :], xs_ref.at[pl.ds(d0, TOKEN_ROWS), :], sem)


def _dispatch_kernel(slot_ref, h_ref, init_ref, xs_ref, sem, *, tm, n_tok):
    base = pl.program_id(0) * tm

    def start(t, carry):
        for k in range(TOP_K):
            _dispatch_copy(slot_ref, h_ref, xs_ref, sem, k, t, base, n_tok).start()
        return carry

    def wait(t, carry):
        for k in range(TOP_K):
            _dispatch_copy(slot_ref, h_ref, xs_ref, sem, k, t, base, n_tok).wait()
        return carry

    lax.fori_loop(0, tm, start, 0)
    lax.fori_loop(0, tm, wait, 0)


def _dispatch(h2p, slot, p_rows):
    n_tok = h2p.shape[0] // TOKEN_ROWS
    tm = 512
    init = jnp.zeros((p_rows * TOKEN_ROWS, LANES), h2p.dtype)
    grid_spec = pltpu.PrefetchScalarGridSpec(
        num_scalar_prefetch=1,
        grid=(n_tok // tm,),
        in_specs=[pl.BlockSpec((tm * TOKEN_ROWS, LANES), lambda i, slot: (i, 0)),
                  pl.BlockSpec(memory_space=pl.ANY)],
        out_specs=pl.BlockSpec(memory_space=pl.ANY),
        scratch_shapes=[pltpu.SemaphoreType.DMA(())],
    )
    return pl.pallas_call(
        functools.partial(_dispatch_kernel, tm=tm, n_tok=n_tok),
        grid_spec=grid_spec,
        out_shape=jax.ShapeDtypeStruct(init.shape, init.dtype),
        input_output_aliases={2: 0},
        compiler_params=_cparams(("arbitrary",)),
        name="dispatch",
    )(slot, h2p, init)


def _moe_up_kernel(blk_ref, oblk_ref, e_ref, nv_ref, new_ref, xs_ref, wg_ref, wl_ref, bg_ref, bl_ref, act_ref,
                   wg_bf, wl_bf):
    it = pl.program_id(1)

    @pl.when(new_ref[it] == 1)
    def _():
        wg_bf[...] = wg_ref[...].astype(BF16)
        wl_bf[...] = wl_ref[...].astype(BF16)

    def rows_chunk(r0, n):
        x = _load_token_major(xs_ref, r0, n).astype(BF16)
        hg = jnp.dot(x, wg_bf[...], preferred_element_type=F32) + bg_ref[...]
        hl = jnp.dot(x, wl_bf[...], preferred_element_type=F32) + bl_ref[...]
        glu = jnp.minimum(hg, SWIGLU_LIMIT)
        lin = jnp.clip(hl, -SWIGLU_LIMIT, SWIGLU_LIMIT)
        act_ref[r0:r0 + n, :] = ((0.5 * glu) * (1.0 + jnp.tanh((0.5 * SWIGLU_ALPHA) * glu)) * (lin + 1.0)).astype(BF16)

    def rows_zero(r0, n):
        act_ref[r0:r0 + n, :] = jnp.zeros((n, act_ref.shape[1]), BF16)

    _moe_row_variants(nv_ref[it], rows_chunk, rows_zero)


def _moe_row_variants(nv, rows_chunk, rows_zero):
    for units in range(MOE_TILE // MOE_STEP + 1):
        rows = units * MOE_STEP

        @pl.when((nv > rows - MOE_STEP) & (nv <= rows) if units else nv <= 0)
        def _():
            r0 = 0
            while r0 < rows:
                n = min(MOE_HALF, rows - r0)
                rows_chunk(r0, n)
                r0 += n
            if rows < MOE_TILE:
                rows_zero(rows, MOE_TILE - rows)


def _moe_down_kernel(blk_ref, oblk_ref, e_ref, nv_ref, new_ref, a_ref, w_ref, b_ref, y_ref, w_bf):
    it = pl.program_id(0)

    @pl.when(new_ref[it] == 1)
    def _():
        w_bf[...] = w_ref[...].astype(BF16)

    def rows_chunk(r0, n):
        y = jnp.dot(a_ref[r0:r0 + n, :], w_bf[...], preferred_element_type=F32) + b_ref[...]
        _store_token_major(y_ref, r0, y)

    def rows_zero(r0, n):
        y_ref[r0 * TOKEN_ROWS:(r0 + n) * TOKEN_ROWS, :] = jnp.zeros((n * TOKEN_ROWS, LANES), F32)

    _moe_row_variants(nv_ref[it], rows_chunk, rows_zero)


def _moe_up(items, xs, w1, b1):
    n_items = items[0].shape[0]
    ne, d, two_f = w1.shape
    f = two_f // 2
    nj = f // MOE_TN
    p = xs.shape[0] // TOKEN_ROWS
    b1r = b1.reshape(ne, 1, two_f)
    grid_spec = pltpu.PrefetchScalarGridSpec(
        num_scalar_prefetch=len(items),
        grid=(nj, n_items),
        in_specs=[pl.BlockSpec((MOE_TILE * TOKEN_ROWS, LANES), lambda j, it, blk, oblk, e, nv, new: (blk[it], 0)),
                  pl.BlockSpec((None, d, MOE_TN), lambda j, it, blk, oblk, e, nv, new: (e[it], 0, j)),
                  pl.BlockSpec((None, d, MOE_TN), lambda j, it, blk, oblk, e, nv, new: (e[it], 0, nj + j)),
                  pl.BlockSpec((None, 1, MOE_TN), lambda j, it, blk, oblk, e, nv, new: (e[it], 0, j)),
                  pl.BlockSpec((None, 1, MOE_TN), lambda j, it, blk, oblk, e, nv, new: (e[it], 0, nj + j))],
        out_specs=pl.BlockSpec((MOE_TILE, MOE_TN), lambda j, it, blk, oblk, e, nv, new: (oblk[it], j)),
        scratch_shapes=[pltpu.VMEM((d, MOE_TN), BF16), pltpu.VMEM((d, MOE_TN), BF16)],
    )
    return pl.pallas_call(
        _moe_up_kernel,
        grid_spec=grid_spec,
        out_shape=jax.ShapeDtypeStruct((p, f), BF16),
        compiler_params=_cparams(("arbitrary", "arbitrary")),
        name="moe_up",
    )(*items, xs, w1, w1, b1r, b1r)


def _moe_down(items, act, w2, b2):
    n_items = items[0].shape[0]
    ne, f, d = w2.shape
    p = act.shape[0]
    b2r = b2.reshape(ne, 1, d)
    grid_spec = pltpu.PrefetchScalarGridSpec(
        num_scalar_prefetch=len(items),
        grid=(n_items,),
        in_specs=[pl.BlockSpec((MOE_TILE, f), lambda it, blk, oblk, e, nv, new: (blk[it], 0)),
                  pl.BlockSpec((None, f, d), lambda it, blk, oblk, e, nv, new: (e[it], 0, 0)),
                  pl.BlockSpec((None, 1, d), lambda it, blk, oblk, e, nv, new: (e[it], 0, 0))],
        out_specs=pl.BlockSpec((MOE_TILE * TOKEN_ROWS, LANES), lambda it, blk, oblk, e, nv, new: (oblk[it], 0)),
        scratch_shapes=[pltpu.VMEM((f, d), BF16)],
    )
    return pl.pallas_call(
        _moe_down_kernel,
        grid_spec=grid_spec,
        out_shape=jax.ShapeDtypeStruct((p * TOKEN_ROWS, LANES), F32),
        compiler_params=_cparams(("arbitrary",)),
        name="moe_down",
    )(*items, act, w2, b2r)


def _combine_copy(slot_ref, y_ref, buf, sems, k, t, tile, par, tm, n_tok):
    s0 = pl.multiple_of(slot_ref[k * n_tok + tile * tm + t] * TOKEN_ROWS, TOKEN_ROWS)
    d0 = pl.multiple_of(t * TOKEN_ROWS, TOKEN_ROWS)
    return pltpu.make_async_copy(y_ref.at[pl.ds(s0, TOKEN_ROWS), :], buf.at[par, k, pl.ds(d0, TOKEN_ROWS), :],
                                 sems.at[par])


def _final_kernel(slot_ref, y_ref, gates_ref, x1_ref, gate_ref, g_ref, o_ref, buf, sems, *, n_tok):
    tm, d = x1_ref.shape
    i = pl.program_id(0)
    par = i % 2

    def gather(tile, par_, start):
        def body(t, carry):
            for k in range(TOP_K):
                cp = _combine_copy(slot_ref, y_ref, buf, sems, k, t, tile, par_, tm, n_tok)
                cp.start() if start else cp.wait()
            return carry
        lax.fori_loop(0, tm, body, 0)

    @pl.when(i == 0)
    def _():
        gather(0, 0, True)

    @pl.when(i + 1 < pl.num_programs(0))
    def _():
        gather(i + 1, 1 - par, True)

    gather(i, par, False)

    acc = jnp.zeros((tm, d), F32)
    for k in range(TOP_K):
        acc = acc + gates_ref[:, k:k + 1] * _load_token_major(buf, 0, tm, lead=(par, k))
    xf = x1_ref[...] + gate_ref[...] * acc
    ms = jnp.mean(xf * xf, axis=-1, keepdims=True)
    o_ref[...] = xf * lax.rsqrt(ms + EPS) * g_ref[...]


def _final(slot, y, gates_tk, x1, mod3, mod_row_of_tile, final_g, tm):
    r, d = x1.shape
    grid_spec = pltpu.PrefetchScalarGridSpec(
        num_scalar_prefetch=1,
        grid=(r // tm,),
        in_specs=[pl.BlockSpec(memory_space=pl.ANY),
                  pl.BlockSpec((tm, TOP_K), lambda i, slot: (i, 0)),
                  pl.BlockSpec((tm, d), lambda i, slot: (i, 0)),
                  pl.BlockSpec((None, 1, d), lambda i, slot: (mod_row_of_tile(i), 0, 5)),
                  pl.BlockSpec((1, d), lambda i, slot: (0, 0))],
        out_specs=pl.BlockSpec((tm, d), lambda i, slot: (i, 0)),
        scratch_shapes=[pltpu.VMEM((2, TOP_K, tm * TOKEN_ROWS, LANES), F32), pltpu.SemaphoreType.DMA((2,))],
    )
    return pl.pallas_call(
        functools.partial(_final_kernel, n_tok=r),
        grid_spec=grid_spec,
        out_shape=jax.ShapeDtypeStruct((r, d), F32),
        compiler_params=_cparams(("arbitrary",)),
        name="final",
    )(slot, y, gates_tk, x1, mod3, final_g)


def _moe_schedule(counts, n_items):
    ne = counts.shape[0]
    padded = (counts + MOE_TILE - 1) // MOE_TILE * MOE_TILE
    pad_end = jnp.cumsum(padded)
    start_pad = pad_end - padded
    n_real = pad_end[-1] // MOE_TILE
    b = jnp.arange(n_items, dtype=I32)
    blk = jnp.minimum(b, jnp.maximum(n_real - 1, 0))
    blk_start = blk * MOE_TILE
    e_of = jnp.minimum(jnp.sum((pad_end[None, :] <= blk_start[:, None]).astype(I32), axis=1), ne - 1)
    nv = jnp.clip(counts[e_of] - (blk_start - start_pad[e_of]), 0, MOE_TILE)
    nv = jnp.where(b < n_real, nv, 0)
    prev_e = jnp.concatenate([jnp.full((1,), -1, I32), e_of[:-1]])
    new = ((e_of != prev_e) | (b == 0)).astype(I32)
    return start_pad, (blk.astype(I32), b, e_of.astype(I32), nv.astype(I32), new)


def kernel(x, c, ctx, c_ctx, mod_w, mod_b, norm1_g, norm2_g, w_in, gate_b, conv_w, conv_b, mlstm_norm_g,
           sgu_norm_g, sgu_w, sgu_b, proj_a, proj_b, w_out, router_w, router_b, exp_w1, exp_b1, exp_w2,
           exp_b2, final_g):
    bsz, t, d = x.shape
    tc = ctx.shape[1]
    depth = mod_w.shape[0]
    assert depth == 1, "single-layer block"
    assert d == TOKEN_ROWS * LANES, "token-major rows hold TOKEN_ROWS*128 features"
    l = 0
    r = bsz * t
    qkw = 2 * M_HEADS * QK_DIM
    vw = M_HEADS * V_DIM
    off_gates = qkw + vw
    n_gates = 4 * M_HEADS
    off_o = off_gates + n_gates

    cc = jnp.zeros((8, d), F32).at[:bsz].set(c).at[bsz].set(c_ctx)
    mod3 = _mod(cc, mod_w[l], mod_b[l]).reshape(8, 1, N_MOD * d)

    in_cols = w_in.shape[2]
    assert off_gates % INPROJ_TN == 0 and (in_cols - off_o) % INPROJ_TN == 0
    n_aligned = off_gates // INPROJ_TN
    n_tiles = n_aligned + (in_cols - off_o) // INPROJ_TN
    w_gates = jnp.pad(w_in[l, :, off_gates:off_o], ((0, 0), (0, LANES - n_gates)))
    gb = jnp.pad(gate_b[l].reshape(1, n_gates), ((0, 0), (0, LANES - n_gates)))
    g1 = norm1_g[l].reshape(1, d)

    tm_in = 512
    h_lat, g_lat = _norm(x.reshape(r, d), mod3, lambda i: i // (t // tm_in), g1, w_gates, gb, tm_in)
    h_ctx, g_ctx = _norm(ctx.reshape(bsz * tc, d), mod3, lambda i: bsz, g1, w_gates, gb, tc)
    z_lat = _inproj(h_lat, w_in, l, n_aligned, n_gates, n_tiles, 1024)
    z_ctx = _inproj(h_ctx, w_in, l, n_aligned, n_gates, n_aligned, bsz * tc)

    z_lat3 = z_lat.reshape(bsz, t, z_lat.shape[1])
    z_ctx3 = z_ctx.reshape(bsz, tc, z_ctx.shape[1])
    qk_lat = _conv(z_lat3, conv_w[l], conv_b[l])
    qk_ctx = _conv(z_ctx3, conv_w[l], conv_b[l])

    hf, hb = _mlstm(qk_ctx, qk_lat, z_ctx3, z_lat3, g_ctx.reshape(bsz, tc, LANES), g_lat.reshape(bsz, t, LANES))

    merged = _merge(hf.reshape(r, vw), hb.reshape(r, vw), z_lat,
                    mlstm_norm_g[l].reshape(1, vw), sgu_norm_g[l].reshape(1, -1),
                    sgu_w[l].astype(BF16), sgu_b[l].T, proj_a[l].astype(BF16), proj_b[l].astype(BF16))

    tm_out = 512
    x1, h2p, logits_t = _outproj(merged, w_out[l].astype(BF16), x.reshape(r, d), mod3,
                                 lambda i: i // (t // tm_out), norm2_g[l].reshape(1, d),
                                 router_w[l].T, router_b[l].reshape(-1, 1), tm_out)
    tm_fin = 256

    eid, gates, rank, cnt = _route(logits_t)
    counts = cnt[:, 0].astype(I32)
    n_items = (r * TOP_K) // MOE_TILE + N_EXPERTS
    _, items = _moe_schedule(counts, n_items)
    slot = _slots(eid, rank, cnt).reshape(-1)
    xs = _dispatch(h2p, slot, n_items * MOE_TILE)
    act = _moe_up(items, xs, exp_w1[l], exp_b1[l])
    y = _moe_down(items, act, exp_w2[l], exp_b2[l])
    out = _final(slot, y, gates.T, x1, mod3, lambda i: i // (t // tm_fin), final_g.reshape(1, d), tm_fin)
    return out.reshape(bsz, t, d)
```

```python
import functools

import jax
import jax.numpy as jnp
from jax import lax
from jax.experimental import pallas as pl
from jax.experimental.pallas import tpu as pltpu

F32 = jnp.float32
BF16 = jnp.bfloat16
I32 = jnp.int32
U32 = jnp.uint32

EPS = 1e-6
M_HEADS = 8
QK_DIM = 128
V_DIM = 256
CONV_W = 5
CHUNK = 128
SGU_GROUPS = 8
N_EXPERTS = 32
TOP_K = 4
SWIGLU_ALPHA = 1.702
SWIGLU_LIMIT = 7.0
N_MOD = 6

V7X_VMEM_LIMIT_BYTES = 56 * 1024 * 1024
LANES = 128
MOE_TILE = 512
MOE_HALF = 256
MOE_STEP = 128
MOE_TN = 1024


def _cparams(sem):
    return pltpu.CompilerParams(dimension_semantics=sem, vmem_limit_bytes=V7X_VMEM_LIMIT_BYTES)


def _sigmoid(x):
    return 1.0 / (1.0 + jnp.exp(-x))


def _gelu_tanh(x):
    return 0.5 * x * (1.0 + jnp.tanh(0.7978845608028654 * (x + 0.044715 * (x * x * x))))


def _split3(a):
    a1 = a.astype(BF16)
    r1 = a - a1.astype(F32)
    a2 = r1.astype(BF16)
    a3 = (r1 - a2.astype(F32)).astype(BF16)
    return a1, a2, a3


TOKEN_ROWS = 16


def _store_token_major(ref, row0, vals):
    n = vals.shape[0]
    for s in range(TOKEN_ROWS):
        ref[pl.ds(row0 * TOKEN_ROWS + s, n, stride=TOKEN_ROWS), :] = vals[:, s * LANES:(s + 1) * LANES]


def _load_token_major(ref, row0, n, lead=()):
    cols = [ref[(*lead, pl.ds(row0 * TOKEN_ROWS + s, n, stride=TOKEN_ROWS), slice(None))] for s in range(TOKEN_ROWS)]
    return jnp.concatenate(cols, axis=1)


def _mod_kernel(s_ref, w_ref, b_ref, o_ref):
    s = s_ref[...]
    s = s * _sigmoid(s)
    o_ref[...] = jnp.dot(s.astype(BF16), w_ref[...].astype(BF16), preferred_element_type=F32) + b_ref[...]


def _mod(cc, mod_w, mod_b):
    d, n = mod_w.shape
    tn = 1024
    return pl.pallas_call(
        _mod_kernel,
        grid=(n // tn,),
        in_specs=[pl.BlockSpec((8, d), lambda j: (0, 0)),
                  pl.BlockSpec((d, tn), lambda j: (0, j)),
                  pl.BlockSpec((1, tn), lambda j: (0, j))],
        out_specs=pl.BlockSpec((8, tn), lambda j: (0, j)),
        out_shape=jax.ShapeDtypeStruct((8, n), F32),
        compiler_params=_cparams(("arbitrary",)),
        name="mod",
    )(cc, mod_w, mod_b.reshape(1, n))


def _norm_kernel(x_ref, shift_ref, scale_ref, g_ref, wg_ref, gb_ref, h_ref, gates_ref):
    x = x_ref[...]
    ms = jnp.mean(x * x, axis=-1, keepdims=True)
    h = (x * lax.rsqrt(ms + EPS) * g_ref[...]) * (1.0 + scale_ref[...]) + shift_ref[...]
    h1, h2, _ = _split3(h)
    h_ref[...] = h1
    w1, w2, _ = _split3(wg_ref[...])
    gates_ref[...] = (jnp.dot(h1, w1, preferred_element_type=F32)
                      + jnp.dot(h1, w2, preferred_element_type=F32)
                      + jnp.dot(h2, w1, preferred_element_type=F32)) + gb_ref[...]


def _norm(x2d, mod3, mod_row_of_tile, norm_g, w_gates, gate_b, tm):
    r, d = x2d.shape
    return pl.pallas_call(
        _norm_kernel,
        grid=(r // tm,),
        in_specs=[pl.BlockSpec((tm, d), lambda i: (i, 0)),
                  pl.BlockSpec((None, 1, d), lambda i: (mod_row_of_tile(i), 0, 0)),
                  pl.BlockSpec((None, 1, d), lambda i: (mod_row_of_tile(i), 0, 1)),
                  pl.BlockSpec((1, d), lambda i: (0, 0)),
                  pl.BlockSpec((d, LANES), lambda i: (0, 0)),
                  pl.BlockSpec((1, LANES), lambda i: (0, 0))],
        out_specs=[pl.BlockSpec((tm, d), lambda i: (i, 0)),
                   pl.BlockSpec((tm, LANES), lambda i: (i, 0))],
        out_shape=[jax.ShapeDtypeStruct((r, d), BF16),
                   jax.ShapeDtypeStruct((r, LANES), F32)],
        compiler_params=_cparams(("arbitrary",)),
        name="norm",
    )(x2d, mod3, mod3, norm_g, w_gates, gate_b)


INPROJ_TN = 1024
INPROJ_ROWS = 256


def _inproj_kernel(h_ref, wt_ref, z_ref, w_bf, *, tiles_per_seg):
    j = pl.program_id(0)
    i = pl.program_id(1)

    @pl.when(i == 0)
    def _():
        w_bf[...] = wt_ref[...].T.astype(BF16)

    seg = j // tiles_per_seg
    is_raw = seg < 2
    is_gelu = (seg == 3) | (seg == 4)
    c0 = 0.7978845608028654
    p0 = jnp.where(is_gelu, c0, 0.5).astype(F32)
    p1 = jnp.where(is_gelu, c0 * 0.044715, 0.0).astype(F32)
    q0 = jnp.where(is_gelu, 0.0, 1.0).astype(F32)
    q1 = jnp.where(is_gelu, 1.0, 0.0).astype(F32)
    w = w_bf[...]
    for c in range(h_ref.shape[0] // INPROJ_ROWS):
        rs = slice(c * INPROJ_ROWS, (c + 1) * INPROJ_ROWS)
        z = jnp.dot(h_ref[rs, :], w, preferred_element_type=F32)
        act = (0.5 * (q0 + q1 * z)) * (1.0 + jnp.tanh(z * (p0 + p1 * (z * z))))
        z_ref[rs, :] = jnp.where(is_raw, z, act).astype(BF16)


def _inproj(h, w_in_t, layer, n_aligned, shift, n_tiles, tm):
    r, d = h.shape
    tn = INPROJ_TN
    kern = functools.partial(_inproj_kernel, tiles_per_seg=d // tn)
    assert shift % 8 == 0
    row0 = lambda j: pl.multiple_of(j * tn + jnp.where(j >= n_aligned, shift, 0), 8)
    return pl.pallas_call(
        kern,
        grid=(n_tiles, r // tm),
        in_specs=[pl.BlockSpec((tm, d), lambda j, i: (i, 0)),
                  pl.BlockSpec((None, pl.Element(tn), pl.Element(d)), lambda j, i: (layer, row0(j), 0))],
        out_specs=pl.BlockSpec((tm, tn), lambda j, i: (i, j)),
        out_shape=jax.ShapeDtypeStruct((r, n_tiles * tn), BF16),
        scratch_shapes=[pltpu.VMEM((d, tn), BF16)],
        compiler_params=_cparams(("arbitrary", "arbitrary")),
        name="inproj",
    )(h, w_in_t)


def _conv_kernel(x_ref, w_ref, b_ref, o_ref, *, q_tiles):
    c = pl.program_id(1)
    x = x_ref[...].astype(F32)
    t = x.shape[0]
    rows = lax.broadcasted_iota(I32, x.shape, 0)
    w = w_ref[...]
    acc = x * w[CONV_W // 2:CONV_W // 2 + 1, :] + b_ref[...]
    for dlt in range(-(CONV_W // 2), CONV_W // 2 + 1):
        if dlt == 0:
            continue
        xs = pltpu.roll(x, shift=(-dlt) % t, axis=0)
        valid = (rows + dlt >= 0) & (rows + dlt < t)
        acc = acc + jnp.where(valid, xs, 0.0) * w[dlt + CONV_W // 2:dlt + CONV_W // 2 + 1, :]
    y = acc * _sigmoid(acc)
    scale = jnp.where(c < q_tiles, QK_DIM ** -0.5, 1.0).astype(F32)
    o_ref[...] = (y * scale).astype(BF16)


def _conv(z3, conv_w, conv_b):
    b, t, _ = z3.shape
    width = conv_w.shape[1]
    tc = 256
    kern = functools.partial(_conv_kernel, q_tiles=(width // 2) // tc)
    return pl.pallas_call(
        kern,
        grid=(b, width // tc),
        in_specs=[pl.BlockSpec((None, t, tc), lambda i, c: (i, 0, c)),
                  pl.BlockSpec((CONV_W, tc), lambda i, c: (0, c)),
                  pl.BlockSpec((1, tc), lambda i, c: (0, c))],
        out_specs=pl.BlockSpec((None, t, tc), lambda i, c: (i, 0, c)),
        out_shape=jax.ShapeDtypeStruct((b, t, width), BF16),
        compiler_params=_cparams(("arbitrary", "arbitrary")),
        name="conv",
    )(z3, conv_w, conv_b.reshape(1, width))


def _mlstm_kernel(qkc_f, qkl_f, vc_f, vl_f, gc_f, gl_f, qkc_b, qkl_b, vc_b, vl_b, gc_b, gl_b,
                  hf_ref, hb_ref, *state, n_ctx_chunks):
    c_scr, m_scr = state[:2 * M_HEADS], state[2 * M_HEADS:]
    s = pl.program_id(1)

    @pl.when(s == 0)
    def _():
        for ref in state:
            ref[...] = jnp.zeros_like(ref)

    is_ctx = s < n_ctx_chunks
    rows = lax.broadcasted_iota(I32, (CHUNK, CHUNK), 0)
    lanes = lax.broadcasted_iota(I32, (CHUNK, CHUNK), 1)
    ones_col = jnp.where(lanes == 0, 1.0, 0.0).astype(BF16)
    dn_nt = (((1,), (1,)), ((), ()))
    dn_tn = (((0,), (0,)), ((), ()))
    qk_w = M_HEADS * QK_DIM

    dirs = ((qkc_f, qkl_f, vc_f, vl_f, gc_f, gl_f, hf_ref), (qkc_b, qkl_b, vc_b, vl_b, gc_b, gl_b, hb_ref))
    per_dir = []
    for d, (qkc, qkl, vc, vl, gc, gl, out_ref) in enumerate(dirs):
        qk = jnp.where(is_ctx, qkc[...], qkl[...])
        v = jnp.where(is_ctx, vc[...], vl[...])
        g = jnp.where(is_ctx, gc[...], gl[...])
        mask = (lanes <= rows) if d == 0 else (lanes >= rows)
        tri = jnp.where(mask, 1.0, 0.0).astype(BF16)
        ls = jnp.minimum(g, 0.0) - jnp.log(1.0 + jnp.exp(-jnp.abs(g)))
        l1, l2, l3 = _split3(ls)
        bcol = (jnp.dot(tri, l1, preferred_element_type=F32) + jnp.dot(tri, l2, preferred_element_type=F32)
                + jnp.dot(tri, l3, preferred_element_type=F32))
        per_dir.append((d, qk, v, g, mask, bcol, bcol.T, g.T, out_ref))

    units = [(pd, h) for pd in per_dir for h in range(M_HEADS)]
    stage1 = []
    for (d, qk, v, g, mask, bcol, brow, grow, out_ref), h in units:
        col_i = 2 * M_HEADS * d + h
        col_f = col_i + M_HEADS
        idx = d * M_HEADS + h
        bc = bcol[:, col_f:col_f + 1]
        br = brow[col_f:col_f + 1, :]
        ir = grow[col_i:col_i + 1, :]
        ic = g[:, col_i:col_i + 1]
        m = m_scr[idx][0:1, 0:1]
        dlog = jnp.where(mask, bc - br + ir, -jnp.inf)
        inter = bc + m
        m_t = jnp.maximum(inter, jnp.max(dlog, axis=1, keepdims=True))
        w_inter = jnp.exp(inter - m_t)
        b_end = br[:, CHUNK - 1:CHUNK] if d == 0 else br[:, 0:1]
        m_new = jnp.maximum(b_end + m, jnp.max(b_end - br + ir, axis=1, keepdims=True))
        decay = jnp.exp(b_end + m - m_new)
        w_col = jnp.exp(b_end - bc + ic - m_new)
        q = qk[:, h * QK_DIM:(h + 1) * QK_DIM]
        k = qk[:, qk_w + h * QK_DIM:qk_w + (h + 1) * QK_DIM]
        sc = lax.dot_general(q, k, dn_nt, preferred_element_type=F32) * jnp.exp(dlog - m_t)
        lhs = jnp.concatenate([(q.astype(F32) * w_inter).astype(BF16), sc.astype(BF16)], axis=1)
        kw = (k.astype(F32) * w_col).astype(BF16)
        stage1.append((lhs, kw, m_t, m_new, decay))

    stage2 = []
    for ((d, qk, v, g, mask, bcol, brow, grow, out_ref), h), (lhs, kw, m_t, m_new, decay) in zip(units, stage1):
        idx = d * M_HEADS + h
        vaug = jnp.concatenate([v[:, h * V_DIM:(h + 1) * V_DIM], ones_col], axis=1)
        caug = c_scr[idx][...]
        num = jnp.dot(lhs, jnp.concatenate([caug.astype(BF16), vaug], axis=0), preferred_element_type=F32)
        upd = lax.dot_general(kw, vaug, dn_tn, preferred_element_type=F32)
        stage2.append((num, upd, caug))

    for ((d, qk, v, g, mask, bcol, brow, grow, out_ref), h), (lhs, kw, m_t, m_new, decay), (num, upd, caug) in zip(
            units, stage1, stage2):
        idx = d * M_HEADS + h
        den = num[:, V_DIM:V_DIM + 1]
        hout = num[:, :V_DIM] / jnp.maximum(jnp.abs(den), jnp.exp(-m_t))
        c_scr[idx][...] = decay * caug + upd
        m_scr[idx][...] = jnp.broadcast_to(m_new, m_scr[idx].shape)
        out_ref[:, h * V_DIM:(h + 1) * V_DIM] = hout.astype(out_ref.dtype)


def _mlstm(qk_ctx, qk_lat, z_ctx3, z_lat3, g_ctx3, g_lat3):
    b, tc, _ = qk_ctx.shape
    t = qk_lat.shape[1]
    nc, nl = tc // CHUNK, t // CHUNK
    vw = M_HEADS * V_DIM
    qkw = 2 * M_HEADS * QK_DIM

    f_ctx = lambda i, s: (i, jnp.minimum(s, nc - 1), 0)
    f_lat = lambda i, s: (i, jnp.clip(s - nc, 0, nl - 1), 0)
    b_ctx = lambda i, s: (i, jnp.maximum(nc - 1 - s, 0), 0)
    b_lat = lambda i, s: (i, jnp.clip(nc + nl - 1 - s, 0, nl - 1), 0)
    v_of = lambda f: (lambda i, s: (f(i, s)[0], f(i, s)[1], 1))

    def specs(fc, fl):
        return [pl.BlockSpec((None, CHUNK, qkw), fc), pl.BlockSpec((None, CHUNK, qkw), fl),
                pl.BlockSpec((None, CHUNK, vw), v_of(fc)), pl.BlockSpec((None, CHUNK, vw), v_of(fl)),
                pl.BlockSpec((None, CHUNK, LANES), fc), pl.BlockSpec((None, CHUNK, LANES), fl)]

    kern = functools.partial(_mlstm_kernel, n_ctx_chunks=nc)
    args = (qk_ctx, qk_lat, z_ctx3, z_lat3, g_ctx3, g_lat3)
    return pl.pallas_call(
        kern,
        grid=(b, nc + nl),
        in_specs=specs(f_ctx, f_lat) + specs(b_ctx, b_lat),
        out_specs=[pl.BlockSpec((None, CHUNK, vw), f_lat), pl.BlockSpec((None, CHUNK, vw), b_lat)],
        out_shape=[jax.ShapeDtypeStruct((b, t, vw), BF16), jax.ShapeDtypeStruct((b, t, vw), BF16)],
        scratch_shapes=([pltpu.VMEM((QK_DIM, V_DIM + LANES), F32)] * (2 * M_HEADS)
                        + [pltpu.VMEM((8, LANES), F32)] * (2 * M_HEADS)),
        compiler_params=_cparams(("arbitrary", "arbitrary")),
        name="mlstm",
    )(*args, *args)


MERGE_ROWS = 256


def _merge_kernel(hf_ref, hb_ref, o_ref, u_ref, sv_ref, ga_ref, gb_ref, gm_ref, gs_ref, sw_ref, sb_ref,
                  pa_ref, pb_ref, out_ref):
    tm, d = out_ref.shape
    gw = d // SGU_GROUPS
    for grp in range(tm // MERGE_ROWS):
        r0 = grp * MERGE_ROWS
        rs = slice(r0, r0 + MERGE_ROWS)
        a_parts = []
        for h in range(M_HEADS):
            sl = slice(h * V_DIM, (h + 1) * V_DIM)
            hs = hf_ref[rs, sl].astype(F32) + hb_ref[rs, sl].astype(F32)
            ms = jnp.mean(hs * hs, axis=-1, keepdims=True)
            y = hs * lax.rsqrt(ms + EPS) * gm_ref[:, sl]
            a_parts.append((y * o_ref[rs, sl].astype(F32)).astype(BF16))
        a = jnp.concatenate(a_parts, axis=1)
        sv = sv_ref[rs, :].astype(F32)
        ms = jnp.mean(sv * sv, axis=-1, keepdims=True)
        svn = (sv * lax.rsqrt(ms + EPS) * gs_ref[...]).astype(BF16)
        b_rows = []
        for c in range(MERGE_ROWS // CHUNK):
            cr = slice(c * CHUNK, (c + 1) * CHUNK)
            b_parts = []
            for gi in range(SGU_GROUPS):
                cs = slice(gi * gw, (gi + 1) * gw)
                zz = jnp.dot(sw_ref[gi], svn[cr, cs], preferred_element_type=F32) + sb_ref[:, gi:gi + 1]
                b_parts.append((u_ref[r0 + c * CHUNK:r0 + (c + 1) * CHUNK, cs].astype(F32) * zz).astype(BF16))
            b_rows.append(jnp.concatenate(b_parts, axis=1))
        b = jnp.concatenate(b_rows, axis=0)
        ya = jnp.dot(a, pa_ref[...], preferred_element_type=F32)
        yb = jnp.dot(b, pb_ref[...], preferred_element_type=F32)
        out_ref[rs, :] = (ga_ref[rs, :].astype(F32) * ya + gb_ref[rs, :].astype(F32) * yb).astype(BF16)


def _merge(hf, hb, z, gm, gs, sgu_w, sgu_bt, pa, pb):
    r, d = hf.shape
    tm = 512
    row = lambda k: (lambda i: (i, k))
    const = lambda shape: pl.BlockSpec(shape, lambda i: (0,) * len(shape), pipeline_mode=pl.Buffered(1))
    return pl.pallas_call(
        _merge_kernel,
        grid=(r // tm,),
        in_specs=[pl.BlockSpec((tm, d), row(0)), pl.BlockSpec((tm, d), row(0)),
                  pl.BlockSpec((tm, d), row(2)), pl.BlockSpec((tm, d), row(3)), pl.BlockSpec((tm, d), row(4)),
                  pl.BlockSpec((tm, d), row(5)), pl.BlockSpec((tm, d), row(6)),
                  const((1, d)), const((1, d)), const(sgu_w.shape), const(sgu_bt.shape),
                  const((d, d)), const((d, d))],
        out_specs=pl.BlockSpec((tm, d), row(0)),
        out_shape=jax.ShapeDtypeStruct((r, d), BF16),
        compiler_params=_cparams(("arbitrary",)),
        name="merge",
    )(hf, hb, z, z, z, z, z, gm, gs, sgu_w, sgu_bt, pa, pb)


def _outproj_kernel(m_ref, w_ref, x_ref, gate_ref, shift_ref, scale_ref, g_ref, rw_ref, rb_ref,
                    x1_ref, h2_ref, lg_ref):
    r1, r2, _ = _split3(rw_ref[...])
    dn_nt = (((1,), (1,)), ((), ()))
    for grp in range(x_ref.shape[0] // MERGE_ROWS):
        r0 = grp * MERGE_ROWS
        rs = slice(r0, r0 + MERGE_ROWS)
        out = jnp.dot(m_ref[rs, :], w_ref[...], preferred_element_type=F32)
        x1 = x_ref[rs, :] + gate_ref[...] * out
        x1_ref[rs, :] = x1
        ms = jnp.mean(x1 * x1, axis=-1, keepdims=True)
        h2 = (x1 * lax.rsqrt(ms + EPS) * g_ref[...]) * (1.0 + scale_ref[...]) + shift_ref[...]
        hi, lo, _ = _split3(h2)
        lg_ref[:, rs] = (lax.dot_general(r1, hi, dn_nt, preferred_element_type=F32)
                         + lax.dot_general(r1, lo, dn_nt, preferred_element_type=F32)
                         + lax.dot_general(r2, hi, dn_nt, preferred_element_type=F32)) + rb_ref[...]
        _store_token_major(h2_ref, r0, h2)


def _outproj(merged, w_out, x2d, mod3, mod_row_of_tile, norm_g, rw_t, rb_col, tm):
    r, d = x2d.shape
    ne = rw_t.shape[0]
    modspec = lambda k: pl.BlockSpec((None, 1, d), lambda i: (mod_row_of_tile(i), 0, k))
    return pl.pallas_call(
        _outproj_kernel,
        grid=(r // tm,),
        in_specs=[pl.BlockSpec((tm, d), lambda i: (i, 0)),
                  pl.BlockSpec((d, d), lambda i: (0, 0), pipeline_mode=pl.Buffered(1)),
                  pl.BlockSpec((tm, d), lambda i: (i, 0)),
                  modspec(2), modspec(3), modspec(4),
                  pl.BlockSpec((1, d), lambda i: (0, 0)),
                  pl.BlockSpec((ne, d), lambda i: (0, 0)),
                  pl.BlockSpec((ne, 1), lambda i: (0, 0))],
        out_specs=[pl.BlockSpec((tm, d), lambda i: (i, 0)),
                   pl.BlockSpec((tm * TOKEN_ROWS, LANES), lambda i: (i, 0)),
                   pl.BlockSpec((ne, tm), lambda i: (0, i))],
        out_shape=[jax.ShapeDtypeStruct((r, d), F32),
                   jax.ShapeDtypeStruct((r * TOKEN_ROWS, LANES), F32),
                   jax.ShapeDtypeStruct((ne, r), F32)],
        compiler_params=_cparams(("arbitrary",)),
        name="outproj",
    )(merged, w_out, x2d, mod3, mod3, mod3, norm_g, rw_t, rb_col)


def _route_kernel(lg_ref, eid_ref, gate_ref, rank_ref, cnt_ref, carry_scr):
    i = pl.program_id(0)

    @pl.when(i == 0)
    def _():
        carry_scr[...] = jnp.zeros_like(carry_scr)

    l = lg_ref[...]
    ne, tm = l.shape
    e_iota = lax.broadcasted_iota(I32, (ne, tm), 0)
    vals, onehots = [], []
    for k in range(TOP_K):
        mx = jnp.max(l, axis=0, keepdims=True)
        idx = jnp.min(jnp.where(l == mx, e_iota, ne), axis=0, keepdims=True)
        oh = e_iota == idx
        l = jnp.where(oh, -jnp.inf, l)
        vals.append(mx)
        onehots.append(oh)
        eid_ref[k:k + 1, :] = idx
    ex = [jnp.exp(vk - vals[0]) for vk in vals]
    tot = ex[0] + ex[1] + ex[2] + ex[3]
    for k in range(TOP_K):
        gate_ref[k:k + 1, :] = ex[k] / tot

    oh_all = jnp.zeros((ne, tm), F32)
    for oh in onehots:
        oh_all = oh_all + jnp.where(oh, 1.0, 0.0)
    oh_all = oh_all.astype(BF16)
    r_i = lax.broadcasted_iota(I32, (LANES, LANES), 0)
    c_i = lax.broadcasted_iota(I32, (LANES, LANES), 1)
    tri_excl = jnp.where(r_i < c_i, 1.0, 0.0).astype(BF16)
    ones_m = jnp.ones((LANES, LANES), BF16)
    carry = carry_scr[...]
    for blk in range(tm // LANES):
        sl = slice(blk * LANES, (blk + 1) * LANES)
        ohb = oh_all[:, sl]
        cum = jnp.dot(ohb, tri_excl, preferred_element_type=F32) + carry
        for k in range(TOP_K):
            rk = jnp.sum(jnp.where(onehots[k][:, sl], cum, 0.0), axis=0, keepdims=True)
            rank_ref[k:k + 1, sl] = rk.astype(I32)
        carry = carry + jnp.dot(ohb, ones_m, preferred_element_type=F32)
    carry_scr[...] = carry
    cnt_ref[...] = carry


def _route(logits_t):
    ne, r = logits_t.shape
    tm = 1024
    return pl.pallas_call(
        _route_kernel,
        grid=(r // tm,),
        in_specs=[pl.BlockSpec((ne, tm), lambda i: (0, i))],
        out_specs=[pl.BlockSpec((TOP_K, tm), lambda i: (0, i)),
                   pl.BlockSpec((TOP_K, tm), lambda i: (0, i)),
                   pl.BlockSpec((TOP_K, tm), lambda i: (0, i)),
                   pl.BlockSpec((ne, LANES), lambda i: (0, 0))],
        out_shape=[jax.ShapeDtypeStruct((TOP_K, r), I32),
                   jax.ShapeDtypeStruct((TOP_K, r), F32),
                   jax.ShapeDtypeStruct((TOP_K, r), I32),
                   jax.ShapeDtypeStruct((ne, LANES), F32)],
        scratch_shapes=[pltpu.VMEM((ne, LANES), F32)],
        compiler_params=_cparams(("arbitrary",)),
        name="route",
    )(logits_t)


def _slot_kernel(eid_ref, rank_ref, cnt_ref, slot_ref):
    ne = cnt_ref.shape[0]
    nblk = jnp.floor((cnt_ref[...] + (MOE_TILE - 1.0)) * (1.0 / MOE_TILE))
    r_i = lax.broadcasted_iota(I32, (ne, ne), 0)
    c_i = lax.broadcasted_iota(I32, (ne, ne), 1)
    tri = jnp.where(c_i < r_i, 1.0, 0.0).astype(BF16)
    start = jnp.dot(tri, nblk.astype(BF16), preferred_element_type=F32) * MOE_TILE
    eid = eid_ref[...]
    acc = rank_ref[...]
    for e in range(ne):
        acc = acc + jnp.where(eid == e, start[e:e + 1, 0:1].astype(I32), 0)
    slot_ref[...] = acc


def _slots(eid, rank, cnt):
    return pl.pallas_call(
        _slot_kernel,
        out_shape=jax.ShapeDtypeStruct(eid.shape, I32),
        compiler_params=pltpu.CompilerParams(vmem_limit_bytes=V7X_VMEM_LIMIT_BYTES),
        name="slots",
    )(eid, rank, cnt)


def _dispatch_copy(slot_ref, h_ref, xs_ref, sem, k, t, base, n_tok):
    d0 = pl.multiple_of(slot_ref[k * n_tok + base + t] * TOKEN_ROWS, TOKEN_ROWS)
    s0 = pl.multiple_of(t * TOKEN_ROWS, TOKEN_ROWS)
    return pltpu.make_async_copy(h_ref.at[pl.ds(s0, TOKEN_ROWS), :], xs_ref.at[pl.ds(d0, TOKEN_ROWS), :], sem)


def _dispatch_kernel(slot_ref, h_ref, init_ref, xs_ref, sem, *, tm, n_tok):
    base = pl.program_id(0) * tm

    def start(t, carry):
        for k in range(TOP_K):
            _dispatch_copy(slot_ref, h_ref, xs_ref, sem, k, t, base, n_tok).start()
        return carry

    def wait(t, carry):
        for k in range(TOP_K):
            _dispatch_copy(slot_ref, h_ref, xs_ref, sem, k, t, base, n_tok).wait()
        return carry

    lax.fori_loop(0, tm, start, 0, unroll=4)
    lax.fori_loop(0, tm, wait, 0, unroll=4)


def _dispatch(h2p, slot, p_rows):
    n_tok = h2p.shape[0] // TOKEN_ROWS
    tm = 512
    init = jnp.zeros((p_rows * TOKEN_ROWS, LANES), h2p.dtype)
    grid_spec = pltpu.PrefetchScalarGridSpec(
        num_scalar_prefetch=1,
        grid=(n_tok // tm,),
        in_specs=[pl.BlockSpec((tm * TOKEN_ROWS, LANES), lambda i, slot: (i, 0)),
                  pl.BlockSpec(memory_space=pl.ANY)],
        out_specs=pl.BlockSpec(memory_space=pl.ANY),
        scratch_shapes=[pltpu.SemaphoreType.DMA(())],
    )
    return pl.pallas_call(
        functools.partial(_dispatch_kernel, tm=tm, n_tok=n_tok),
        grid_spec=grid_spec,
        out_shape=jax.ShapeDtypeStruct(init.shape, init.dtype),
        input_output_aliases={2: 0},
        compiler_params=_cparams(("arbitrary",)),
        name="dispatch",
    )(slot, h2p, init)


def _weight_copies(w_hbm, wbuf, sems, e, col_starts, slot):
    tn = wbuf.shape[-1]
    aligned = lambda c: c if isinstance(c, int) else pl.multiple_of(c, LANES)
    return [pltpu.make_async_copy(w_hbm.at[e, :, pl.ds(aligned(c), tn)], wbuf.at[slot, i], sems.at[slot])
            for i, c in enumerate(col_starts)]


def _advance_weight_pipeline(w_hbm, wbuf, sems, sched, j, nj, it, col_starts_of):
    e_ref, run_ref, last_ref, nxt_ref, nruns_ref = sched
    slot = (j * nruns_ref[0] + run_ref[it]) % 2

    @pl.when((j == 0) & (it == 0))
    def _():
        for cp in _weight_copies(w_hbm, wbuf, sems, e_ref[it], col_starts_of(j), slot):
            cp.start()

    for cp in _weight_copies(w_hbm, wbuf, sems, e_ref[it], col_starts_of(j), slot):
        cp.wait()
    is_last = last_ref[it] == 1

    @pl.when(jnp.logical_not(is_last))
    def _():
        for cp in _weight_copies(w_hbm, wbuf, sems, nxt_ref[it], col_starts_of(j), 1 - slot):
            cp.start()

    @pl.when(is_last & (j + 1 < nj))
    def _():
        for cp in _weight_copies(w_hbm, wbuf, sems, nxt_ref[it], col_starts_of(j + 1), 1 - slot):
            cp.start()

    return slot


def _moe_up_kernel(blk_ref, oblk_ref, e_ref, nv_ref, new_ref, run_ref, last_ref, nxt_ref, nruns_ref,
                   xs_ref, w_hbm, bg_ref, bl_ref, act_ref, wbuf, sems, wg_bf, wl_bf):
    j = pl.program_id(0)
    nj = pl.num_programs(0)
    it = pl.program_id(1)
    tn = wg_bf.shape[1]

    @pl.when(new_ref[it] == 1)
    def _():
        slot = _advance_weight_pipeline(w_hbm, wbuf, sems, (e_ref, run_ref, last_ref, nxt_ref, nruns_ref), j, nj, it,
                                        lambda jj: (jj * tn, (nj + jj) * tn))
        wg_bf[...] = wbuf[slot, 0].astype(BF16)
        wl_bf[...] = wbuf[slot, 1].astype(BF16)

    def rows_chunk(r0, n):
        x = _load_token_major(xs_ref, r0, n).astype(BF16)
        hg = jnp.dot(x, wg_bf[...], preferred_element_type=F32) + bg_ref[...]
        hl = jnp.dot(x, wl_bf[...], preferred_element_type=F32) + bl_ref[...]
        glu = jnp.minimum(hg, SWIGLU_LIMIT)
        lin = jnp.clip(hl, -SWIGLU_LIMIT, SWIGLU_LIMIT)
        act_ref[r0:r0 + n, :] = ((0.5 * glu) * (1.0 + jnp.tanh((0.5 * SWIGLU_ALPHA) * glu)) * (lin + 1.0)).astype(BF16)

    def rows_zero(r0, n):
        act_ref[r0:r0 + n, :] = jnp.zeros((n, act_ref.shape[1]), BF16)

    _moe_row_variants(nv_ref[it], rows_chunk, rows_zero)


def _moe_row_variants(nv, rows_chunk, rows_zero):
    for units in range(MOE_TILE // MOE_STEP + 1):
        rows = units * MOE_STEP

        @pl.when((nv > rows - MOE_STEP) & (nv <= rows) if units else nv <= 0)
        def _():
            r0 = 0
            while r0 < rows:
                n = min(MOE_HALF, rows - r0)
                rows_chunk(r0, n)
                r0 += n
            if rows < MOE_TILE:
                rows_zero(rows, MOE_TILE - rows)


def _moe_down_kernel(blk_ref, oblk_ref, e_ref, nv_ref, new_ref, run_ref, last_ref, nxt_ref, nruns_ref,
                     a_ref, w_hbm, b_ref, y_ref, wbuf, sems, w_bf):
    it = pl.program_id(0)

    @pl.when(new_ref[it] == 1)
    def _():
        slot = _advance_weight_pipeline(w_hbm, wbuf, sems, (e_ref, run_ref, last_ref, nxt_ref, nruns_ref), 0, 1, it,
                                        lambda jj: (0,))
        w_bf[...] = wbuf[slot, 0].astype(BF16)

    def rows_chunk(r0, n):
        y = jnp.dot(a_ref[r0:r0 + n, :], w_bf[...], preferred_element_type=F32) + b_ref[...]
        _store_token_major(y_ref, r0, y)

    def rows_zero(r0, n):
        y_ref[r0 * TOKEN_ROWS:(r0 + n) * TOKEN_ROWS, :] = jnp.zeros((n * TOKEN_ROWS, LANES), F32)

    _moe_row_variants(nv_ref[it], rows_chunk, rows_zero)


def _moe_up(items, xs, w1, b1):
    n_items = items[0].shape[0]
    ne, d, two_f = w1.shape
    f = two_f // 2
    nj = f // MOE_TN
    p = xs.shape[0] // TOKEN_ROWS
    b1r = b1.reshape(ne, 1, two_f)
    grid_spec = pltpu.PrefetchScalarGridSpec(
        num_scalar_prefetch=len(items),
        grid=(nj, n_items),
        in_specs=[pl.BlockSpec((MOE_TILE * TOKEN_ROWS, LANES), lambda j, it, blk, *_: (blk[it], 0)),
                  pl.BlockSpec(memory_space=pl.ANY),
                  pl.BlockSpec((None, 1, MOE_TN), lambda j, it, blk, oblk, e, *_: (e[it], 0, j)),
                  pl.BlockSpec((None, 1, MOE_TN), lambda j, it, blk, oblk, e, *_: (e[it], 0, nj + j))],
        out_specs=pl.BlockSpec((MOE_TILE, MOE_TN), lambda j, it, blk, oblk, *_: (oblk[it], j)),
        scratch_shapes=[pltpu.VMEM((2, 2, d, MOE_TN), F32), pltpu.SemaphoreType.DMA((2,)),
                        pltpu.VMEM((d, MOE_TN), BF16), pltpu.VMEM((d, MOE_TN), BF16)],
    )
    return pl.pallas_call(
        _moe_up_kernel,
        grid_spec=grid_spec,
        out_shape=jax.ShapeDtypeStruct((p, f), BF16),
        compiler_params=_cparams(("arbitrary", "arbitrary")),
        name="moe_up",
    )(*items, xs, w1, b1r, b1r)


def _moe_down(items, act, w2, b2):
    n_items = items[0].shape[0]
    ne, f, d = w2.shape
    p = act.shape[0]
    b2r = b2.reshape(ne, 1, d)
    grid_spec = pltpu.PrefetchScalarGridSpec(
        num_scalar_prefetch=len(items),
        grid=(n_items,),
        in_specs=[pl.BlockSpec((MOE_TILE, f), lambda it, blk, *_: (blk[it], 0)),
                  pl.BlockSpec(memory_space=pl.ANY),
                  pl.BlockSpec((None, 1, d), lambda it, blk, oblk, e, *_: (e[it], 0, 0))],
        out_specs=pl.BlockSpec((MOE_TILE * TOKEN_ROWS, LANES), lambda it, blk, oblk, *_: (oblk[it], 0)),
        scratch_shapes=[pltpu.VMEM((2, 1, f, d), F32), pltpu.SemaphoreType.DMA((2,)), pltpu.VMEM((f, d), BF16)],
    )
    return pl.pallas_call(
        _moe_down_kernel,
        grid_spec=grid_spec,
        out_shape=jax.ShapeDtypeStruct((p * TOKEN_ROWS, LANES), F32),
        compiler_params=_cparams(("arbitrary",)),
        name="moe_down",
    )(*items, act, w2, b2r)


def _combine_copy(slot_ref, y_ref, buf, sems, k, t, tile, par, tm, n_tok):
    s0 = pl.multiple_of(slot_ref[k * n_tok + tile * tm + t] * TOKEN_ROWS, TOKEN_ROWS)
    d0 = pl.multiple_of(t * TOKEN_ROWS, TOKEN_ROWS)
    return pltpu.make_async_copy(y_ref.at[pl.ds(s0, TOKEN_ROWS), :], buf.at[par, k, pl.ds(d0, TOKEN_ROWS), :],
                                 sems.at[par])


def _final_kernel(slot_ref, y_ref, gates_ref, x1_ref, gate_ref, g_ref, o_ref, buf, sems, *, n_tok):
    tm, d = x1_ref.shape
    i = pl.program_id(0)
    par = i % 2

    def gather(tile, par_, start):
        def body(t, carry):
            for k in range(TOP_K):
                cp = _combine_copy(slot_ref, y_ref, buf, sems, k, t, tile, par_, tm, n_tok)
                cp.start() if start else cp.wait()
            return carry
        lax.fori_loop(0, tm, body, 0, unroll=4)

    @pl.when(i == 0)
    def _():
        gather(0, 0, True)

    @pl.when(i + 1 < pl.num_programs(0))
    def _():
        gather(i + 1, 1 - par, True)

    gather(i, par, False)

    acc = jnp.zeros((tm, d), F32)
    for k in range(TOP_K):
        acc = acc + gates_ref[:, k:k + 1] * _load_token_major(buf, 0, tm, lead=(par, k))
    xf = x1_ref[...] + gate_ref[...] * acc
    ms = jnp.mean(xf * xf, axis=-1, keepdims=True)
    o_ref[...] = xf * lax.rsqrt(ms + EPS) * g_ref[...]


def _final(slot, y, gates_tk, x1, mod3, mod_row_of_tile, final_g, tm):
    r, d = x1.shape
    grid_spec = pltpu.PrefetchScalarGridSpec(
        num_scalar_prefetch=1,
        grid=(r // tm,),
        in_specs=[pl.BlockSpec(memory_space=pl.ANY),
                  pl.BlockSpec((tm, TOP_K), lambda i, slot: (i, 0)),
                  pl.BlockSpec((tm, d), lambda i, slot: (i, 0)),
                  pl.BlockSpec((None, 1, d), lambda i, slot: (mod_row_of_tile(i), 0, 5)),
                  pl.BlockSpec((1, d), lambda i, slot: (0, 0))],
        out_specs=pl.BlockSpec((tm, d), lambda i, slot: (i, 0)),
        scratch_shapes=[pltpu.VMEM((2, TOP_K, tm * TOKEN_ROWS, LANES), F32), pltpu.SemaphoreType.DMA((2,))],
    )
    return pl.pallas_call(
        functools.partial(_final_kernel, n_tok=r),
        grid_spec=grid_spec,
        out_shape=jax.ShapeDtypeStruct((r, d), F32),
        compiler_params=_cparams(("arbitrary",)),
        name="final",
    )(slot, y, gates_tk, x1, mod3, final_g)


def _moe_schedule(counts, n_items):
    ne = counts.shape[0]
    padded = (counts + MOE_TILE - 1) // MOE_TILE * MOE_TILE
    pad_end = jnp.cumsum(padded)
    start_pad = pad_end - padded
    n_real = pad_end[-1] // MOE_TILE
    b = jnp.arange(n_items, dtype=I32)
    blk = jnp.minimum(b, jnp.maximum(n_real - 1, 0))
    blk_start = blk * MOE_TILE
    e_of = jnp.minimum(jnp.sum((pad_end[None, :] <= blk_start[:, None]).astype(I32), axis=1), ne - 1)
    nv = jnp.clip(counts[e_of] - (blk_start - start_pad[e_of]), 0, MOE_TILE)
    nv = jnp.where(b < n_real, nv, 0)
    e_of = e_of.astype(I32)
    prev_e = jnp.concatenate([jnp.full((1,), -1, I32), e_of[:-1]])
    new = ((e_of != prev_e) | (b == 0)).astype(I32)
    run = jnp.cumsum(new) - 1
    n_runs = run[-1] + 1
    last = (run == n_runs - 1).astype(I32)
    first_pos = jnp.where(new == 1, b, n_items)
    nxt_pos = jnp.concatenate([lax.cummin(first_pos, reverse=True)[1:], jnp.full((1,), n_items, I32)])
    nxt = jnp.where(nxt_pos < n_items, e_of[jnp.minimum(nxt_pos, n_items - 1)], e_of[0])
    return (blk.astype(I32), b, e_of, nv.astype(I32), new, run.astype(I32), last, nxt.astype(I32),
            n_runs.reshape(1).astype(I32))


def kernel(x, c, ctx, c_ctx, mod_w, mod_b, norm1_g, norm2_g, w_in, gate_b, conv_w, conv_b, mlstm_norm_g,
           sgu_norm_g, sgu_w, sgu_b, proj_a, proj_b, w_out, router_w, router_b, exp_w1, exp_b1, exp_w2,
           exp_b2, final_g):
    bsz, t, d = x.shape
    tc = ctx.shape[1]
    depth = mod_w.shape[0]
    assert depth == 1, "single-layer block"
    assert d == TOKEN_ROWS * LANES, "token-major rows hold TOKEN_ROWS*128 features"
    l = 0
    r = bsz * t
    qkw = 2 * M_HEADS * QK_DIM
    vw = M_HEADS * V_DIM
    off_gates = qkw + vw
    n_gates = 4 * M_HEADS
    off_o = off_gates + n_gates

    cc = jnp.zeros((8, d), F32).at[:bsz].set(c).at[bsz].set(c_ctx)
    mod3 = _mod(cc, mod_w[l], mod_b[l]).reshape(8, 1, N_MOD * d)

    in_cols = w_in.shape[2]
    assert off_gates % INPROJ_TN == 0 and (in_cols - off_o) % INPROJ_TN == 0
    n_aligned = off_gates // INPROJ_TN
    n_tiles = n_aligned + (in_cols - off_o) // INPROJ_TN
    w_in_t = jnp.swapaxes(w_in, 1, 2)
    w_gates = jnp.pad(w_in_t[l, off_gates:off_o, :].T, ((0, 0), (0, LANES - n_gates)))
    gb = jnp.pad(gate_b[l].reshape(1, n_gates), ((0, 0), (0, LANES - n_gates)))
    g1 = norm1_g[l].reshape(1, d)

    tm_in = 512
    h_lat, g_lat = _norm(x.reshape(r, d), mod3, lambda i: i // (t // tm_in), g1, w_gates, gb, tm_in)
    h_ctx, g_ctx = _norm(ctx.reshape(bsz * tc, d), mod3, lambda i: bsz, g1, w_gates, gb, tc)
    z_lat = _inproj(h_lat, w_in_t, l, n_aligned, n_gates, n_tiles, 1024)
    z_ctx = _inproj(h_ctx, w_in_t, l, n_aligned, n_gates, n_aligned, bsz * tc)

    z_lat3 = z_lat.reshape(bsz, t, z_lat.shape[1])
    z_ctx3 = z_ctx.reshape(bsz, tc, z_ctx.shape[1])
    qk_lat = _conv(z_lat3, conv_w[l], conv_b[l])
    qk_ctx = _conv(z_ctx3, conv_w[l], conv_b[l])

    hf, hb = _mlstm(qk_ctx, qk_lat, z_ctx3, z_lat3, g_ctx.reshape(bsz, tc, LANES), g_lat.reshape(bsz, t, LANES))

    merged = _merge(hf.reshape(r, vw), hb.reshape(r, vw), z_lat,
                    mlstm_norm_g[l].reshape(1, vw), sgu_norm_g[l].reshape(1, -1),
                    sgu_w[l].astype(BF16), sgu_b[l].T, proj_a[l].astype(BF16), proj_b[l].astype(BF16))

    tm_out = 512
    x1, h2p, logits_t = _outproj(merged, w_out[l].astype(BF16), x.reshape(r, d), mod3,
                                 lambda i: i // (t // tm_out), norm2_g[l].reshape(1, d),
                                 router_w[l].T, router_b[l].reshape(-1, 1), tm_out)
    tm_fin = 256

    eid, gates, rank, cnt = _route(logits_t)
    counts = cnt[:, 0].astype(I32)
    n_items = (r * TOP_K) // MOE_TILE + N_EXPERTS
    items = _moe_schedule(counts, n_items)
    slot = _slots(eid, rank, cnt).reshape(-1)
    xs = _dispatch(h2p, slot, n_items * MOE_TILE)
    act = _moe_up(items, xs, exp_w1[l], exp_b1[l])
    y = _moe_down(items, act, exp_w2[l], exp_b2[l])
    out = _final(slot, y, gates.T, x1, mod3, lambda i: i // (t // tm_fin), final_g.reshape(1, d), tm_fin)
    return out.reshape(bsz, t, d)
```

```python
import functools

import jax
import jax.numpy as jnp
from jax import lax
from jax.experimental import pallas as pl
from jax.experimental.pallas import tpu as pltpu

F32 = jnp.float32
BF16 = jnp.bfloat16
I32 = jnp.int32
U32 = jnp.uint32

EPS = 1e-6
M_HEADS = 8
QK_DIM = 128
V_DIM = 256
CONV_W = 5
CHUNK = 128
SGU_GROUPS = 8
N_EXPERTS = 32
TOP_K = 4
SWIGLU_ALPHA = 1.702
SWIGLU_LIMIT = 7.0
N_MOD = 6

V7X_VMEM_LIMIT_BYTES = 56 * 1024 * 1024
LANES = 128
MOE_TILE = 512
MOE_HALF = 256
MOE_STEP = 128
MOE_TN = 1024


def _cparams(sem):
    return pltpu.CompilerParams(dimension_semantics=sem, vmem_limit_bytes=V7X_VMEM_LIMIT_BYTES)


def _sigmoid(x):
    return 1.0 / (1.0 + jnp.exp(-x))


def _gelu_tanh(x):
    return 0.5 * x * (1.0 + jnp.tanh(0.7978845608028654 * (x + 0.044715 * (x * x * x))))


def _split3(a):
    a1 = a.astype(BF16)
    r1 = a - a1.astype(F32)
    a2 = r1.astype(BF16)
    a3 = (r1 - a2.astype(F32)).astype(BF16)
    return a1, a2, a3


TOKEN_ROWS = 16


def _store_token_major(ref, row0, vals):
    n = vals.shape[0]
    for s in range(TOKEN_ROWS):
        ref[pl.ds(row0 * TOKEN_ROWS + s, n, stride=TOKEN_ROWS), :] = vals[:, s * LANES:(s + 1) * LANES]


def _load_token_major(ref, row0, n, lead=()):
    cols = [ref[(*lead, pl.ds(row0 * TOKEN_ROWS + s, n, stride=TOKEN_ROWS), slice(None))] for s in range(TOKEN_ROWS)]
    return jnp.concatenate(cols, axis=1)


def _mod_kernel(s_ref, w_ref, b_ref, o_ref):
    s = s_ref[...]
    s = s * _sigmoid(s)
    o_ref[...] = jnp.dot(s.astype(BF16), w_ref[...].astype(BF16), preferred_element_type=F32) + b_ref[...]


def _mod(cc, mod_w, mod_b):
    d, n = mod_w.shape
    tn = 1024
    return pl.pallas_call(
        _mod_kernel,
        grid=(n // tn,),
        in_specs=[pl.BlockSpec((8, d), lambda j: (0, 0)),
                  pl.BlockSpec((d, tn), lambda j: (0, j)),
                  pl.BlockSpec((1, tn), lambda j: (0, j))],
        out_specs=pl.BlockSpec((8, tn), lambda j: (0, j)),
        out_shape=jax.ShapeDtypeStruct((8, n), F32),
        compiler_params=_cparams(("arbitrary",)),
        name="mod",
    )(cc, mod_w, mod_b.reshape(1, n))


def _norm_kernel(x_ref, shift_ref, scale_ref, g_ref, wg_ref, gb_ref, h_ref, gates_ref):
    x = x_ref[...]
    ms = jnp.mean(x * x, axis=-1, keepdims=True)
    h = (x * lax.rsqrt(ms + EPS) * g_ref[...]) * (1.0 + scale_ref[...]) + shift_ref[...]
    h1, h2, _ = _split3(h)
    h_ref[...] = h1
    w1, w2, _ = _split3(wg_ref[...])
    gates_ref[...] = (jnp.dot(h1, w1, preferred_element_type=F32)
                      + jnp.dot(h1, w2, preferred_element_type=F32)
                      + jnp.dot(h2, w1, preferred_element_type=F32)) + gb_ref[...]


def _norm(x2d, mod3, mod_row_of_tile, norm_g, w_gates, gate_b, tm):
    r, d = x2d.shape
    return pl.pallas_call(
        _norm_kernel,
        grid=(r // tm,),
        in_specs=[pl.BlockSpec((tm, d), lambda i: (i, 0)),
                  pl.BlockSpec((None, 1, d), lambda i: (mod_row_of_tile(i), 0, 0)),
                  pl.BlockSpec((None, 1, d), lambda i: (mod_row_of_tile(i), 0, 1)),
                  pl.BlockSpec((1, d), lambda i: (0, 0)),
                  pl.BlockSpec((d, LANES), lambda i: (0, 0)),
                  pl.BlockSpec((1, LANES), lambda i: (0, 0))],
        out_specs=[pl.BlockSpec((tm, d), lambda i: (i, 0)),
                   pl.BlockSpec((tm, LANES), lambda i: (i, 0))],
        out_shape=[jax.ShapeDtypeStruct((r, d), BF16),
                   jax.ShapeDtypeStruct((r, LANES), F32)],
        compiler_params=_cparams(("arbitrary",)),
        name="norm",
    )(x2d, mod3, mod3, norm_g, w_gates, gate_b)


INPROJ_TN = 1024
INPROJ_ROWS = 256


def _inproj_kernel(h_ref, wt_ref, z_ref, w_bf, *, tiles_per_seg):
    j = pl.program_id(0)
    i = pl.program_id(1)

    @pl.when(i == 0)
    def _():
        w_bf[...] = wt_ref[...].T.astype(BF16)

    seg = j // tiles_per_seg
    is_raw = seg < 2
    is_gelu = (seg == 3) | (seg == 4)
    c0 = 0.7978845608028654
    p0 = jnp.where(is_gelu, c0, 0.5).astype(F32)
    p1 = jnp.where(is_gelu, c0 * 0.044715, 0.0).astype(F32)
    q0 = jnp.where(is_gelu, 0.0, 1.0).astype(F32)
    q1 = jnp.where(is_gelu, 1.0, 0.0).astype(F32)
    w = w_bf[...]
    for c in range(h_ref.shape[0] // INPROJ_ROWS):
        rs = slice(c * INPROJ_ROWS, (c + 1) * INPROJ_ROWS)
        z = jnp.dot(h_ref[rs, :], w, preferred_element_type=F32)
        act = (0.5 * (q0 + q1 * z)) * (1.0 + jnp.tanh(z * (p0 + p1 * (z * z))))
        z_ref[rs, :] = jnp.where(is_raw, z, act).astype(BF16)


def _inproj(h, w_in_t, layer, n_aligned, shift, n_tiles, tm):
    r, d = h.shape
    tn = INPROJ_TN
    kern = functools.partial(_inproj_kernel, tiles_per_seg=d // tn)
    assert shift % 8 == 0
    row0 = lambda j: pl.multiple_of(j * tn + jnp.where(j >= n_aligned, shift, 0), 8)
    return pl.pallas_call(
        kern,
        grid=(n_tiles, r // tm),
        in_specs=[pl.BlockSpec((tm, d), lambda j, i: (i, 0)),
                  pl.BlockSpec((None, pl.Element(tn), pl.Element(d)), lambda j, i: (layer, row0(j), 0))],
        out_specs=pl.BlockSpec((tm, tn), lambda j, i: (i, j)),
        out_shape=jax.ShapeDtypeStruct((r, n_tiles * tn), BF16),
        scratch_shapes=[pltpu.VMEM((d, tn), BF16)],
        compiler_params=_cparams(("arbitrary", "arbitrary")),
        name="inproj",
    )(h, w_in_t)


def _conv_kernel(x_ref, w_ref, b_ref, o_ref, *, q_tiles):
    c = pl.program_id(1)
    x = x_ref[...].astype(F32)
    t = x.shape[0]
    rows = lax.broadcasted_iota(I32, x.shape, 0)
    w = w_ref[...]
    acc = x * w[CONV_W // 2:CONV_W // 2 + 1, :] + b_ref[...]
    for dlt in range(-(CONV_W // 2), CONV_W // 2 + 1):
        if dlt == 0:
            continue
        xs = pltpu.roll(x, shift=(-dlt) % t, axis=0)
        valid = (rows + dlt >= 0) & (rows + dlt < t)
        acc = acc + jnp.where(valid, xs, 0.0) * w[dlt + CONV_W // 2:dlt + CONV_W // 2 + 1, :]
    y = acc * _sigmoid(acc)
    scale = jnp.where(c < q_tiles, QK_DIM ** -0.5, 1.0).astype(F32)
    o_ref[...] = (y * scale).astype(BF16)


def _conv(z3, conv_w, conv_b):
    b, t, _ = z3.shape
    width = conv_w.shape[1]
    tc = 256
    kern = functools.partial(_conv_kernel, q_tiles=(width // 2) // tc)
    return pl.pallas_call(
        kern,
        grid=(b, width // tc),
        in_specs=[pl.BlockSpec((None, t, tc), lambda i, c: (i, 0, c)),
                  pl.BlockSpec((CONV_W, tc), lambda i, c: (0, c)),
                  pl.BlockSpec((1, tc), lambda i, c: (0, c))],
        out_specs=pl.BlockSpec((None, t, tc), lambda i, c: (i, 0, c)),
        out_shape=jax.ShapeDtypeStruct((b, t, width), BF16),
        compiler_params=_cparams(("arbitrary", "arbitrary")),
        name="conv",
    )(z3, conv_w, conv_b.reshape(1, width))


def _mlstm_kernel(qkc_f, qkl_f, vc_f, vl_f, gc_f, gl_f, qkc_b, qkl_b, vc_b, vl_b, gc_b, gl_b,
                  hf_ref, hb_ref, *state, n_ctx_chunks):
    c_scr, m_scr = state[:2 * M_HEADS], state[2 * M_HEADS:]
    s = pl.program_id(1)

    @pl.when(s == 0)
    def _():
        for ref in state:
            ref[...] = jnp.zeros_like(ref)

    is_ctx = s < n_ctx_chunks
    rows = lax.broadcasted_iota(I32, (CHUNK, CHUNK), 0)
    lanes = lax.broadcasted_iota(I32, (CHUNK, CHUNK), 1)
    ones_col = jnp.where(lanes == 0, 1.0, 0.0).astype(BF16)
    dn_nt = (((1,), (1,)), ((), ()))
    dn_tn = (((0,), (0,)), ((), ()))
    qk_w = M_HEADS * QK_DIM

    dirs = ((qkc_f, qkl_f, vc_f, vl_f, gc_f, gl_f, hf_ref), (qkc_b, qkl_b, vc_b, vl_b, gc_b, gl_b, hb_ref))
    per_dir = []
    for d, (qkc, qkl, vc, vl, gc, gl, out_ref) in enumerate(dirs):
        qk = jnp.where(is_ctx, qkc[...], qkl[...])
        v = jnp.where(is_ctx, vc[...], vl[...])
        g = jnp.where(is_ctx, gc[...], gl[...])
        mask = (lanes <= rows) if d == 0 else (lanes >= rows)
        tri = jnp.where(mask, 1.0, 0.0).astype(BF16)
        ls = jnp.minimum(g, 0.0) - jnp.log(1.0 + jnp.exp(-jnp.abs(g)))
        l1, l2, l3 = _split3(ls)
        bcol = (jnp.dot(tri, l1, preferred_element_type=F32) + jnp.dot(tri, l2, preferred_element_type=F32)
                + jnp.dot(tri, l3, preferred_element_type=F32))
        brow, grow = bcol.T, g.T
        c0 = 2 * M_HEADS * d
        ib = grow[c0:c0 + M_HEADS, :] - brow[c0 + M_HEADS:c0 + 2 * M_HEADS, :]
        lane_h = lax.broadcasted_iota(I32, ib.shape, 1)
        rmax = ib
        step = 1
        while step < CHUNK:
            if d == 0:
                prev = jnp.where(lane_h >= step, pltpu.roll(rmax, step, axis=1), -jnp.inf)
            else:
                prev = jnp.where(lane_h < CHUNK - step, pltpu.roll(rmax, CHUNK - step, axis=1), -jnp.inf)
            rmax = jnp.maximum(rmax, prev)
            step *= 2
        rmax_col = jnp.concatenate([rmax, jnp.zeros((CHUNK - M_HEADS, CHUNK), F32)], axis=0).T
        per_dir.append((d, qk, v, g, mask, bcol, ib, rmax_col, out_ref))

    units = [(pd, h) for pd in per_dir for h in range(M_HEADS)]
    stage1 = []
    for (d, qk, v, g, mask, bcol, ib, rmax_col, out_ref), h in units:
        col_i = 2 * M_HEADS * d + h
        col_f = col_i + M_HEADS
        idx = d * M_HEADS + h
        end = CHUNK - 1 if d == 0 else 0
        bc = bcol[:, col_f:col_f + 1]
        ic = g[:, col_i:col_i + 1]
        ibr = ib[h:h + 1, :]
        m = m_scr[idx][0:1, 0:1]
        inter = bc + m
        m_t = jnp.maximum(inter, bc + rmax_col[:, h:h + 1])
        w_inter = jnp.exp(inter - m_t)
        b_end = bc[end:end + 1, :]
        m_new = jnp.maximum(b_end + m, b_end + rmax_col[end:end + 1, h:h + 1])
        decay = jnp.exp(b_end + m - m_new)
        w_col = jnp.exp(b_end - bc + ic - m_new)
        q = qk[:, h * QK_DIM:(h + 1) * QK_DIM]
        k = qk[:, qk_w + h * QK_DIM:qk_w + (h + 1) * QK_DIM]
        decay_mat = jnp.exp(jnp.where(mask, (bc - m_t) + ibr, -jnp.inf))
        sc = lax.dot_general(q, k, dn_nt, preferred_element_type=F32) * decay_mat
        lhs = jnp.concatenate([(q.astype(F32) * w_inter).astype(BF16), sc.astype(BF16)], axis=1)
        kw = (k.astype(F32) * w_col).astype(BF16)
        stage1.append((lhs, kw, m_t, m_new, decay))

    stage2 = []
    for ((d, qk, v, g, mask, bcol, brow, grow, out_ref), h), (lhs, kw, m_t, m_new, decay) in zip(units, stage1):
        idx = d * M_HEADS + h
        vaug = jnp.concatenate([v[:, h * V_DIM:(h + 1) * V_DIM], ones_col], axis=1)
        caug = c_scr[idx][...]
        num = jnp.dot(lhs, jnp.concatenate([caug.astype(BF16), vaug], axis=0), preferred_element_type=F32)
        upd = lax.dot_general(kw, vaug, dn_tn, preferred_element_type=F32)
        stage2.append((num, upd, caug))

    for ((d, qk, v, g, mask, bcol, brow, grow, out_ref), h), (lhs, kw, m_t, m_new, decay), (num, upd, caug) in zip(
            units, stage1, stage2):
        idx = d * M_HEADS + h
        den = num[:, V_DIM:V_DIM + 1]
        hout = num[:, :V_DIM] / jnp.maximum(jnp.abs(den), jnp.exp(-m_t))
        c_scr[idx][...] = decay * caug + upd
        m_scr[idx][...] = jnp.broadcast_to(m_new, m_scr[idx].shape)
        out_ref[:, h * V_DIM:(h + 1) * V_DIM] = hout.astype(out_ref.dtype)


def _mlstm(qk_ctx, qk_lat, z_ctx3, z_lat3, g_ctx3, g_lat3):
    b, tc, _ = qk_ctx.shape
    t = qk_lat.shape[1]
    nc, nl = tc // CHUNK, t // CHUNK
    vw = M_HEADS * V_DIM
    qkw = 2 * M_HEADS * QK_DIM

    f_ctx = lambda i, s: (i, jnp.minimum(s, nc - 1), 0)
    f_lat = lambda i, s: (i, jnp.clip(s - nc, 0, nl - 1), 0)
    b_ctx = lambda i, s: (i, jnp.maximum(nc - 1 - s, 0), 0)
    b_lat = lambda i, s: (i, jnp.clip(nc + nl - 1 - s, 0, nl - 1), 0)
    v_of = lambda f: (lambda i, s: (f(i, s)[0], f(i, s)[1], 1))

    def specs(fc, fl):
        return [pl.BlockSpec((None, CHUNK, qkw), fc), pl.BlockSpec((None, CHUNK, qkw), fl),
                pl.BlockSpec((None, CHUNK, vw), v_of(fc)), pl.BlockSpec((None, CHUNK, vw), v_of(fl)),
                pl.BlockSpec((None, CHUNK, LANES), fc), pl.BlockSpec((None, CHUNK, LANES), fl)]

    kern = functools.partial(_mlstm_kernel, n_ctx_chunks=nc)
    args = (qk_ctx, qk_lat, z_ctx3, z_lat3, g_ctx3, g_lat3)
    return pl.pallas_call(
        kern,
        grid=(b, nc + nl),
        in_specs=specs(f_ctx, f_lat) + specs(b_ctx, b_lat),
        out_specs=[pl.BlockSpec((None, CHUNK, vw), f_lat), pl.BlockSpec((None, CHUNK, vw), b_lat)],
        out_shape=[jax.ShapeDtypeStruct((b, t, vw), BF16), jax.ShapeDtypeStruct((b, t, vw), BF16)],
        scratch_shapes=([pltpu.VMEM((QK_DIM, V_DIM + LANES), F32)] * (2 * M_HEADS)
                        + [pltpu.VMEM((8, LANES), F32)] * (2 * M_HEADS)),
        compiler_params=_cparams(("arbitrary", "arbitrary")),
        name="mlstm",
    )(*args, *args)


MERGE_ROWS = 256


def _merge_kernel(hf_ref, hb_ref, o_ref, u_ref, sv_ref, ga_ref, gb_ref, gm_ref, gs_ref, sw_ref, sb_ref,
                  pa_ref, pb_ref, out_ref):
    tm, d = out_ref.shape
    gw = d // SGU_GROUPS
    for grp in range(tm // MERGE_ROWS):
        r0 = grp * MERGE_ROWS
        rs = slice(r0, r0 + MERGE_ROWS)
        a_parts = []
        for h in range(M_HEADS):
            sl = slice(h * V_DIM, (h + 1) * V_DIM)
            hs = hf_ref[rs, sl].astype(F32) + hb_ref[rs, sl].astype(F32)
            ms = jnp.mean(hs * hs, axis=-1, keepdims=True)
            y = hs * lax.rsqrt(ms + EPS) * gm_ref[:, sl]
            a_parts.append((y * o_ref[rs, sl].astype(F32)).astype(BF16))
        a = jnp.concatenate(a_parts, axis=1)
        sv = sv_ref[rs, :].astype(F32)
        ms = jnp.mean(sv * sv, axis=-1, keepdims=True)
        svn = (sv * lax.rsqrt(ms + EPS) * gs_ref[...]).astype(BF16)
        b_rows = []
        for c in range(MERGE_ROWS // CHUNK):
            cr = slice(c * CHUNK, (c + 1) * CHUNK)
            b_parts = []
            for gi in range(SGU_GROUPS):
                cs = slice(gi * gw, (gi + 1) * gw)
                zz = jnp.dot(sw_ref[gi], svn[cr, cs], preferred_element_type=F32) + sb_ref[:, gi:gi + 1]
                b_parts.append((u_ref[r0 + c * CHUNK:r0 + (c + 1) * CHUNK, cs].astype(F32) * zz).astype(BF16))
            b_rows.append(jnp.concatenate(b_parts, axis=1))
        b = jnp.concatenate(b_rows, axis=0)
        ya = jnp.dot(a, pa_ref[...], preferred_element_type=F32)
        yb = jnp.dot(b, pb_ref[...], preferred_element_type=F32)
        out_ref[rs, :] = (ga_ref[rs, :].astype(F32) * ya + gb_ref[rs, :].astype(F32) * yb).astype(BF16)


def _merge(hf, hb, z, gm, gs, sgu_w, sgu_bt, pa, pb):
    r, d = hf.shape
    tm = 512
    row = lambda k: (lambda i: (i, k))
    const = lambda shape: pl.BlockSpec(shape, lambda i: (0,) * len(shape), pipeline_mode=pl.Buffered(1))
    return pl.pallas_call(
        _merge_kernel,
        grid=(r // tm,),
        in_specs=[pl.BlockSpec((tm, d), row(0)), pl.BlockSpec((tm, d), row(0)),
                  pl.BlockSpec((tm, d), row(2)), pl.BlockSpec((tm, d), row(3)), pl.BlockSpec((tm, d), row(4)),
                  pl.BlockSpec((tm, d), row(5)), pl.BlockSpec((tm, d), row(6)),
                  const((1, d)), const((1, d)), const(sgu_w.shape), const(sgu_bt.shape),
                  const((d, d)), const((d, d))],
        out_specs=pl.BlockSpec((tm, d), row(0)),
        out_shape=jax.ShapeDtypeStruct((r, d), BF16),
        compiler_params=_cparams(("arbitrary",)),
        name="merge",
    )(hf, hb, z, z, z, z, z, gm, gs, sgu_w, sgu_bt, pa, pb)


def _outproj_kernel(m_ref, w_ref, x_ref, gate_ref, shift_ref, scale_ref, g_ref, rw_ref, rb_ref,
                    x1_ref, h2_ref, lg_ref):
    r1, r2, _ = _split3(rw_ref[...])
    ne = lg_ref.shape[0]
    for grp in range(x_ref.shape[0] // MERGE_ROWS):
        r0 = grp * MERGE_ROWS
        rs = slice(r0, r0 + MERGE_ROWS)
        out = jnp.dot(m_ref[rs, :], w_ref[...], preferred_element_type=F32)
        x1 = x_ref[rs, :] + gate_ref[...] * out
        x1_ref[rs, :] = x1
        ms = jnp.mean(x1 * x1, axis=-1, keepdims=True)
        h2 = (x1 * lax.rsqrt(ms + EPS) * g_ref[...]) * (1.0 + scale_ref[...]) + shift_ref[...]
        hi, lo, _ = _split3(h2)
        lg = (jnp.dot(hi, r1, preferred_element_type=F32) + jnp.dot(lo, r1, preferred_element_type=F32)
              + jnp.dot(hi, r2, preferred_element_type=F32)) + rb_ref[...]
        lg_ref[:, rs] = lg.T[:ne, :]
        _store_token_major(h2_ref, r0, h2)


def _outproj(merged, w_out, x2d, mod3, mod_row_of_tile, norm_g, rw, rb, tm):
    r, d = x2d.shape
    ne = rw.shape[1]
    rw_t = jnp.pad(rw, ((0, 0), (0, LANES - ne)))
    rb_col = jnp.pad(rb.reshape(1, ne), ((0, 0), (0, LANES - ne)))
    modspec = lambda k: pl.BlockSpec((None, 1, d), lambda i: (mod_row_of_tile(i), 0, k))
    return pl.pallas_call(
        _outproj_kernel,
        grid=(r // tm,),
        in_specs=[pl.BlockSpec((tm, d), lambda i: (i, 0)),
                  pl.BlockSpec((d, d), lambda i: (0, 0), pipeline_mode=pl.Buffered(1)),
                  pl.BlockSpec((tm, d), lambda i: (i, 0)),
                  modspec(2), modspec(3), modspec(4),
                  pl.BlockSpec((1, d), lambda i: (0, 0)),
                  pl.BlockSpec((d, LANES), lambda i: (0, 0)),
                  pl.BlockSpec((1, LANES), lambda i: (0, 0))],
        out_specs=[pl.BlockSpec((tm, d), lambda i: (i, 0)),
                   pl.BlockSpec((tm * TOKEN_ROWS, LANES), lambda i: (i, 0)),
                   pl.BlockSpec((ne, tm), lambda i: (0, i))],
        out_shape=[jax.ShapeDtypeStruct((r, d), F32),
                   jax.ShapeDtypeStruct((r * TOKEN_ROWS, LANES), F32),
                   jax.ShapeDtypeStruct((ne, r), F32)],
        compiler_params=_cparams(("arbitrary",)),
        name="outproj",
    )(merged, w_out, x2d, mod3, mod3, mod3, norm_g, rw_t, rb_col)


def _route_kernel(lg_ref, eid_ref, gate_ref, rank_ref, cnt_ref, carry_scr):
    i = pl.program_id(0)

    @pl.when(i == 0)
    def _():
        carry_scr[...] = jnp.zeros_like(carry_scr)

    l = lg_ref[...]
    ne, tm = l.shape
    e_iota = lax.broadcasted_iota(I32, (ne, tm), 0)
    vals, onehots = [], []
    for k in range(TOP_K):
        mx = jnp.max(l, axis=0, keepdims=True)
        idx = jnp.min(jnp.where(l == mx, e_iota, ne), axis=0, keepdims=True)
        oh = e_iota == idx
        l = jnp.where(oh, -jnp.inf, l)
        vals.append(mx)
        onehots.append(oh)
        eid_ref[k:k + 1, :] = idx
    ex = [jnp.exp(vk - vals[0]) for vk in vals]
    tot = ex[0] + ex[1] + ex[2] + ex[3]
    for k in range(TOP_K):
        gate_ref[k:k + 1, :] = ex[k] / tot

    oh_all = jnp.zeros((ne, tm), F32)
    for oh in onehots:
        oh_all = oh_all + jnp.where(oh, 1.0, 0.0)
    oh_all = oh_all.astype(BF16)
    r_i = lax.broadcasted_iota(I32, (LANES, LANES), 0)
    c_i = lax.broadcasted_iota(I32, (LANES, LANES), 1)
    tri_excl = jnp.where(r_i < c_i, 1.0, 0.0).astype(BF16)
    ones_m = jnp.ones((LANES, LANES), BF16)
    carry = carry_scr[...]
    for blk in range(tm // LANES):
        sl = slice(blk * LANES, (blk + 1) * LANES)
        ohb = oh_all[:, sl]
        cum = jnp.dot(ohb, tri_excl, preferred_element_type=F32) + carry
        for k in range(TOP_K):
            rk = jnp.sum(jnp.where(onehots[k][:, sl], cum, 0.0), axis=0, keepdims=True)
            rank_ref[k:k + 1, sl] = rk.astype(I32)
        carry = carry + jnp.dot(ohb, ones_m, preferred_element_type=F32)
    carry_scr[...] = carry
    cnt_ref[...] = carry


def _route(logits_t):
    ne, r = logits_t.shape
    tm = 1024
    return pl.pallas_call(
        _route_kernel,
        grid=(r // tm,),
        in_specs=[pl.BlockSpec((ne, tm), lambda i: (0, i))],
        out_specs=[pl.BlockSpec((TOP_K, tm), lambda i: (0, i)),
                   pl.BlockSpec((TOP_K, tm), lambda i: (0, i)),
                   pl.BlockSpec((TOP_K, tm), lambda i: (0, i)),
                   pl.BlockSpec((ne, LANES), lambda i: (0, 0))],
        out_shape=[jax.ShapeDtypeStruct((TOP_K, r), I32),
                   jax.ShapeDtypeStruct((TOP_K, r), F32),
                   jax.ShapeDtypeStruct((TOP_K, r), I32),
                   jax.ShapeDtypeStruct((ne, LANES), F32)],
        scratch_shapes=[pltpu.VMEM((ne, LANES), F32)],
        compiler_params=_cparams(("arbitrary",)),
        name="route",
    )(logits_t)


def _slot_kernel(eid_ref, rank_ref, cnt_ref, slot_ref):
    ne = cnt_ref.shape[0]
    nblk = jnp.floor((cnt_ref[...] + (MOE_TILE - 1.0)) * (1.0 / MOE_TILE))
    r_i = lax.broadcasted_iota(I32, (ne, ne), 0)
    c_i = lax.broadcasted_iota(I32, (ne, ne), 1)
    tri = jnp.where(c_i < r_i, 1.0, 0.0).astype(BF16)
    start = jnp.dot(tri, nblk.astype(BF16), preferred_element_type=F32) * MOE_TILE
    eid = eid_ref[...]
    acc = rank_ref[...]
    for e in range(ne):
        acc = acc + jnp.where(eid == e, start[e:e + 1, 0:1].astype(I32), 0)
    slot_ref[...] = acc


def _slots(eid, rank, cnt):
    return pl.pallas_call(
        _slot_kernel,
        out_shape=jax.ShapeDtypeStruct(eid.shape, I32),
        compiler_params=pltpu.CompilerParams(vmem_limit_bytes=V7X_VMEM_LIMIT_BYTES),
        name="slots",
    )(eid, rank, cnt)


def _dispatch_copy(slot_ref, h_ref, xs_ref, sem, k, t, base, n_tok):
    d0 = pl.multiple_of(slot_ref[k * n_tok + base + t] * TOKEN_ROWS, TOKEN_ROWS)
    s0 = pl.multiple_of(t * TOKEN_ROWS, TOKEN_ROWS)
    return pltpu.make_async_copy(h_ref.at[pl.ds(s0, TOKEN_ROWS), :], xs_ref.at[pl.ds(d0, TOKEN_ROWS), :], sem)


def _dispatch_kernel(slot_ref, h_ref, init_ref, xs_ref, sem, *, tm, n_tok):
    base = pl.program_id(0) * tm

    def start(t, carry):
        for k in range(TOP_K):
            _dispatch_copy(slot_ref, h_ref, xs_ref, sem, k, t, base, n_tok).start()
        return carry

    def wait(t, carry):
        for k in range(TOP_K):
            _dispatch_copy(slot_ref, h_ref, xs_ref, sem, k, t, base, n_tok).wait()
        return carry

    lax.fori_loop(0, tm, start, 0, unroll=4)
    lax.fori_loop(0, tm, wait, 0, unroll=4)


def _dispatch(h2p, slot, p_rows):
    n_tok = h2p.shape[0] // TOKEN_ROWS
    tm = 512
    init = jnp.zeros((p_rows * TOKEN_ROWS, LANES), h2p.dtype)
    grid_spec = pltpu.PrefetchScalarGridSpec(
        num_scalar_prefetch=1,
        grid=(n_tok // tm,),
        in_specs=[pl.BlockSpec((tm * TOKEN_ROWS, LANES), lambda i, slot: (i, 0)),
                  pl.BlockSpec(memory_space=pl.ANY)],
        out_specs=pl.BlockSpec(memory_space=pl.ANY),
        scratch_shapes=[pltpu.SemaphoreType.DMA(())],
    )
    return pl.pallas_call(
        functools.partial(_dispatch_kernel, tm=tm, n_tok=n_tok),
        grid_spec=grid_spec,
        out_shape=jax.ShapeDtypeStruct(init.shape, init.dtype),
        input_output_aliases={2: 0},
        compiler_params=_cparams(("arbitrary",)),
        name="dispatch",
    )(slot, h2p, init)


def _weight_copies(w_hbm, wbuf, sems, e, col_starts, slot):
    tn = wbuf.shape[-1]
    aligned = lambda c: c if isinstance(c, int) else pl.multiple_of(c, LANES)
    return [pltpu.make_async_copy(w_hbm.at[e, :, pl.ds(aligned(c), tn)], wbuf.at[slot, i], sems.at[slot])
            for i, c in enumerate(col_starts)]


def _advance_weight_pipeline(w_hbm, wbuf, sems, sched, j, nj, it, col_starts_of):
    e_ref, run_ref, last_ref, nxt_ref, nruns_ref = sched
    slot = (j * nruns_ref[0] + run_ref[it]) % 2

    @pl.when((j == 0) & (it == 0))
    def _():
        for cp in _weight_copies(w_hbm, wbuf, sems, e_ref[it], col_starts_of(j), slot):
            cp.start()

    for cp in _weight_copies(w_hbm, wbuf, sems, e_ref[it], col_starts_of(j), slot):
        cp.wait()
    is_last = last_ref[it] == 1

    @pl.when(jnp.logical_not(is_last))
    def _():
        for cp in _weight_copies(w_hbm, wbuf, sems, nxt_ref[it], col_starts_of(j), 1 - slot):
            cp.start()

    @pl.when(is_last & (j + 1 < nj))
    def _():
        for cp in _weight_copies(w_hbm, wbuf, sems, nxt_ref[it], col_starts_of(j + 1), 1 - slot):
            cp.start()

    return slot


def _moe_up_kernel(blk_ref, oblk_ref, e_ref, nv_ref, new_ref, run_ref, last_ref, nxt_ref, nruns_ref,
                   xs_ref, w_hbm, bg_ref, bl_ref, act_ref, wbuf, sems, wg_bf, wl_bf):
    j = pl.program_id(0)
    nj = pl.num_programs(0)
    it = pl.program_id(1)
    tn = wg_bf.shape[1]

    @pl.when(new_ref[it] == 1)
    def _():
        slot = _advance_weight_pipeline(w_hbm, wbuf, sems, (e_ref, run_ref, last_ref, nxt_ref, nruns_ref), j, nj, it,
                                        lambda jj: (jj * tn, (nj + jj) * tn))
        wg_bf[...] = wbuf[slot, 0].astype(BF16)
        wl_bf[...] = wbuf[slot, 1].astype(BF16)

    def rows_chunk(r0, n):
        x = _load_token_major(xs_ref, r0, n).astype(BF16)
        hg = jnp.dot(x, wg_bf[...], preferred_element_type=F32) + bg_ref[...]
        hl = jnp.dot(x, wl_bf[...], preferred_element_type=F32) + bl_ref[...]
        glu = jnp.minimum(hg, SWIGLU_LIMIT)
        lin = jnp.clip(hl, -SWIGLU_LIMIT, SWIGLU_LIMIT)
        act_ref[r0:r0 + n, :] = ((0.5 * glu) * (1.0 + jnp.tanh((0.5 * SWIGLU_ALPHA) * glu)) * (lin + 1.0)).astype(BF16)

    def rows_zero(r0, n):
        act_ref[r0:r0 + n, :] = jnp.zeros((n, act_ref.shape[1]), BF16)

    _moe_row_variants(nv_ref[it], rows_chunk, rows_zero)


def _moe_row_variants(nv, rows_chunk, rows_zero):
    for units in range(MOE_TILE // MOE_STEP + 1):
        rows = units * MOE_STEP

        @pl.when((nv > rows - MOE_STEP) & (nv <= rows) if units else nv <= 0)
        def _():
            r0 = 0
            while r0 < rows:
                n = min(MOE_HALF, rows - r0)
                rows_chunk(r0, n)
                r0 += n
            if rows < MOE_TILE:
                rows_zero(rows, MOE_TILE - rows)


def _moe_down_kernel(blk_ref, oblk_ref, e_ref, nv_ref, new_ref, run_ref, last_ref, nxt_ref, nruns_ref,
                     a_ref, w_hbm, b_ref, y_ref, wbuf, sems, w_bf):
    it = pl.program_id(0)

    @pl.when(new_ref[it] == 1)
    def _():
        slot = _advance_weight_pipeline(w_hbm, wbuf, sems, (e_ref, run_ref, last_ref, nxt_ref, nruns_ref), 0, 1, it,
                                        lambda jj: (0,))
        w_bf[...] = wbuf[slot, 0].astype(BF16)

    def rows_chunk(r0, n):
        y = jnp.dot(a_ref[r0:r0 + n, :], w_bf[...], preferred_element_type=F32) + b_ref[...]
        _store_token_major(y_ref, r0, y)

    def rows_zero(r0, n):
        y_ref[r0 * TOKEN_ROWS:(r0 + n) * TOKEN_ROWS, :] = jnp.zeros((n * TOKEN_ROWS, LANES), F32)

    _moe_row_variants(nv_ref[it], rows_chunk, rows_zero)


def _moe_up(items, xs, w1, b1):
    n_items = items[0].shape[0]
    ne, d, two_f = w1.shape
    f = two_f // 2
    nj = f // MOE_TN
    p = xs.shape[0] // TOKEN_ROWS
    b1r = b1.reshape(ne, 1, two_f)
    grid_spec = pltpu.PrefetchScalarGridSpec(
        num_scalar_prefetch=len(items),
        grid=(nj, n_items),
        in_specs=[pl.BlockSpec((MOE_TILE * TOKEN_ROWS, LANES), lambda j, it, blk, *_: (blk[it], 0)),
                  pl.BlockSpec(memory_space=pl.ANY),
                  pl.BlockSpec((None, 1, MOE_TN), lambda j, it, blk, oblk, e, *_: (e[it], 0, j)),
                  pl.BlockSpec((None, 1, MOE_TN), lambda j, it, blk, oblk, e, *_: (e[it], 0, nj + j))],
        out_specs=pl.BlockSpec((MOE_TILE, MOE_TN), lambda j, it, blk, oblk, *_: (oblk[it], j)),
        scratch_shapes=[pltpu.VMEM((2, 2, d, MOE_TN), F32), pltpu.SemaphoreType.DMA((2,)),
                        pltpu.VMEM((d, MOE_TN), BF16), pltpu.VMEM((d, MOE_TN), BF16)],
    )
    return pl.pallas_call(
        _moe_up_kernel,
        grid_spec=grid_spec,
        out_shape=jax.ShapeDtypeStruct((p, f), BF16),
        compiler_params=_cparams(("arbitrary", "arbitrary")),
        name="moe_up",
    )(*items, xs, w1, b1r, b1r)


def _moe_down(items, act, w2, b2):
    n_items = items[0].shape[0]
    ne, f, d = w2.shape
    p = act.shape[0]
    b2r = b2.reshape(ne, 1, d)
    grid_spec = pltpu.PrefetchScalarGridSpec(
        num_scalar_prefetch=len(items),
        grid=(n_items,),
        in_specs=[pl.BlockSpec((MOE_TILE, f), lambda it, blk, *_: (blk[it], 0)),
                  pl.BlockSpec(memory_space=pl.ANY),
                  pl.BlockSpec((None, 1, d), lambda it, blk, oblk, e, *_: (e[it], 0, 0))],
        out_specs=pl.BlockSpec((MOE_TILE * TOKEN_ROWS, LANES), lambda it, blk, oblk, *_: (oblk[it], 0)),
        scratch_shapes=[pltpu.VMEM((2, 1, f, d), F32), pltpu.SemaphoreType.DMA((2,)), pltpu.VMEM((f, d), BF16)],
    )
    return pl.pallas_call(
        _moe_down_kernel,
        grid_spec=grid_spec,
        out_shape=jax.ShapeDtypeStruct((p * TOKEN_ROWS, LANES), F32),
        compiler_params=_cparams(("arbitrary",)),
        name="moe_down",
    )(*items, act, w2, b2r)


def _combine_copy(slot_ref, y_ref, buf, sems, k, t, tile, par, tm, n_tok):
    s0 = pl.multiple_of(slot_ref[k * n_tok + tile * tm + t] * TOKEN_ROWS, TOKEN_ROWS)
    d0 = pl.multiple_of(t * TOKEN_ROWS, TOKEN_ROWS)
    return pltpu.make_async_copy(y_ref.at[pl.ds(s0, TOKEN_ROWS), :], buf.at[par, k, pl.ds(d0, TOKEN_ROWS), :],
                                 sems.at[par])


def _final_kernel(slot_ref, y_ref, gates_ref, x1_ref, gate_ref, g_ref, o_ref, buf, sems, *, n_tok):
    tm, d = x1_ref.shape
    i = pl.program_id(0)
    par = i % 2

    def gather(tile, par_, start):
        def body(t, carry):
            for k in range(TOP_K):
                cp = _combine_copy(slot_ref, y_ref, buf, sems, k, t, tile, par_, tm, n_tok)
                cp.start() if start else cp.wait()
            return carry
        lax.fori_loop(0, tm, body, 0, unroll=4)

    @pl.when(i == 0)
    def _():
        gather(0, 0, True)

    @pl.when(i + 1 < pl.num_programs(0))
    def _():
        gather(i + 1, 1 - par, True)

    gather(i, par, False)

    acc = jnp.zeros((tm, d), F32)
    for k in range(TOP_K):
        acc = acc + gates_ref[:, k:k + 1] * _load_token_major(buf, 0, tm, lead=(par, k))
    xf = x1_ref[...] + gate_ref[...] * acc
    ms = jnp.mean(xf * xf, axis=-1, keepdims=True)
    o_ref[...] = xf * lax.rsqrt(ms + EPS) * g_ref[...]


def _final(slot, y, gates_tk, x1, mod3, mod_row_of_tile, final_g, tm):
    r, d = x1.shape
    grid_spec = pltpu.PrefetchScalarGridSpec(
        num_scalar_prefetch=1,
        grid=(r // tm,),
        in_specs=[pl.BlockSpec(memory_space=pl.ANY),
                  pl.BlockSpec((tm, TOP_K), lambda i, slot: (i, 0)),
                  pl.BlockSpec((tm, d), lambda i, slot: (i, 0)),
                  pl.BlockSpec((None, 1, d), lambda i, slot: (mod_row_of_tile(i), 0, 5)),
                  pl.BlockSpec((1, d), lambda i, slot: (0, 0))],
        out_specs=pl.BlockSpec((tm, d), lambda i, slot: (i, 0)),
        scratch_shapes=[pltpu.VMEM((2, TOP_K, tm * TOKEN_ROWS, LANES), F32), pltpu.SemaphoreType.DMA((2,))],
    )
    return pl.pallas_call(
        functools.partial(_final_kernel, n_tok=r),
        grid_spec=grid_spec,
        out_shape=jax.ShapeDtypeStruct((r, d), F32),
        compiler_params=_cparams(("arbitrary",)),
        name="final",
    )(slot, y, gates_tk, x1, mod3, final_g)


def _moe_schedule(counts, n_items):
    ne = counts.shape[0]
    padded = (counts + MOE_TILE - 1) // MOE_TILE * MOE_TILE
    pad_end = jnp.cumsum(padded)
    start_pad = pad_end - padded
    n_real = pad_end[-1] // MOE_TILE
    b = jnp.arange(n_items, dtype=I32)
    blk = jnp.minimum(b, jnp.maximum(n_real - 1, 0))
    blk_start = blk * MOE_TILE
    e_of = jnp.minimum(jnp.sum((pad_end[None, :] <= blk_start[:, None]).astype(I32), axis=1), ne - 1)
    nv = jnp.clip(counts[e_of] - (blk_start - start_pad[e_of]), 0, MOE_TILE)
    nv = jnp.where(b < n_real, nv, 0)
    e_of = e_of.astype(I32)
    prev_e = jnp.concatenate([jnp.full((1,), -1, I32), e_of[:-1]])
    new = ((e_of != prev_e) | (b == 0)).astype(I32)
    run = jnp.cumsum(new) - 1
    n_runs = run[-1] + 1
    last = (run == n_runs - 1).astype(I32)
    first_pos = jnp.where(new == 1, b, n_items)
    nxt_pos = jnp.concatenate([lax.cummin(first_pos, reverse=True)[1:], jnp.full((1,), n_items, I32)])
    nxt = jnp.where(nxt_pos < n_items, e_of[jnp.minimum(nxt_pos, n_items - 1)], e_of[0])
    return (blk.astype(I32), b, e_of, nv.astype(I32), new, run.astype(I32), last, nxt.astype(I32),
            n_runs.reshape(1).astype(I32))


def kernel(x, c, ctx, c_ctx, mod_w, mod_b, norm1_g, norm2_g, w_in, gate_b, conv_w, conv_b, mlstm_norm_g,
           sgu_norm_g, sgu_w, sgu_b, proj_a, proj_b, w_out, router_w, router_b, exp_w1, exp_b1, exp_w2,
           exp_b2, final_g):
    bsz, t, d = x.shape
    tc = ctx.shape[1]
    depth = mod_w.shape[0]
    assert depth == 1, "single-layer block"
    assert d == TOKEN_ROWS * LANES, "token-major rows hold TOKEN_ROWS*128 features"
    l = 0
    r = bsz * t
    qkw = 2 * M_HEADS * QK_DIM
    vw = M_HEADS * V_DIM
    off_gates = qkw + vw
    n_gates = 4 * M_HEADS
    off_o = off_gates + n_gates

    cc = jnp.zeros((8, d), F32).at[:bsz].set(c).at[bsz].set(c_ctx)
    mod3 = _mod(cc, mod_w[l], mod_b[l]).reshape(8, 1, N_MOD * d)

    in_cols = w_in.shape[2]
    assert off_gates % INPROJ_TN == 0 and (in_cols - off_o) % INPROJ_TN == 0
    n_aligned = off_gates // INPROJ_TN
    n_tiles = n_aligned + (in_cols - off_o) // INPROJ_TN
    w_in_t = jnp.swapaxes(w_in, 1, 2)
    w_gates = jnp.pad(w_in_t[l, off_gates:off_o, :].T, ((0, 0), (0, LANES - n_gates)))
    gb = jnp.pad(gate_b[l].reshape(1, n_gates), ((0, 0), (0, LANES - n_gates)))
    g1 = norm1_g[l].reshape(1, d)

    tm_in = 512
    h_lat, g_lat = _norm(x.reshape(r, d), mod3, lambda i: i // (t // tm_in), g1, w_gates, gb, tm_in)
    h_ctx, g_ctx = _norm(ctx.reshape(bsz * tc, d), mod3, lambda i: bsz, g1, w_gates, gb, tc)
    z_lat = _inproj(h_lat, w_in_t, l, n_aligned, n_gates, n_tiles, 1024)
    z_ctx = _inproj(h_ctx, w_in_t, l, n_aligned, n_gates, n_aligned, bsz * tc)

    z_lat3 = z_lat.reshape(bsz, t, z_lat.shape[1])
    z_ctx3 = z_ctx.reshape(bsz, tc, z_ctx.shape[1])
    qk_lat = _conv(z_lat3, conv_w[l], conv_b[l])
    qk_ctx = _conv(z_ctx3, conv_w[l], conv_b[l])

    hf, hb = _mlstm(qk_ctx, qk_lat, z_ctx3, z_lat3, g_ctx.reshape(bsz, tc, LANES), g_lat.reshape(bsz, t, LANES))

    merged = _merge(hf.reshape(r, vw), hb.reshape(r, vw), z_lat,
                    mlstm_norm_g[l].reshape(1, vw), sgu_norm_g[l].reshape(1, -1),
                    sgu_w[l].astype(BF16), sgu_b[l].T, proj_a[l].astype(BF16), proj_b[l].astype(BF16))

    tm_out = 512
    x1, h2p, logits_t = _outproj(merged, w_out[l].astype(BF16), x.reshape(r, d), mod3,
                                 lambda i: i // (t // tm_out), norm2_g[l].reshape(1, d),
                                 router_w[l], router_b[l], tm_out)
    tm_fin = 256

    eid, gates, rank, cnt = _route(logits_t)
    counts = cnt[:, 0].astype(I32)
    n_items = (r * TOP_K) // MOE_TILE + N_EXPERTS
    items = _moe_schedule(counts, n_items)
    slot = _slots(eid, rank, cnt).reshape(-1)
    xs = _dispatch(h2p, slot, n_items * MOE_TILE)
    act = _moe_up(items, xs, exp_w1[l], exp_b1[l])
    y = _moe_down(items, act, exp_w2[l], exp_b2[l])
    out = _final(slot, y, gates.T, x1, mod3, lambda i: i // (t // tm_fin), final_g.reshape(1, d), tm_fin)
    return out.reshape(bsz, t, d)
```

```python
import functools

import jax
import jax.numpy as jnp
from jax import lax
from jax.experimental import pallas as pl
from jax.experimental.pallas import tpu as pltpu

F32 = jnp.float32
BF16 = jnp.bfloat16
I32 = jnp.int32
U32 = jnp.uint32

EPS = 1e-6
M_HEADS = 8
QK_DIM = 128
V_DIM = 256
CONV_W = 5
CHUNK = 128
SGU_GROUPS = 8
N_EXPERTS = 32
TOP_K = 4
SWIGLU_ALPHA = 1.702
SWIGLU_LIMIT = 7.0
N_MOD = 6

V7X_VMEM_LIMIT_BYTES = 56 * 1024 * 1024
LANES = 128
MOE_TILE = 512
MOE_HALF = 256
MOE_STEP = 128
MOE_TN = 1024


def _cparams(sem):
    return pltpu.CompilerParams(dimension_semantics=sem, vmem_limit_bytes=V7X_VMEM_LIMIT_BYTES)


def _sigmoid(x):
    return 1.0 / (1.0 + jnp.exp(-x))


def _gelu_tanh(x):
    return 0.5 * x * (1.0 + jnp.tanh(0.7978845608028654 * (x + 0.044715 * (x * x * x))))


def _split3(a):
    a1 = a.astype(BF16)
    r1 = a - a1.astype(F32)
    a2 = r1.astype(BF16)
    a3 = (r1 - a2.astype(F32)).astype(BF16)
    return a1, a2, a3


TOKEN_ROWS = 16


def _store_token_major(ref, row0, vals):
    n = vals.shape[0]
    for s in range(TOKEN_ROWS):
        ref[pl.ds(row0 * TOKEN_ROWS + s, n, stride=TOKEN_ROWS), :] = vals[:, s * LANES:(s + 1) * LANES]


def _load_token_major(ref, row0, n, lead=()):
    cols = [ref[(*lead, pl.ds(row0 * TOKEN_ROWS + s, n, stride=TOKEN_ROWS), slice(None))] for s in range(TOKEN_ROWS)]
    return jnp.concatenate(cols, axis=1)


def _mod_kernel(s_ref, w_ref, b_ref, o_ref):
    s = s_ref[...]
    s = s * _sigmoid(s)
    o_ref[...] = jnp.dot(s.astype(BF16), w_ref[...].astype(BF16), preferred_element_type=F32) + b_ref[...]


def _mod(cc, mod_w, mod_b):
    d, n = mod_w.shape
    tn = 1024
    return pl.pallas_call(
        _mod_kernel,
        grid=(n // tn,),
        in_specs=[pl.BlockSpec((8, d), lambda j: (0, 0)),
                  pl.BlockSpec((d, tn), lambda j: (0, j)),
                  pl.BlockSpec((1, tn), lambda j: (0, j))],
        out_specs=pl.BlockSpec((8, tn), lambda j: (0, j)),
        out_shape=jax.ShapeDtypeStruct((8, n), F32),
        compiler_params=_cparams(("arbitrary",)),
        name="mod",
    )(cc, mod_w, mod_b.reshape(1, n))


def _norm_kernel(x_ref, shift_ref, scale_ref, g_ref, wg_ref, gb_ref, h_ref, gates_ref):
    x = x_ref[...]
    ms = jnp.mean(x * x, axis=-1, keepdims=True)
    h = (x * lax.rsqrt(ms + EPS) * g_ref[...]) * (1.0 + scale_ref[...]) + shift_ref[...]
    h1, h2, _ = _split3(h)
    h_ref[...] = h1
    w1, w2, _ = _split3(wg_ref[...])
    gates_ref[...] = (jnp.dot(h1, w1, preferred_element_type=F32)
                      + jnp.dot(h1, w2, preferred_element_type=F32)
                      + jnp.dot(h2, w1, preferred_element_type=F32)) + gb_ref[...]


def _norm(x2d, mod3, mod_row_of_tile, norm_g, w_gates, gate_b, tm):
    r, d = x2d.shape
    return pl.pallas_call(
        _norm_kernel,
        grid=(r // tm,),
        in_specs=[pl.BlockSpec((tm, d), lambda i: (i, 0)),
                  pl.BlockSpec((None, 1, d), lambda i: (mod_row_of_tile(i), 0, 0)),
                  pl.BlockSpec((None, 1, d), lambda i: (mod_row_of_tile(i), 0, 1)),
                  pl.BlockSpec((1, d), lambda i: (0, 0)),
                  pl.BlockSpec((d, LANES), lambda i: (0, 0)),
                  pl.BlockSpec((1, LANES), lambda i: (0, 0))],
        out_specs=[pl.BlockSpec((tm, d), lambda i: (i, 0)),
                   pl.BlockSpec((tm, LANES), lambda i: (i, 0))],
        out_shape=[jax.ShapeDtypeStruct((r, d), BF16),
                   jax.ShapeDtypeStruct((r, LANES), F32)],
        compiler_params=_cparams(("arbitrary",)),
        name="norm",
    )(x2d, mod3, mod3, norm_g, w_gates, gate_b)


INPROJ_TN = 1024
INPROJ_ROWS = 256


def _inproj_kernel(h_ref, wt_ref, z_ref, w_bf, *, tiles_per_seg):
    j = pl.program_id(0)
    i = pl.program_id(1)

    @pl.when(i == 0)
    def _():
        w_bf[...] = wt_ref[...].T.astype(BF16)

    seg = j // tiles_per_seg
    is_raw = seg < 2
    is_gelu = (seg == 3) | (seg == 4)
    c0 = 0.7978845608028654
    p0 = jnp.where(is_gelu, c0, 0.5).astype(F32)
    p1 = jnp.where(is_gelu, c0 * 0.044715, 0.0).astype(F32)
    q0 = jnp.where(is_gelu, 0.0, 1.0).astype(F32)
    q1 = jnp.where(is_gelu, 1.0, 0.0).astype(F32)
    w = w_bf[...]
    for c in range(h_ref.shape[0] // INPROJ_ROWS):
        rs = slice(c * INPROJ_ROWS, (c + 1) * INPROJ_ROWS)
        z = jnp.dot(h_ref[rs, :], w, preferred_element_type=F32)
        act = (0.5 * (q0 + q1 * z)) * (1.0 + jnp.tanh(z * (p0 + p1 * (z * z))))
        z_ref[rs, :] = jnp.where(is_raw, z, act).astype(BF16)


def _inproj(h, w_in_t, layer, n_aligned, shift, n_tiles, tm):
    r, d = h.shape
    tn = INPROJ_TN
    kern = functools.partial(_inproj_kernel, tiles_per_seg=d // tn)
    assert shift % 8 == 0
    row0 = lambda j: pl.multiple_of(j * tn + jnp.where(j >= n_aligned, shift, 0), 8)
    return pl.pallas_call(
        kern,
        grid=(n_tiles, r // tm),
        in_specs=[pl.BlockSpec((tm, d), lambda j, i: (i, 0)),
                  pl.BlockSpec((None, pl.Element(tn), pl.Element(d)), lambda j, i: (layer, row0(j), 0))],
        out_specs=pl.BlockSpec((tm, tn), lambda j, i: (i, j)),
        out_shape=jax.ShapeDtypeStruct((r, n_tiles * tn), BF16),
        scratch_shapes=[pltpu.VMEM((d, tn), BF16)],
        compiler_params=_cparams(("arbitrary", "arbitrary")),
        name="inproj",
    )(h, w_in_t)


def _conv_kernel(x_ref, w_ref, b_ref, o_ref, *, q_tiles):
    c = pl.program_id(1)
    x = x_ref[...].astype(F32)
    t = x.shape[0]
    rows = lax.broadcasted_iota(I32, x.shape, 0)
    w = w_ref[...]
    acc = x * w[CONV_W // 2:CONV_W // 2 + 1, :] + b_ref[...]
    for dlt in range(-(CONV_W // 2), CONV_W // 2 + 1):
        if dlt == 0:
            continue
        xs = pltpu.roll(x, shift=(-dlt) % t, axis=0)
        valid = (rows + dlt >= 0) & (rows + dlt < t)
        acc = acc + jnp.where(valid, xs, 0.0) * w[dlt + CONV_W // 2:dlt + CONV_W // 2 + 1, :]
    y = acc * _sigmoid(acc)
    scale = jnp.where(c < q_tiles, QK_DIM ** -0.5, 1.0).astype(F32)
    o_ref[...] = (y * scale).astype(BF16)


def _conv(z3, conv_w, conv_b):
    b, t, _ = z3.shape
    width = conv_w.shape[1]
    tc = 256
    kern = functools.partial(_conv_kernel, q_tiles=(width // 2) // tc)
    return pl.pallas_call(
        kern,
        grid=(b, width // tc),
        in_specs=[pl.BlockSpec((None, t, tc), lambda i, c: (i, 0, c)),
                  pl.BlockSpec((CONV_W, tc), lambda i, c: (0, c)),
                  pl.BlockSpec((1, tc), lambda i, c: (0, c))],
        out_specs=pl.BlockSpec((None, t, tc), lambda i, c: (i, 0, c)),
        out_shape=jax.ShapeDtypeStruct((b, t, width), BF16),
        compiler_params=_cparams(("arbitrary", "arbitrary")),
        name="conv",
    )(z3, conv_w, conv_b.reshape(1, width))


def _mlstm_kernel(qkc_f, qkl_f, vc_f, vl_f, gc_f, gl_f, qkc_b, qkl_b, vc_b, vl_b, gc_b, gl_b,
                  hf_ref, hb_ref, *state, n_ctx_chunks):
    c_scr, m_scr = state[:2 * M_HEADS], state[2 * M_HEADS:]
    s = pl.program_id(1)

    @pl.when(s == 0)
    def _():
        for ref in state:
            ref[...] = jnp.zeros_like(ref)

    is_ctx = s < n_ctx_chunks
    rows = lax.broadcasted_iota(I32, (CHUNK, CHUNK), 0)
    lanes = lax.broadcasted_iota(I32, (CHUNK, CHUNK), 1)
    ones_col = jnp.where(lanes == 0, 1.0, 0.0).astype(BF16)
    dn_nt = (((1,), (1,)), ((), ()))
    dn_tn = (((0,), (0,)), ((), ()))
    qk_w = M_HEADS * QK_DIM

    dirs = ((qkc_f, qkl_f, vc_f, vl_f, gc_f, gl_f, hf_ref), (qkc_b, qkl_b, vc_b, vl_b, gc_b, gl_b, hb_ref))
    per_dir = []
    for d, (qkc, qkl, vc, vl, gc, gl, out_ref) in enumerate(dirs):
        qk = jnp.where(is_ctx, qkc[...], qkl[...])
        v = jnp.where(is_ctx, vc[...], vl[...])
        g = jnp.where(is_ctx, gc[...], gl[...])
        mask = (lanes <= rows) if d == 0 else (lanes >= rows)
        tri = jnp.where(mask, 1.0, 0.0).astype(BF16)
        ls = jnp.minimum(g, 0.0) - jnp.log(1.0 + jnp.exp(-jnp.abs(g)))
        l1, l2, l3 = _split3(ls)
        bcol = (jnp.dot(tri, l1, preferred_element_type=F32) + jnp.dot(tri, l2, preferred_element_type=F32)
                + jnp.dot(tri, l3, preferred_element_type=F32))
        brow, grow = bcol.T, g.T
        c0 = 2 * M_HEADS * d
        ib = grow[c0:c0 + M_HEADS, :] - brow[c0 + M_HEADS:c0 + 2 * M_HEADS, :]
        lane_h = lax.broadcasted_iota(I32, ib.shape, 1)
        rmax = ib
        step = 1
        while step < CHUNK:
            if d == 0:
                prev = jnp.where(lane_h >= step, pltpu.roll(rmax, step, axis=1), -jnp.inf)
            else:
                prev = jnp.where(lane_h < CHUNK - step, pltpu.roll(rmax, CHUNK - step, axis=1), -jnp.inf)
            rmax = jnp.maximum(rmax, prev)
            step *= 2
        rmax_col = jnp.concatenate([rmax, jnp.zeros((CHUNK - M_HEADS, CHUNK), F32)], axis=0).T
        per_dir.append((d, qk, v, g, mask, bcol, ib, rmax_col, out_ref))

    units = [(pd, h) for pd in per_dir for h in range(M_HEADS)]
    stage1 = []
    for (d, qk, v, g, mask, bcol, ib, rmax_col, out_ref), h in units:
        col_i = 2 * M_HEADS * d + h
        col_f = col_i + M_HEADS
        idx = d * M_HEADS + h
        end = CHUNK - 1 if d == 0 else 0
        bc = bcol[:, col_f:col_f + 1]
        ic = g[:, col_i:col_i + 1]
        ibr = ib[h:h + 1, :]
        m = m_scr[idx][0:1, 0:1]
        inter = bc + m
        m_t = jnp.maximum(inter, bc + rmax_col[:, h:h + 1])
        w_inter = jnp.exp(inter - m_t)
        b_end = bc[end:end + 1, :]
        m_new = jnp.maximum(b_end + m, b_end + rmax_col[end:end + 1, h:h + 1])
        decay = jnp.exp(b_end + m - m_new)
        w_col = jnp.exp(b_end - bc + ic - m_new)
        q = qk[:, h * QK_DIM:(h + 1) * QK_DIM]
        k = qk[:, qk_w + h * QK_DIM:qk_w + (h + 1) * QK_DIM]
        decay_mat = jnp.exp(jnp.where(mask, (bc - m_t) + ibr, -jnp.inf))
        sc = lax.dot_general(q, k, dn_nt, preferred_element_type=F32) * decay_mat
        lhs = jnp.concatenate([(q.astype(F32) * w_inter).astype(BF16), sc.astype(BF16)], axis=1)
        kw = (k.astype(F32) * w_col).astype(BF16)
        stage1.append((lhs, kw, m_t, m_new, decay))

    stage2 = []
    for ((d, qk, v, g, mask, bcol, brow, grow, out_ref), h), (lhs, kw, m_t, m_new, decay) in zip(units, stage1):
        idx = d * M_HEADS + h
        vaug = jnp.concatenate([v[:, h * V_DIM:(h + 1) * V_DIM], ones_col], axis=1)
        caug = c_scr[idx][...]
        num = jnp.dot(lhs, jnp.concatenate([caug.astype(BF16), vaug], axis=0), preferred_element_type=F32)
        upd = lax.dot_general(kw, vaug, dn_tn, preferred_element_type=F32)
        stage2.append((num, upd, caug))

    for ((d, qk, v, g, mask, bcol, brow, grow, out_ref), h), (lhs, kw, m_t, m_new, decay), (num, upd, caug) in zip(
            units, stage1, stage2):
        idx = d * M_HEADS + h
        den = num[:, V_DIM:V_DIM + 1]
        hout = num[:, :V_DIM] / jnp.maximum(jnp.abs(den), jnp.exp(-m_t))
        c_scr[idx][...] = decay * caug + upd
        m_scr[idx][...] = jnp.broadcast_to(m_new, m_scr[idx].shape)
        out_ref[:, h * V_DIM:(h + 1) * V_DIM] = hout.astype(out_ref.dtype)


def _mlstm(qk_ctx, qk_lat, z_ctx3, z_lat3, g_ctx3, g_lat3):
    b, tc, _ = qk_ctx.shape
    t = qk_lat.shape[1]
    nc, nl = tc // CHUNK, t // CHUNK
    vw = M_HEADS * V_DIM
    qkw = 2 * M_HEADS * QK_DIM

    f_ctx = lambda i, s: (i, jnp.minimum(s, nc - 1), 0)
    f_lat = lambda i, s: (i, jnp.clip(s - nc, 0, nl - 1), 0)
    b_ctx = lambda i, s: (i, jnp.maximum(nc - 1 - s, 0), 0)
    b_lat = lambda i, s: (i, jnp.clip(nc + nl - 1 - s, 0, nl - 1), 0)
    v_of = lambda f: (lambda i, s: (f(i, s)[0], f(i, s)[1], 1))

    def specs(fc, fl):
        return [pl.BlockSpec((None, CHUNK, qkw), fc), pl.BlockSpec((None, CHUNK, qkw), fl),
                pl.BlockSpec((None, CHUNK, vw), v_of(fc)), pl.BlockSpec((None, CHUNK, vw), v_of(fl)),
                pl.BlockSpec((None, CHUNK, LANES), fc), pl.BlockSpec((None, CHUNK, LANES), fl)]

    kern = functools.partial(_mlstm_kernel, n_ctx_chunks=nc)
    args = (qk_ctx, qk_lat, z_ctx3, z_lat3, g_ctx3, g_lat3)
    return pl.pallas_call(
        kern,
        grid=(b, nc + nl),
        in_specs=specs(f_ctx, f_lat) + specs(b_ctx, b_lat),
        out_specs=[pl.BlockSpec((None, CHUNK, vw), f_lat), pl.BlockSpec((None, CHUNK, vw), b_lat)],
        out_shape=[jax.ShapeDtypeStruct((b, t, vw), BF16), jax.ShapeDtypeStruct((b, t, vw), BF16)],
        scratch_shapes=([pltpu.VMEM((QK_DIM, V_DIM + LANES), F32)] * (2 * M_HEADS)
                        + [pltpu.VMEM((8, LANES), F32)] * (2 * M_HEADS)),
        compiler_params=_cparams(("arbitrary", "arbitrary")),
        name="mlstm",
    )(*args, *args)


MERGE_ROWS = 256


def _merge_kernel(hf_ref, hb_ref, o_ref, u_ref, sv_ref, ga_ref, gb_ref, gm_ref, gs_ref, sw_ref, sb_ref,
                  pa_ref, pb_ref, out_ref):
    tm, d = out_ref.shape
    gw = d // SGU_GROUPS
    for grp in range(tm // MERGE_ROWS):
        r0 = grp * MERGE_ROWS
        rs = slice(r0, r0 + MERGE_ROWS)
        a_parts = []
        for h in range(M_HEADS):
            sl = slice(h * V_DIM, (h + 1) * V_DIM)
            hs = hf_ref[rs, sl].astype(F32) + hb_ref[rs, sl].astype(F32)
            ms = jnp.mean(hs * hs, axis=-1, keepdims=True)
            y = hs * lax.rsqrt(ms + EPS) * gm_ref[:, sl]
            a_parts.append((y * o_ref[rs, sl].astype(F32)).astype(BF16))
        a = jnp.concatenate(a_parts, axis=1)
        sv = sv_ref[rs, :].astype(F32)
        ms = jnp.mean(sv * sv, axis=-1, keepdims=True)
        svn = (sv * lax.rsqrt(ms + EPS) * gs_ref[...]).astype(BF16)
        b_rows = []
        for c in range(MERGE_ROWS // CHUNK):
            cr = slice(c * CHUNK, (c + 1) * CHUNK)
            b_parts = []
            for gi in range(SGU_GROUPS):
                cs = slice(gi * gw, (gi + 1) * gw)
                zz = jnp.dot(sw_ref[gi], svn[cr, cs], preferred_element_type=F32) + sb_ref[:, gi:gi + 1]
                b_parts.append((u_ref[r0 + c * CHUNK:r0 + (c + 1) * CHUNK, cs].astype(F32) * zz).astype(BF16))
            b_rows.append(jnp.concatenate(b_parts, axis=1))
        b = jnp.concatenate(b_rows, axis=0)
        ya = jnp.dot(a, pa_ref[...], preferred_element_type=F32)
        yb = jnp.dot(b, pb_ref[...], preferred_element_type=F32)
        out_ref[rs, :] = (ga_ref[rs, :].astype(F32) * ya + gb_ref[rs, :].astype(F32) * yb).astype(BF16)


def _merge(hf, hb, z, gm, gs, sgu_w, sgu_bt, pa, pb):
    r, d = hf.shape
    tm = 512
    row = lambda k: (lambda i: (i, k))
    const = lambda shape: pl.BlockSpec(shape, lambda i: (0,) * len(shape), pipeline_mode=pl.Buffered(1))
    return pl.pallas_call(
        _merge_kernel,
        grid=(r // tm,),
        in_specs=[pl.BlockSpec((tm, d), row(0)), pl.BlockSpec((tm, d), row(0)),
                  pl.BlockSpec((tm, d), row(2)), pl.BlockSpec((tm, d), row(3)), pl.BlockSpec((tm, d), row(4)),
                  pl.BlockSpec((tm, d), row(5)), pl.BlockSpec((tm, d), row(6)),
                  const((1, d)), const((1, d)), const(sgu_w.shape), const(sgu_bt.shape),
                  const((d, d)), const((d, d))],
        out_specs=pl.BlockSpec((tm, d), row(0)),
        out_shape=jax.ShapeDtypeStruct((r, d), BF16),
        compiler_params=_cparams(("arbitrary",)),
        name="merge",
    )(hf, hb, z, z, z, z, z, gm, gs, sgu_w, sgu_bt, pa, pb)


def _outproj_kernel(m_ref, w_ref, x_ref, gate_ref, shift_ref, scale_ref, g_ref, rw_ref, rb_ref,
                    x1_ref, h2_ref, lg_ref):
    r1, r2, _ = _split3(rw_ref[...])
    ne = lg_ref.shape[0]
    for grp in range(x_ref.shape[0] // MERGE_ROWS):
        r0 = grp * MERGE_ROWS
        rs = slice(r0, r0 + MERGE_ROWS)
        out = jnp.dot(m_ref[rs, :], w_ref[...], preferred_element_type=F32)
        x1 = x_ref[rs, :] + gate_ref[...] * out
        x1_ref[rs, :] = x1
        ms = jnp.mean(x1 * x1, axis=-1, keepdims=True)
        h2 = (x1 * lax.rsqrt(ms + EPS) * g_ref[...]) * (1.0 + scale_ref[...]) + shift_ref[...]
        hi, lo, _ = _split3(h2)
        lg = (jnp.dot(hi, r1, preferred_element_type=F32) + jnp.dot(lo, r1, preferred_element_type=F32)
              + jnp.dot(hi, r2, preferred_element_type=F32)) + rb_ref[...]
        lg_ref[:, rs] = lg.T[:ne, :]
        _store_token_major(h2_ref, r0, h2)


def _outproj(merged, w_out, x2d, mod3, mod_row_of_tile, norm_g, rw, rb, tm):
    r, d = x2d.shape
    ne = rw.shape[1]
    rw_t = jnp.pad(rw, ((0, 0), (0, LANES - ne)))
    rb_col = jnp.pad(rb.reshape(1, ne), ((0, 0), (0, LANES - ne)))
    modspec = lambda k: pl.BlockSpec((None, 1, d), lambda i: (mod_row_of_tile(i), 0, k))
    return pl.pallas_call(
        _outproj_kernel,
        grid=(r // tm,),
        in_specs=[pl.BlockSpec((tm, d), lambda i: (i, 0)),
                  pl.BlockSpec((d, d), lambda i: (0, 0), pipeline_mode=pl.Buffered(1)),
                  pl.BlockSpec((tm, d), lambda i: (i, 0)),
                  modspec(2), modspec(3), modspec(4),
                  pl.BlockSpec((1, d), lambda i: (0, 0)),
                  pl.BlockSpec((d, LANES), lambda i: (0, 0)),
                  pl.BlockSpec((1, LANES), lambda i: (0, 0))],
        out_specs=[pl.BlockSpec((tm, d), lambda i: (i, 0)),
                   pl.BlockSpec((tm * TOKEN_ROWS, LANES), lambda i: (i, 0)),
                   pl.BlockSpec((ne, tm), lambda i: (0, i))],
        out_shape=[jax.ShapeDtypeStruct((r, d), F32),
                   jax.ShapeDtypeStruct((r * TOKEN_ROWS, LANES), F32),
                   jax.ShapeDtypeStruct((ne, r), F32)],
        compiler_params=_cparams(("arbitrary",)),
        name="outproj",
    )(merged, w_out, x2d, mod3, mod3, mod3, norm_g, rw_t, rb_col)


def _route_kernel(lg_ref, eid_ref, gate_ref, rank_ref, cnt_ref, carry_scr):
    i = pl.program_id(0)

    @pl.when(i == 0)
    def _():
        carry_scr[...] = jnp.zeros_like(carry_scr)

    l = lg_ref[...]
    ne, tm = l.shape
    e_iota = lax.broadcasted_iota(I32, (ne, tm), 0)
    vals, onehots = [], []
    for k in range(TOP_K):
        mx = jnp.max(l, axis=0, keepdims=True)
        idx = jnp.min(jnp.where(l == mx, e_iota, ne), axis=0, keepdims=True)
        oh = e_iota == idx
        l = jnp.where(oh, -jnp.inf, l)
        vals.append(mx)
        onehots.append(oh)
        eid_ref[k:k + 1, :] = idx
    ex = [jnp.exp(vk - vals[0]) for vk in vals]
    tot = ex[0] + ex[1] + ex[2] + ex[3]
    for k in range(TOP_K):
        gate_ref[k:k + 1, :] = ex[k] / tot

    oh_all = jnp.zeros((ne, tm), F32)
    for oh in onehots:
        oh_all = oh_all + jnp.where(oh, 1.0, 0.0)
    oh_all = oh_all.astype(BF16)
    r_i = lax.broadcasted_iota(I32, (LANES, LANES), 0)
    c_i = lax.broadcasted_iota(I32, (LANES, LANES), 1)
    tri_excl = jnp.where(r_i < c_i, 1.0, 0.0).astype(BF16)
    ones_m = jnp.ones((LANES, LANES), BF16)
    carry = carry_scr[...]
    for blk in range(tm // LANES):
        sl = slice(blk * LANES, (blk + 1) * LANES)
        ohb = oh_all[:, sl]
        cum = jnp.dot(ohb, tri_excl, preferred_element_type=F32) + carry
        for k in range(TOP_K):
            rk = jnp.sum(jnp.where(onehots[k][:, sl], cum, 0.0), axis=0, keepdims=True)
            rank_ref[k:k + 1, sl] = rk.astype(I32)
        carry = carry + jnp.dot(ohb, ones_m, preferred_element_type=F32)
    carry_scr[...] = carry
    cnt_ref[...] = carry


def _route(logits_t):
    ne, r = logits_t.shape
    tm = 1024
    return pl.pallas_call(
        _route_kernel,
        grid=(r // tm,),
        in_specs=[pl.BlockSpec((ne, tm), lambda i: (0, i))],
        out_specs=[pl.BlockSpec((TOP_K, tm), lambda i: (0, i)),
                   pl.BlockSpec((TOP_K, tm), lambda i: (0, i)),
                   pl.BlockSpec((TOP_K, tm), lambda i: (0, i)),
                   pl.BlockSpec((ne, LANES), lambda i: (0, 0))],
        out_shape=[jax.ShapeDtypeStruct((TOP_K, r), I32),
                   jax.ShapeDtypeStruct((TOP_K, r), F32),
                   jax.ShapeDtypeStruct((TOP_K, r), I32),
                   jax.ShapeDtypeStruct((ne, LANES), F32)],
        scratch_shapes=[pltpu.VMEM((ne, LANES), F32)],
        compiler_params=_cparams(("arbitrary",)),
        name="route",
    )(logits_t)


def _slot_kernel(eid_ref, rank_ref, cnt_ref, slot_ref):
    ne = cnt_ref.shape[0]
    nblk = jnp.floor((cnt_ref[...] + (MOE_TILE - 1.0)) * (1.0 / MOE_TILE))
    r_i = lax.broadcasted_iota(I32, (ne, ne), 0)
    c_i = lax.broadcasted_iota(I32, (ne, ne), 1)
    tri = jnp.where(c_i < r_i, 1.0, 0.0).astype(BF16)
    start = jnp.dot(tri, nblk.astype(BF16), preferred_element_type=F32) * MOE_TILE
    eid = eid_ref[...]
    acc = rank_ref[...]
    for e in range(ne):
        acc = acc + jnp.where(eid == e, start[e:e + 1, 0:1].astype(I32), 0)
    slot_ref[...] = acc


def _slots(eid, rank, cnt):
    return pl.pallas_call(
        _slot_kernel,
        out_shape=jax.ShapeDtypeStruct(eid.shape, I32),
        compiler_params=pltpu.CompilerParams(vmem_limit_bytes=V7X_VMEM_LIMIT_BYTES),
        name="slots",
    )(eid, rank, cnt)


def _dispatch_copy(slot_ref, h_ref, xs_ref, sem, k, t, base, n_tok):
    d0 = pl.multiple_of(slot_ref[k * n_tok + base + t] * TOKEN_ROWS, TOKEN_ROWS)
    s0 = pl.multiple_of(t * TOKEN_ROWS, TOKEN_ROWS)
    return pltpu.make_async_copy(h_ref.at[pl.ds(s0, TOKEN_ROWS), :], xs_ref.at[pl.ds(d0, TOKEN_ROWS), :], sem)


def _zero_copy(zbuf, xs_ref, zsem, row, n_rows):
    d0 = pl.multiple_of(row * TOKEN_ROWS, TOKEN_ROWS)
    return pltpu.make_async_copy(zbuf.at[pl.ds(0, n_rows * TOKEN_ROWS), :],
                                 xs_ref.at[pl.ds(d0, n_rows * TOKEN_ROWS), :], zsem)


def _zero_fill(start_ref, cnt_ref, zbuf, xs_ref, zsem, p_rows, begin):
    ne = start_ref.shape[0]

    def op(cp):
        cp.start() if begin else cp.wait()

    def per_expert(e, used):
        n = cnt_ref[e]
        s0 = start_ref[e]
        c_step = (n + MOE_STEP - 1) // MOE_STEP * MOE_STEP
        c_tile = (n + MOE_TILE - 1) // MOE_TILE * MOE_TILE

        def one_row(r, c):
            op(_zero_copy(zbuf, xs_ref, zsem, s0 + r, 1))
            return c

        def one_unit(u, c):
            op(_zero_copy(zbuf, xs_ref, zsem, s0 + u * MOE_STEP, MOE_STEP))
            return c

        lax.fori_loop(n, c_step, one_row, 0)
        lax.fori_loop(c_step // MOE_STEP, c_tile // MOE_STEP, one_unit, 0)
        return s0 + c_tile

    used = lax.fori_loop(0, ne, per_expert, 0)

    def trailing(u, c):
        op(_zero_copy(zbuf, xs_ref, zsem, u * MOE_STEP, MOE_STEP))
        return c

    lax.fori_loop(used // MOE_STEP, p_rows // MOE_STEP, trailing, 0)


def _dispatch_kernel(slot_ref, start_ref, cnt_ref, h_ref, xs_ref, zbuf, sem, zsem, *, tm, n_tok, p_rows):
    i = pl.program_id(0)
    base = i * tm

    @pl.when(i == 0)
    def _():
        zbuf[...] = jnp.zeros_like(zbuf)
        _zero_fill(start_ref, cnt_ref, zbuf, xs_ref, zsem, p_rows, True)

    def start(t, carry):
        for k in range(TOP_K):
            _dispatch_copy(slot_ref, h_ref, xs_ref, sem, k, t, base, n_tok).start()
        return carry

    def wait(t, carry):
        for k in range(TOP_K):
            _dispatch_copy(slot_ref, h_ref, xs_ref, sem, k, t, base, n_tok).wait()
        return carry

    lax.fori_loop(0, tm, start, 0, unroll=4)
    lax.fori_loop(0, tm, wait, 0, unroll=4)

    @pl.when(i == 0)
    def _():
        _zero_fill(start_ref, cnt_ref, zbuf, xs_ref, zsem, p_rows, False)


def _dispatch(h2p, slot, start_pad, counts, p_rows):
    n_tok = h2p.shape[0] // TOKEN_ROWS
    tm = 512
    grid_spec = pltpu.PrefetchScalarGridSpec(
        num_scalar_prefetch=3,
        grid=(n_tok // tm,),
        in_specs=[pl.BlockSpec((tm * TOKEN_ROWS, LANES), lambda i, *_: (i, 0))],
        out_specs=pl.BlockSpec(memory_space=pl.ANY),
        scratch_shapes=[pltpu.VMEM((MOE_STEP * TOKEN_ROWS, LANES), h2p.dtype),
                        pltpu.SemaphoreType.DMA(()), pltpu.SemaphoreType.DMA(())],
    )
    return pl.pallas_call(
        functools.partial(_dispatch_kernel, tm=tm, n_tok=n_tok, p_rows=p_rows),
        grid_spec=grid_spec,
        out_shape=jax.ShapeDtypeStruct((p_rows * TOKEN_ROWS, LANES), h2p.dtype),
        compiler_params=_cparams(("arbitrary",)),
        name="dispatch",
    )(slot, start_pad, counts, h2p)


def _weight_copies(w_hbm, wbuf, sems, e, col_starts, slot):
    tn = wbuf.shape[-1]
    aligned = lambda c: c if isinstance(c, int) else pl.multiple_of(c, LANES)
    return [pltpu.make_async_copy(w_hbm.at[e, :, pl.ds(aligned(c), tn)], wbuf.at[slot, i], sems.at[slot])
            for i, c in enumerate(col_starts)]


def _advance_weight_pipeline(w_hbm, wbuf, sems, sched, j, nj, it, col_starts_of):
    e_ref, run_ref, last_ref, nxt_ref, nruns_ref = sched
    slot = (j * nruns_ref[0] + run_ref[it]) % 2

    @pl.when((j == 0) & (it == 0))
    def _():
        for cp in _weight_copies(w_hbm, wbuf, sems, e_ref[it], col_starts_of(j), slot):
            cp.start()

    for cp in _weight_copies(w_hbm, wbuf, sems, e_ref[it], col_starts_of(j), slot):
        cp.wait()
    is_last = last_ref[it] == 1

    @pl.when(jnp.logical_not(is_last))
    def _():
        for cp in _weight_copies(w_hbm, wbuf, sems, nxt_ref[it], col_starts_of(j), 1 - slot):
            cp.start()

    @pl.when(is_last & (j + 1 < nj))
    def _():
        for cp in _weight_copies(w_hbm, wbuf, sems, nxt_ref[it], col_starts_of(j + 1), 1 - slot):
            cp.start()

    return slot


def _moe_up_kernel(blk_ref, oblk_ref, e_ref, nv_ref, new_ref, run_ref, last_ref, nxt_ref, nruns_ref,
                   xs_ref, w_hbm, bg_ref, bl_ref, act_ref, wbuf, sems, wg_bf, wl_bf):
    j = pl.program_id(0)
    nj = pl.num_programs(0)
    it = pl.program_id(1)
    tn = wg_bf.shape[1]

    @pl.when(new_ref[it] == 1)
    def _():
        slot = _advance_weight_pipeline(w_hbm, wbuf, sems, (e_ref, run_ref, last_ref, nxt_ref, nruns_ref), j, nj, it,
                                        lambda jj: (jj * tn, (nj + jj) * tn))
        wg_bf[...] = wbuf[slot, 0].astype(BF16)
        wl_bf[...] = wbuf[slot, 1].astype(BF16)

    def rows_chunk(r0, n):
        x = _load_token_major(xs_ref, r0, n).astype(BF16)
        hg = jnp.dot(x, wg_bf[...], preferred_element_type=F32) + bg_ref[...]
        hl = jnp.dot(x, wl_bf[...], preferred_element_type=F32) + bl_ref[...]
        glu = jnp.minimum(hg, SWIGLU_LIMIT)
        lin = jnp.clip(hl, -SWIGLU_LIMIT, SWIGLU_LIMIT)
        act_ref[r0:r0 + n, :] = ((0.5 * glu) * (1.0 + jnp.tanh((0.5 * SWIGLU_ALPHA) * glu)) * (lin + 1.0)).astype(BF16)

    def rows_zero(r0, n):
        act_ref[r0:r0 + n, :] = jnp.zeros((n, act_ref.shape[1]), BF16)

    _moe_row_variants(nv_ref[it], rows_chunk, rows_zero)


def _moe_row_variants(nv, rows_chunk, rows_zero):
    for units in range(MOE_TILE // MOE_STEP + 1):
        rows = units * MOE_STEP

        @pl.when((nv > rows - MOE_STEP) & (nv <= rows) if units else nv <= 0)
        def _():
            r0 = 0
            while r0 < rows:
                n = min(MOE_HALF, rows - r0)
                rows_chunk(r0, n)
                r0 += n
            if rows < MOE_TILE:
                rows_zero(rows, MOE_TILE - rows)


def _moe_down_kernel(blk_ref, oblk_ref, e_ref, nv_ref, new_ref, run_ref, last_ref, nxt_ref, nruns_ref,
                     a_ref, w_hbm, b_ref, y_ref, wbuf, sems, w_bf):
    it = pl.program_id(0)

    @pl.when(new_ref[it] == 1)
    def _():
        slot = _advance_weight_pipeline(w_hbm, wbuf, sems, (e_ref, run_ref, last_ref, nxt_ref, nruns_ref), 0, 1, it,
                                        lambda jj: (0,))
        w_bf[...] = wbuf[slot, 0].astype(BF16)

    def rows_chunk(r0, n):
        y = jnp.dot(a_ref[r0:r0 + n, :], w_bf[...], preferred_element_type=F32) + b_ref[...]
        _store_token_major(y_ref, r0, y)

    def rows_zero(r0, n):
        y_ref[r0 * TOKEN_ROWS:(r0 + n) * TOKEN_ROWS, :] = jnp.zeros((n * TOKEN_ROWS, LANES), F32)

    _moe_row_variants(nv_ref[it], rows_chunk, rows_zero)


def _moe_up(items, xs, w1, b1):
    n_items = items[0].shape[0]
    ne, d, two_f = w1.shape
    f = two_f // 2
    nj = f // MOE_TN
    p = xs.shape[0] // TOKEN_ROWS
    b1r = b1.reshape(ne, 1, two_f)
    grid_spec = pltpu.PrefetchScalarGridSpec(
        num_scalar_prefetch=len(items),
        grid=(nj, n_items),
        in_specs=[pl.BlockSpec((MOE_TILE * TOKEN_ROWS, LANES), lambda j, it, blk, *_: (blk[it], 0)),
                  pl.BlockSpec(memory_space=pl.ANY),
                  pl.BlockSpec((None, 1, MOE_TN), lambda j, it, blk, oblk, e, *_: (e[it], 0, j)),
                  pl.BlockSpec((None, 1, MOE_TN), lambda j, it, blk, oblk, e, *_: (e[it], 0, nj + j))],
        out_specs=pl.BlockSpec((MOE_TILE, MOE_TN), lambda j, it, blk, oblk, *_: (oblk[it], j)),
        scratch_shapes=[pltpu.VMEM((2, 2, d, MOE_TN), F32), pltpu.SemaphoreType.DMA((2,)),
                        pltpu.VMEM((d, MOE_TN), BF16), pltpu.VMEM((d, MOE_TN), BF16)],
    )
    return pl.pallas_call(
        _moe_up_kernel,
        grid_spec=grid_spec,
        out_shape=jax.ShapeDtypeStruct((p, f), BF16),
        compiler_params=_cparams(("arbitrary", "arbitrary")),
        name="moe_up",
    )(*items, xs, w1, b1r, b1r)


def _moe_down(items, act, w2, b2):
    n_items = items[0].shape[0]
    ne, f, d = w2.shape
    p = act.shape[0]
    b2r = b2.reshape(ne, 1, d)
    grid_spec = pltpu.PrefetchScalarGridSpec(
        num_scalar_prefetch=len(items),
        grid=(n_items,),
        in_specs=[pl.BlockSpec((MOE_TILE, f), lambda it, blk, *_: (blk[it], 0)),
                  pl.BlockSpec(memory_space=pl.ANY),
                  pl.BlockSpec((None, 1, d), lambda it, blk, oblk, e, *_: (e[it], 0, 0))],
        out_specs=pl.BlockSpec((MOE_TILE * TOKEN_ROWS, LANES), lambda it, blk, oblk, *_: (oblk[it], 0)),
        scratch_shapes=[pltpu.VMEM((2, 1, f, d), F32), pltpu.SemaphoreType.DMA((2,)), pltpu.VMEM((f, d), BF16)],
    )
    return pl.pallas_call(
        _moe_down_kernel,
        grid_spec=grid_spec,
        out_shape=jax.ShapeDtypeStruct((p * TOKEN_ROWS, LANES), F32),
        compiler_params=_cparams(("arbitrary",)),
        name="moe_down",
    )(*items, act, w2, b2r)


FINAL_ROWS = 64


def _combine_copy(slot_ref, y_ref, buf, sems, k, t, tile, par, tm, n_tok):
    s0 = pl.multiple_of(slot_ref[k * n_tok + tile * tm + t] * TOKEN_ROWS, TOKEN_ROWS)
    d0 = pl.multiple_of(t * TOKEN_ROWS, TOKEN_ROWS)
    return pltpu.make_async_copy(y_ref.at[pl.ds(s0, TOKEN_ROWS), :], buf.at[par, k, pl.ds(d0, TOKEN_ROWS), :],
                                 sems.at[par])


def _final_kernel(slot_ref, y_ref, gates_ref, x1_ref, gate_ref, g_ref, o_ref, buf, sems, *, n_tok):
    tm, d = x1_ref.shape
    i = pl.program_id(0)
    par = i % 2

    def gather(tile, par_, start):
        def body(t, carry):
            for k in range(TOP_K):
                cp = _combine_copy(slot_ref, y_ref, buf, sems, k, t, tile, par_, tm, n_tok)
                cp.start() if start else cp.wait()
            return carry
        lax.fori_loop(0, tm, body, 0, unroll=4)

    @pl.when(i == 0)
    def _():
        gather(0, 0, True)

    @pl.when(i + 1 < pl.num_programs(0))
    def _():
        gather(i + 1, 1 - par, True)

    gather(i, par, False)

    for grp in range(tm // FINAL_ROWS):
        r0 = grp * FINAL_ROWS
        rs = slice(r0, r0 + FINAL_ROWS)
        gk = [jnp.broadcast_to(gates_ref[rs, k:k + 1], (FINAL_ROWS, LANES)) for k in range(TOP_K)]
        sq = jnp.zeros((FINAL_ROWS, LANES), F32)
        for s in range(TOKEN_ROWS):
            cs = slice(s * LANES, (s + 1) * LANES)
            rows = pl.ds(r0 * TOKEN_ROWS + s, FINAL_ROWS, stride=TOKEN_ROWS)
            acc = gk[0] * buf[par, 0, rows, :]
            for k in range(1, TOP_K):
                acc = acc + gk[k] * buf[par, k, rows, :]
            xf = x1_ref[rs, cs] + gate_ref[:, cs] * acc
            o_ref[rs, cs] = xf
            sq = sq + xf * xf
        scale = lax.rsqrt(jnp.sum(sq, axis=-1, keepdims=True) * (1.0 / d) + EPS)
        for s in range(TOKEN_ROWS):
            cs = slice(s * LANES, (s + 1) * LANES)
            o_ref[rs, cs] = o_ref[rs, cs] * scale * g_ref[:, cs]


def _final(slot, y, gates_tk, x1, mod3, mod_row_of_tile, final_g, tm):
    r, d = x1.shape
    grid_spec = pltpu.PrefetchScalarGridSpec(
        num_scalar_prefetch=1,
        grid=(r // tm,),
        in_specs=[pl.BlockSpec(memory_space=pl.ANY),
                  pl.BlockSpec((tm, TOP_K), lambda i, slot: (i, 0)),
                  pl.BlockSpec((tm, d), lambda i, slot: (i, 0)),
                  pl.BlockSpec((None, 1, d), lambda i, slot: (mod_row_of_tile(i), 0, 5)),
                  pl.BlockSpec((1, d), lambda i, slot: (0, 0))],
        out_specs=pl.BlockSpec((tm, d), lambda i, slot: (i, 0)),
        scratch_shapes=[pltpu.VMEM((2, TOP_K, tm * TOKEN_ROWS, LANES), F32), pltpu.SemaphoreType.DMA((2,))],
    )
    return pl.pallas_call(
        functools.partial(_final_kernel, n_tok=r),
        grid_spec=grid_spec,
        out_shape=jax.ShapeDtypeStruct((r, d), F32),
        compiler_params=_cparams(("arbitrary",)),
        name="final",
    )(slot, y, gates_tk, x1, mod3, final_g)


def _moe_schedule(counts, n_items):
    ne = counts.shape[0]
    padded = (counts + MOE_TILE - 1) // MOE_TILE * MOE_TILE
    pad_end = jnp.cumsum(padded)
    start_pad = pad_end - padded
    n_real = pad_end[-1] // MOE_TILE
    b = jnp.arange(n_items, dtype=I32)
    blk = jnp.minimum(b, jnp.maximum(n_real - 1, 0))
    blk_start = blk * MOE_TILE
    e_of = jnp.minimum(jnp.sum((pad_end[None, :] <= blk_start[:, None]).astype(I32), axis=1), ne - 1)
    nv = jnp.clip(counts[e_of] - (blk_start - start_pad[e_of]), 0, MOE_TILE)
    nv = jnp.where(b < n_real, nv, 0)
    e_of = e_of.astype(I32)
    prev_e = jnp.concatenate([jnp.full((1,), -1, I32), e_of[:-1]])
    new = ((e_of != prev_e) | (b == 0)).astype(I32)
    run = jnp.cumsum(new) - 1
    n_runs = run[-1] + 1
    last = (run == n_runs - 1).astype(I32)
    first_pos = jnp.where(new == 1, b, n_items)
    nxt_pos = jnp.concatenate([lax.cummin(first_pos, reverse=True)[1:], jnp.full((1,), n_items, I32)])
    nxt = jnp.where(nxt_pos < n_items, e_of[jnp.minimum(nxt_pos, n_items - 1)], e_of[0])
    return start_pad.astype(I32), (blk.astype(I32), b, e_of, nv.astype(I32), new, run.astype(I32), last,
                                   nxt.astype(I32), n_runs.reshape(1).astype(I32))


def kernel(x, c, ctx, c_ctx, mod_w, mod_b, norm1_g, norm2_g, w_in, gate_b, conv_w, conv_b, mlstm_norm_g,
           sgu_norm_g, sgu_w, sgu_b, proj_a, proj_b, w_out, router_w, router_b, exp_w1, exp_b1, exp_w2,
           exp_b2, final_g):
    bsz, t, d = x.shape
    tc = ctx.shape[1]
    depth = mod_w.shape[0]
    assert depth == 1, "single-layer block"
    assert d == TOKEN_ROWS * LANES, "token-major rows hold TOKEN_ROWS*128 features"
    l = 0
    r = bsz * t
    qkw = 2 * M_HEADS * QK_DIM
    vw = M_HEADS * V_DIM
    off_gates = qkw + vw
    n_gates = 4 * M_HEADS
    off_o = off_gates + n_gates

    cc = jnp.zeros((8, d), F32).at[:bsz].set(c).at[bsz].set(c_ctx)
    mod3 = _mod(cc, mod_w[l], mod_b[l]).reshape(8, 1, N_MOD * d)

    in_cols = w_in.shape[2]
    assert off_gates % INPROJ_TN == 0 and (in_cols - off_o) % INPROJ_TN == 0
    n_aligned = off_gates // INPROJ_TN
    n_tiles = n_aligned + (in_cols - off_o) // INPROJ_TN
    w_in_t = jnp.swapaxes(w_in, 1, 2)
    w_gates = jnp.pad(w_in_t[l, off_gates:off_o, :].T, ((0, 0), (0, LANES - n_gates)))
    gb = jnp.pad(gate_b[l].reshape(1, n_gates), ((0, 0), (0, LANES - n_gates)))
    g1 = norm1_g[l].reshape(1, d)

    tm_in = 512
    h_lat, g_lat = _norm(x.reshape(r, d), mod3, lambda i: i // (t // tm_in), g1, w_gates, gb, tm_in)
    h_ctx, g_ctx = _norm(ctx.reshape(bsz * tc, d), mod3, lambda i: bsz, g1, w_gates, gb, tc)
    z_lat = _inproj(h_lat, w_in_t, l, n_aligned, n_gates, n_tiles, 2048)
    z_ctx = _inproj(h_ctx, w_in_t, l, n_aligned, n_gates, n_aligned, bsz * tc)

    z_lat3 = z_lat.reshape(bsz, t, z_lat.shape[1])
    z_ctx3 = z_ctx.reshape(bsz, tc, z_ctx.shape[1])
    qk_lat = _conv(z_lat3, conv_w[l], conv_b[l])
    qk_ctx = _conv(z_ctx3, conv_w[l], conv_b[l])

    hf, hb = _mlstm(qk_ctx, qk_lat, z_ctx3, z_lat3, g_ctx.reshape(bsz, tc, LANES), g_lat.reshape(bsz, t, LANES))

    merged = _merge(hf.reshape(r, vw), hb.reshape(r, vw), z_lat,
                    mlstm_norm_g[l].reshape(1, vw), sgu_norm_g[l].reshape(1, -1),
                    sgu_w[l].astype(BF16), sgu_b[l].T, proj_a[l].astype(BF16), proj_b[l].astype(BF16))

    tm_out = 512
    x1, h2p, logits_t = _outproj(merged, w_out[l].astype(BF16), x.reshape(r, d), mod3,
                                 lambda i: i // (t // tm_out), norm2_g[l].reshape(1, d),
                                 router_w[l], router_b[l], tm_out)
    tm_fin = 256

    eid, gates, rank, cnt = _route(logits_t)
    counts = cnt[:, 0].astype(I32)
    n_items = (r * TOP_K) // MOE_TILE + N_EXPERTS
    start_pad, items = _moe_schedule(counts, n_items)
    slot = _slots(eid, rank, cnt).reshape(-1)
    xs = _dispatch(h2p, slot, start_pad, counts, n_items * MOE_TILE)
    act = _moe_up(items, xs, exp_w1[l], exp_b1[l])
    y = _moe_down(items, act, exp_w2[l], exp_b2[l])
    out = _final(slot, y, gates.T, x1, mod3, lambda i: i // (t // tm_fin), final_g.reshape(1, d), tm_fin)
    return out.reshape(bsz, t, d)
```

```python
import functools

import jax
import jax.numpy as jnp
from jax import lax
from jax.experimental import pallas as pl
from jax.experimental.pallas import tpu as pltpu

F32 = jnp.float32
BF16 = jnp.bfloat16
I32 = jnp.int32
U32 = jnp.uint32

EPS = 1e-6
M_HEADS = 8
QK_DIM = 128
V_DIM = 256
CONV_W = 5
CHUNK = 128
SGU_GROUPS = 8
N_EXPERTS = 32
TOP_K = 4
SWIGLU_ALPHA = 1.702
SWIGLU_LIMIT = 7.0
N_MOD = 6

V7X_VMEM_LIMIT_BYTES = 56 * 1024 * 1024
LANES = 128
MOE_TILE = 512
MOE_HALF = 256
MOE_STEP = 128
MOE_TN = 1024


def _cparams(sem):
    return pltpu.CompilerParams(dimension_semantics=sem, vmem_limit_bytes=V7X_VMEM_LIMIT_BYTES)


def _sigmoid(x):
    return 1.0 / (1.0 + jnp.exp(-x))


def _gelu_tanh(x):
    return 0.5 * x * (1.0 + jnp.tanh(0.7978845608028654 * (x + 0.044715 * (x * x * x))))


def _split3(a):
    a1 = a.astype(BF16)
    r1 = a - a1.astype(F32)
    a2 = r1.astype(BF16)
    a3 = (r1 - a2.astype(F32)).astype(BF16)
    return a1, a2, a3


TOKEN_ROWS = 16


def _store_token_major(ref, row0, vals):
    n = vals.shape[0]
    for s in range(TOKEN_ROWS):
        ref[pl.ds(row0 * TOKEN_ROWS + s, n, stride=TOKEN_ROWS), :] = vals[:, s * LANES:(s + 1) * LANES]


def _load_token_major(ref, row0, n, lead=()):
    cols = [ref[(*lead, pl.ds(row0 * TOKEN_ROWS + s, n, stride=TOKEN_ROWS), slice(None))] for s in range(TOKEN_ROWS)]
    return jnp.concatenate(cols, axis=1)


def _mod_kernel(s_ref, w_ref, b_ref, o_ref):
    s = s_ref[...]
    s = s * _sigmoid(s)
    o_ref[...] = jnp.dot(s.astype(BF16), w_ref[...].astype(BF16), preferred_element_type=F32) + b_ref[...]


def _mod(cc, mod_w, mod_b):
    d, n = mod_w.shape
    tn = 1024
    return pl.pallas_call(
        _mod_kernel,
        grid=(n // tn,),
        in_specs=[pl.BlockSpec((8, d), lambda j: (0, 0)),
                  pl.BlockSpec((d, tn), lambda j: (0, j)),
                  pl.BlockSpec((1, tn), lambda j: (0, j))],
        out_specs=pl.BlockSpec((8, tn), lambda j: (0, j)),
        out_shape=jax.ShapeDtypeStruct((8, n), F32),
        compiler_params=_cparams(("arbitrary",)),
        name="mod",
    )(cc, mod_w, mod_b.reshape(1, n))


def _norm_kernel(x_ref, shift_ref, scale_ref, g_ref, wg_ref, gb_ref, h_ref, gates_ref):
    x = x_ref[...]
    ms = jnp.mean(x * x, axis=-1, keepdims=True)
    h = (x * lax.rsqrt(ms + EPS) * g_ref[...]) * (1.0 + scale_ref[...]) + shift_ref[...]
    h1, h2, _ = _split3(h)
    h_ref[...] = h1
    w1, w2, _ = _split3(wg_ref[...])
    gates_ref[...] = (jnp.dot(h1, w1, preferred_element_type=F32)
                      + jnp.dot(h1, w2, preferred_element_type=F32)
                      + jnp.dot(h2, w1, preferred_element_type=F32)) + gb_ref[...]


def _norm(x2d, mod3, mod_row_of_tile, norm_g, w_gates, gate_b, tm):
    r, d = x2d.shape
    return pl.pallas_call(
        _norm_kernel,
        grid=(r // tm,),
        in_specs=[pl.BlockSpec((tm, d), lambda i: (i, 0)),
                  pl.BlockSpec((None, 1, d), lambda i: (mod_row_of_tile(i), 0, 0)),
                  pl.BlockSpec((None, 1, d), lambda i: (mod_row_of_tile(i), 0, 1)),
                  pl.BlockSpec((1, d), lambda i: (0, 0)),
                  pl.BlockSpec((d, LANES), lambda i: (0, 0)),
                  pl.BlockSpec((1, LANES), lambda i: (0, 0))],
        out_specs=[pl.BlockSpec((tm, d), lambda i: (i, 0)),
                   pl.BlockSpec((tm, LANES), lambda i: (i, 0))],
        out_shape=[jax.ShapeDtypeStruct((r, d), BF16),
                   jax.ShapeDtypeStruct((r, LANES), F32)],
        compiler_params=_cparams(("arbitrary",)),
        name="norm",
    )(x2d, mod3, mod3, norm_g, w_gates, gate_b)


INPROJ_TN = 1024
INPROJ_ROWS = 256


def _inproj_kernel(h_ref, wt_ref, z_ref, w_bf, *, tiles_per_seg):
    j = pl.program_id(0)
    i = pl.program_id(1)

    @pl.when(i == 0)
    def _():
        w_bf[...] = wt_ref[...].T.astype(BF16)

    seg = j // tiles_per_seg
    is_raw = seg < 2
    is_gelu = (seg == 3) | (seg == 4)
    c0 = 0.7978845608028654
    p0 = jnp.where(is_gelu, c0, 0.5).astype(F32)
    p1 = jnp.where(is_gelu, c0 * 0.044715, 0.0).astype(F32)
    q0 = jnp.where(is_gelu, 0.0, 1.0).astype(F32)
    q1 = jnp.where(is_gelu, 1.0, 0.0).astype(F32)
    w = w_bf[...]
    for c in range(h_ref.shape[0] // INPROJ_ROWS):
        rs = slice(c * INPROJ_ROWS, (c + 1) * INPROJ_ROWS)
        z = jnp.dot(h_ref[rs, :], w, preferred_element_type=F32)
        act = (0.5 * (q0 + q1 * z)) * (1.0 + jnp.tanh(z * (p0 + p1 * (z * z))))
        z_ref[rs, :] = jnp.where(is_raw, z, act).astype(BF16)


def _inproj(h, w_in_t, layer, n_aligned, shift, n_tiles, tm):
    r, d = h.shape
    tn = INPROJ_TN
    kern = functools.partial(_inproj_kernel, tiles_per_seg=d // tn)
    assert shift % 8 == 0
    row0 = lambda j: pl.multiple_of(j * tn + jnp.where(j >= n_aligned, shift, 0), 8)
    return pl.pallas_call(
        kern,
        grid=(n_tiles, r // tm),
        in_specs=[pl.BlockSpec((tm, d), lambda j, i: (i, 0)),
                  pl.BlockSpec((None, pl.Element(tn), pl.Element(d)), lambda j, i: (layer, row0(j), 0))],
        out_specs=pl.BlockSpec((tm, tn), lambda j, i: (i, j)),
        out_shape=jax.ShapeDtypeStruct((r, n_tiles * tn), BF16),
        scratch_shapes=[pltpu.VMEM((d, tn), BF16)],
        compiler_params=_cparams(("arbitrary", "arbitrary")),
        name="inproj",
    )(h, w_in_t)


def _conv_kernel(x_ref, w_ref, b_ref, o_ref, *, q_tiles):
    c = pl.program_id(1)
    x = x_ref[...].astype(F32)
    t = x.shape[0]
    rows = lax.broadcasted_iota(I32, x.shape, 0)
    w = w_ref[...]
    acc = x * w[CONV_W // 2:CONV_W // 2 + 1, :] + b_ref[...]
    for dlt in range(-(CONV_W // 2), CONV_W // 2 + 1):
        if dlt == 0:
            continue
        xs = pltpu.roll(x, shift=(-dlt) % t, axis=0)
        valid = (rows + dlt >= 0) & (rows + dlt < t)
        acc = acc + jnp.where(valid, xs, 0.0) * w[dlt + CONV_W // 2:dlt + CONV_W // 2 + 1, :]
    y = (0.5 * acc) * (1.0 + jnp.tanh(0.5 * acc))
    scale = jnp.where(c < q_tiles, QK_DIM ** -0.5, 1.0).astype(F32)
    o_ref[...] = (y * scale).astype(BF16)


def _conv(z3, conv_w, conv_b):
    b, t, _ = z3.shape
    width = conv_w.shape[1]
    tc = 256
    kern = functools.partial(_conv_kernel, q_tiles=(width // 2) // tc)
    return pl.pallas_call(
        kern,
        grid=(b, width // tc),
        in_specs=[pl.BlockSpec((None, t, tc), lambda i, c: (i, 0, c)),
                  pl.BlockSpec((CONV_W, tc), lambda i, c: (0, c)),
                  pl.BlockSpec((1, tc), lambda i, c: (0, c))],
        out_specs=pl.BlockSpec((None, t, tc), lambda i, c: (i, 0, c)),
        out_shape=jax.ShapeDtypeStruct((b, t, width), BF16),
        compiler_params=_cparams(("arbitrary", "arbitrary")),
        name="conv",
    )(z3, conv_w, conv_b.reshape(1, width))


def _mlstm_kernel(qkc_f, qkl_f, vc_f, vl_f, gc_f, gl_f, qkc_b, qkl_b, vc_b, vl_b, gc_b, gl_b,
                  hf_ref, hb_ref, *state, n_ctx_chunks):
    c_scr, m_scr = state[:2 * M_HEADS], state[2 * M_HEADS:]
    s = pl.program_id(1)

    @pl.when(s == 0)
    def _():
        for ref in state:
            ref[...] = jnp.zeros_like(ref)

    is_ctx = s < n_ctx_chunks
    rows = lax.broadcasted_iota(I32, (CHUNK, CHUNK), 0)
    lanes = lax.broadcasted_iota(I32, (CHUNK, CHUNK), 1)
    ones_col = jnp.where(lanes == 0, 1.0, 0.0).astype(BF16)
    dn_nt = (((1,), (1,)), ((), ()))
    dn_tn = (((0,), (0,)), ((), ()))
    qk_w = M_HEADS * QK_DIM

    dirs = ((qkc_f, qkl_f, vc_f, vl_f, gc_f, gl_f, hf_ref), (qkc_b, qkl_b, vc_b, vl_b, gc_b, gl_b, hb_ref))
    per_dir = []
    for d, (qkc, qkl, vc, vl, gc, gl, out_ref) in enumerate(dirs):
        qk = jnp.where(is_ctx, qkc[...], qkl[...])
        v = jnp.where(is_ctx, vc[...], vl[...])
        g = jnp.where(is_ctx, gc[...], gl[...])
        mask = (lanes <= rows) if d == 0 else (lanes >= rows)
        tri = jnp.where(mask, 1.0, 0.0).astype(BF16)
        ls = jnp.minimum(g, 0.0) - jnp.log(1.0 + jnp.exp(-jnp.abs(g)))
        l1, l2, l3 = _split3(ls)
        bcol = (jnp.dot(tri, l1, preferred_element_type=F32) + jnp.dot(tri, l2, preferred_element_type=F32)
                + jnp.dot(tri, l3, preferred_element_type=F32))
        brow, grow = bcol.T, g.T
        c0 = 2 * M_HEADS * d
        ib = grow[c0:c0 + M_HEADS, :] - brow[c0 + M_HEADS:c0 + 2 * M_HEADS, :]
        lane_h = lax.broadcasted_iota(I32, ib.shape, 1)
        rmax = ib
        step = 1
        while step < CHUNK:
            if d == 0:
                prev = jnp.where(lane_h >= step, pltpu.roll(rmax, step, axis=1), -jnp.inf)
            else:
                prev = jnp.where(lane_h < CHUNK - step, pltpu.roll(rmax, CHUNK - step, axis=1), -jnp.inf)
            rmax = jnp.maximum(rmax, prev)
            step *= 2
        rmax_col = jnp.concatenate([rmax, jnp.zeros((CHUNK - M_HEADS, CHUNK), F32)], axis=0).T
        per_dir.append((d, qk, v, g, mask, bcol, ib, rmax_col, out_ref))

    units = [(pd, h) for pd in per_dir for h in range(M_HEADS)]
    stage1 = []
    for (d, qk, v, g, mask, bcol, ib, rmax_col, out_ref), h in units:
        col_i = 2 * M_HEADS * d + h
        col_f = col_i + M_HEADS
        idx = d * M_HEADS + h
        end = CHUNK - 1 if d == 0 else 0
        bc = bcol[:, col_f:col_f + 1]
        ic = g[:, col_i:col_i + 1]
        ibr = ib[h:h + 1, :]
        m = m_scr[idx][0:1, 0:1]
        inter = bc + m
        m_t = jnp.maximum(inter, bc + rmax_col[:, h:h + 1])
        w_inter = jnp.exp(inter - m_t)
        b_end = bc[end:end + 1, :]
        m_new = jnp.maximum(b_end + m, b_end + rmax_col[end:end + 1, h:h + 1])
        decay = jnp.exp(b_end + m - m_new)
        w_col = jnp.exp(b_end - bc + ic - m_new)
        q = qk[:, h * QK_DIM:(h + 1) * QK_DIM]
        k = qk[:, qk_w + h * QK_DIM:qk_w + (h + 1) * QK_DIM]
        decay_mat = jnp.exp(jnp.where(mask, (bc - m_t) + ibr, -jnp.inf))
        sc = lax.dot_general(q, k, dn_nt, preferred_element_type=F32) * decay_mat
        lhs = jnp.concatenate([(q.astype(F32) * w_inter).astype(BF16), sc.astype(BF16)], axis=1)
        kw = (k.astype(F32) * w_col).astype(BF16)
        stage1.append((lhs, kw, m_t, m_new, decay))

    stage2 = []
    for ((d, qk, v, g, mask, bcol, brow, grow, out_ref), h), (lhs, kw, m_t, m_new, decay) in zip(units, stage1):
        idx = d * M_HEADS + h
        vaug = jnp.concatenate([v[:, h * V_DIM:(h + 1) * V_DIM], ones_col], axis=1)
        caug = c_scr[idx][...]
        num = jnp.dot(lhs, jnp.concatenate([caug.astype(BF16), vaug], axis=0), preferred_element_type=F32)
        upd = lax.dot_general(kw, vaug, dn_tn, preferred_element_type=F32)
        stage2.append((num, upd, caug))

    for ((d, qk, v, g, mask, bcol, brow, grow, out_ref), h), (lhs, kw, m_t, m_new, decay), (num, upd, caug) in zip(
            units, stage1, stage2):
        idx = d * M_HEADS + h
        den = num[:, V_DIM:V_DIM + 1]
        hout = num[:, :V_DIM] / jnp.maximum(jnp.abs(den), jnp.exp(-m_t))
        c_scr[idx][...] = decay * caug + upd
        m_scr[idx][...] = jnp.broadcast_to(m_new, m_scr[idx].shape)
        out_ref[:, h * V_DIM:(h + 1) * V_DIM] = hout.astype(out_ref.dtype)


def _mlstm(qk_ctx, qk_lat, z_ctx3, z_lat3, g_ctx3, g_lat3):
    b, tc, _ = qk_ctx.shape
    t = qk_lat.shape[1]
    nc, nl = tc // CHUNK, t // CHUNK
    vw = M_HEADS * V_DIM
    qkw = 2 * M_HEADS * QK_DIM

    f_ctx = lambda i, s: (i, jnp.minimum(s, nc - 1), 0)
    f_lat = lambda i, s: (i, jnp.clip(s - nc, 0, nl - 1), 0)
    b_ctx = lambda i, s: (i, jnp.maximum(nc - 1 - s, 0), 0)
    b_lat = lambda i, s: (i, jnp.clip(nc + nl - 1 - s, 0, nl - 1), 0)
    v_of = lambda f: (lambda i, s: (f(i, s)[0], f(i, s)[1], 1))

    def specs(fc, fl):
        return [pl.BlockSpec((None, CHUNK, qkw), fc), pl.BlockSpec((None, CHUNK, qkw), fl),
                pl.BlockSpec((None, CHUNK, vw), v_of(fc)), pl.BlockSpec((None, CHUNK, vw), v_of(fl)),
                pl.BlockSpec((None, CHUNK, LANES), fc), pl.BlockSpec((None, CHUNK, LANES), fl)]

    kern = functools.partial(_mlstm_kernel, n_ctx_chunks=nc)
    args = (qk_ctx, qk_lat, z_ctx3, z_lat3, g_ctx3, g_lat3)
    return pl.pallas_call(
        kern,
        grid=(b, nc + nl),
        in_specs=specs(f_ctx, f_lat) + specs(b_ctx, b_lat),
        out_specs=[pl.BlockSpec((None, CHUNK, vw), f_lat), pl.BlockSpec((None, CHUNK, vw), b_lat)],
        out_shape=[jax.ShapeDtypeStruct((b, t, vw), BF16), jax.ShapeDtypeStruct((b, t, vw), BF16)],
        scratch_shapes=([pltpu.VMEM((QK_DIM, V_DIM + LANES), F32)] * (2 * M_HEADS)
                        + [pltpu.VMEM((8, LANES), F32)] * (2 * M_HEADS)),
        compiler_params=_cparams(("arbitrary", "arbitrary")),
        name="mlstm",
    )(*args, *args)


MERGE_ROWS = 256


def _merge_kernel(hf_ref, hb_ref, o_ref, u_ref, sv_ref, ga_ref, gb_ref, gm_ref, gs_ref, sw_ref, sb_ref,
                  pa_ref, pb_ref, out_ref):
    tm, d = out_ref.shape
    gw = d // SGU_GROUPS
    for grp in range(tm // MERGE_ROWS):
        r0 = grp * MERGE_ROWS
        rs = slice(r0, r0 + MERGE_ROWS)
        a_parts = []
        for h in range(M_HEADS):
            sl = slice(h * V_DIM, (h + 1) * V_DIM)
            hs = hf_ref[rs, sl].astype(F32) + hb_ref[rs, sl].astype(F32)
            ms = jnp.mean(hs * hs, axis=-1, keepdims=True)
            y = hs * lax.rsqrt(ms + EPS) * gm_ref[:, sl]
            a_parts.append((y * o_ref[rs, sl].astype(F32)).astype(BF16))
        a = jnp.concatenate(a_parts, axis=1)
        sv = sv_ref[rs, :].astype(F32)
        ms = jnp.mean(sv * sv, axis=-1, keepdims=True)
        svn = (sv * lax.rsqrt(ms + EPS) * gs_ref[...]).astype(BF16)
        b_rows = []
        for c in range(MERGE_ROWS // CHUNK):
            cr = slice(c * CHUNK, (c + 1) * CHUNK)
            b_parts = []
            for gi in range(SGU_GROUPS):
                cs = slice(gi * gw, (gi + 1) * gw)
                zz = jnp.dot(sw_ref[gi], svn[cr, cs], preferred_element_type=F32) + sb_ref[:, gi:gi + 1]
                b_parts.append((u_ref[r0 + c * CHUNK:r0 + (c + 1) * CHUNK, cs].astype(F32) * zz).astype(BF16))
            b_rows.append(jnp.concatenate(b_parts, axis=1))
        b = jnp.concatenate(b_rows, axis=0)
        ya = jnp.dot(a, pa_ref[...], preferred_element_type=F32)
        yb = jnp.dot(b, pb_ref[...], preferred_element_type=F32)
        out_ref[rs, :] = (ga_ref[rs, :].astype(F32) * ya + gb_ref[rs, :].astype(F32) * yb).astype(BF16)


def _merge(hf, hb, z, gm, gs, sgu_w, sgu_bt, pa, pb):
    r, d = hf.shape
    tm = 512
    row = lambda k: (lambda i: (i, k))
    const = lambda shape: pl.BlockSpec(shape, lambda i: (0,) * len(shape), pipeline_mode=pl.Buffered(1))
    return pl.pallas_call(
        _merge_kernel,
        grid=(r // tm,),
        in_specs=[pl.BlockSpec((tm, d), row(0)), pl.BlockSpec((tm, d), row(0)),
                  pl.BlockSpec((tm, d), row(2)), pl.BlockSpec((tm, d), row(3)), pl.BlockSpec((tm, d), row(4)),
                  pl.BlockSpec((tm, d), row(5)), pl.BlockSpec((tm, d), row(6)),
                  const((1, d)), const((1, d)), const(sgu_w.shape), const(sgu_bt.shape),
                  const((d, d)), const((d, d))],
        out_specs=pl.BlockSpec((tm, d), row(0)),
        out_shape=jax.ShapeDtypeStruct((r, d), BF16),
        compiler_params=_cparams(("arbitrary",)),
        name="merge",
    )(hf, hb, z, z, z, z, z, gm, gs, sgu_w, sgu_bt, pa, pb)


def _outproj_kernel(m_ref, w_ref, x_ref, gate_ref, shift_ref, scale_ref, g_ref, rw_ref, rb_ref,
                    x1_ref, h2_ref, lg_ref):
    r1, r2, _ = _split3(rw_ref[...])
    ne = lg_ref.shape[0]
    for grp in range(x_ref.shape[0] // MERGE_ROWS):
        r0 = grp * MERGE_ROWS
        rs = slice(r0, r0 + MERGE_ROWS)
        out = jnp.dot(m_ref[rs, :], w_ref[...], preferred_element_type=F32)
        x1 = x_ref[rs, :] + gate_ref[...] * out
        x1_ref[rs, :] = x1
        ms = jnp.mean(x1 * x1, axis=-1, keepdims=True)
        h2 = (x1 * lax.rsqrt(ms + EPS) * g_ref[...]) * (1.0 + scale_ref[...]) + shift_ref[...]
        hi, lo, _ = _split3(h2)
        lg = (jnp.dot(hi, r1, preferred_element_type=F32) + jnp.dot(lo, r1, preferred_element_type=F32)
              + jnp.dot(hi, r2, preferred_element_type=F32)) + rb_ref[...]
        lg_ref[:, rs] = lg.T[:ne, :]
        _store_token_major(h2_ref, r0, h2)


def _outproj(merged, w_out, x2d, mod3, mod_row_of_tile, norm_g, rw, rb, tm):
    r, d = x2d.shape
    ne = rw.shape[1]
    rw_t = jnp.pad(rw, ((0, 0), (0, LANES - ne)))
    rb_col = jnp.pad(rb.reshape(1, ne), ((0, 0), (0, LANES - ne)))
    modspec = lambda k: pl.BlockSpec((None, 1, d), lambda i: (mod_row_of_tile(i), 0, k))
    return pl.pallas_call(
        _outproj_kernel,
        grid=(r // tm,),
        in_specs=[pl.BlockSpec((tm, d), lambda i: (i, 0)),
                  pl.BlockSpec((d, d), lambda i: (0, 0), pipeline_mode=pl.Buffered(1)),
                  pl.BlockSpec((tm, d), lambda i: (i, 0)),
                  modspec(2), modspec(3), modspec(4),
                  pl.BlockSpec((1, d), lambda i: (0, 0)),
                  pl.BlockSpec((d, LANES), lambda i: (0, 0)),
                  pl.BlockSpec((1, LANES), lambda i: (0, 0))],
        out_specs=[pl.BlockSpec((tm, d), lambda i: (i, 0)),
                   pl.BlockSpec((tm * TOKEN_ROWS, LANES), lambda i: (i, 0)),
                   pl.BlockSpec((ne, tm), lambda i: (0, i))],
        out_shape=[jax.ShapeDtypeStruct((r, d), F32),
                   jax.ShapeDtypeStruct((r * TOKEN_ROWS, LANES), F32),
                   jax.ShapeDtypeStruct((ne, r), F32)],
        compiler_params=_cparams(("arbitrary",)),
        name="outproj",
    )(merged, w_out, x2d, mod3, mod3, mod3, norm_g, rw_t, rb_col)


def _route_kernel(lg_ref, eid_ref, gate_ref, rank_ref, cnt_ref, carry_scr):
    i = pl.program_id(0)

    @pl.when(i == 0)
    def _():
        carry_scr[...] = jnp.zeros_like(carry_scr)

    l = lg_ref[...]
    ne, tm = l.shape
    e_iota = lax.broadcasted_iota(I32, (ne, tm), 0)
    vals, onehots = [], []
    for k in range(TOP_K):
        mx = jnp.max(l, axis=0, keepdims=True)
        idx = jnp.min(jnp.where(l == mx, e_iota, ne), axis=0, keepdims=True)
        oh = e_iota == idx
        l = jnp.where(oh, -jnp.inf, l)
        vals.append(mx)
        onehots.append(oh)
        eid_ref[k:k + 1, :] = idx
    ex = [jnp.exp(vk - vals[0]) for vk in vals]
    tot = ex[0] + ex[1] + ex[2] + ex[3]
    for k in range(TOP_K):
        gate_ref[k:k + 1, :] = ex[k] / tot

    oh_all = jnp.zeros((ne, tm), F32)
    for oh in onehots:
        oh_all = oh_all + jnp.where(oh, 1.0, 0.0)
    oh_all = oh_all.astype(BF16)
    r_i = lax.broadcasted_iota(I32, (LANES, LANES), 0)
    c_i = lax.broadcasted_iota(I32, (LANES, LANES), 1)
    tri_excl = jnp.where(r_i < c_i, 1.0, 0.0).astype(BF16)
    ones_m = jnp.ones((LANES, LANES), BF16)
    carry = carry_scr[...]
    for blk in range(tm // LANES):
        sl = slice(blk * LANES, (blk + 1) * LANES)
        ohb = oh_all[:, sl]
        cum = jnp.dot(ohb, tri_excl, preferred_element_type=F32) + carry
        for k in range(TOP_K):
            rk = jnp.sum(jnp.where(onehots[k][:, sl], cum, 0.0), axis=0, keepdims=True)
            rank_ref[k:k + 1, sl] = rk.astype(I32)
        carry = carry + jnp.dot(ohb, ones_m, preferred_element_type=F32)
    carry_scr[...] = carry
    cnt_ref[...] = carry


def _route(logits_t):
    ne, r = logits_t.shape
    tm = 1024
    return pl.pallas_call(
        _route_kernel,
        grid=(r // tm,),
        in_specs=[pl.BlockSpec((ne, tm), lambda i: (0, i))],
        out_specs=[pl.BlockSpec((TOP_K, tm), lambda i: (0, i)),
                   pl.BlockSpec((TOP_K, tm), lambda i: (0, i)),
                   pl.BlockSpec((TOP_K, tm), lambda i: (0, i)),
                   pl.BlockSpec((ne, LANES), lambda i: (0, 0))],
        out_shape=[jax.ShapeDtypeStruct((TOP_K, r), I32),
                   jax.ShapeDtypeStruct((TOP_K, r), F32),
                   jax.ShapeDtypeStruct((TOP_K, r), I32),
                   jax.ShapeDtypeStruct((ne, LANES), F32)],
        scratch_shapes=[pltpu.VMEM((ne, LANES), F32)],
        compiler_params=_cparams(("arbitrary",)),
        name="route",
    )(logits_t)


def _slot_kernel(eid_ref, rank_ref, cnt_ref, slot_ref):
    ne = cnt_ref.shape[0]
    nblk = jnp.floor((cnt_ref[...] + (MOE_TILE - 1.0)) * (1.0 / MOE_TILE))
    r_i = lax.broadcasted_iota(I32, (ne, ne), 0)
    c_i = lax.broadcasted_iota(I32, (ne, ne), 1)
    tri = jnp.where(c_i < r_i, 1.0, 0.0).astype(BF16)
    start = jnp.dot(tri, nblk.astype(BF16), preferred_element_type=F32) * MOE_TILE
    eid = eid_ref[...]
    acc = rank_ref[...]
    for e in range(ne):
        acc = acc + jnp.where(eid == e, start[e:e + 1, 0:1].astype(I32), 0)
    slot_ref[...] = acc


def _slots(eid, rank, cnt):
    return pl.pallas_call(
        _slot_kernel,
        out_shape=jax.ShapeDtypeStruct(eid.shape, I32),
        compiler_params=pltpu.CompilerParams(vmem_limit_bytes=V7X_VMEM_LIMIT_BYTES),
        name="slots",
    )(eid, rank, cnt)


def _dispatch_copy(slot_ref, h_ref, xs_ref, sem, k, t, base, n_tok):
    d0 = pl.multiple_of(slot_ref[k * n_tok + base + t] * TOKEN_ROWS, TOKEN_ROWS)
    s0 = pl.multiple_of(t * TOKEN_ROWS, TOKEN_ROWS)
    return pltpu.make_async_copy(h_ref.at[pl.ds(s0, TOKEN_ROWS), :], xs_ref.at[pl.ds(d0, TOKEN_ROWS), :], sem)


def _zero_copy(zbuf, xs_ref, zsem, row, n_rows):
    d0 = pl.multiple_of(row * TOKEN_ROWS, TOKEN_ROWS)
    return pltpu.make_async_copy(zbuf.at[pl.ds(0, n_rows * TOKEN_ROWS), :],
                                 xs_ref.at[pl.ds(d0, n_rows * TOKEN_ROWS), :], zsem)


def _zero_fill(start_ref, cnt_ref, zbuf, xs_ref, zsem, p_rows, begin):
    ne = start_ref.shape[0]

    def op(cp):
        cp.start() if begin else cp.wait()

    def per_expert(e, used):
        n = cnt_ref[e]
        s0 = start_ref[e]
        c_step = (n + MOE_STEP - 1) // MOE_STEP * MOE_STEP
        c_tile = (n + MOE_TILE - 1) // MOE_TILE * MOE_TILE

        def one_row(r, c):
            op(_zero_copy(zbuf, xs_ref, zsem, s0 + r, 1))
            return c

        def one_unit(u, c):
            op(_zero_copy(zbuf, xs_ref, zsem, s0 + u * MOE_STEP, MOE_STEP))
            return c

        lax.fori_loop(n, c_step, one_row, 0)
        lax.fori_loop(c_step // MOE_STEP, c_tile // MOE_STEP, one_unit, 0)
        return s0 + c_tile

    used = lax.fori_loop(0, ne, per_expert, 0)

    def trailing(u, c):
        op(_zero_copy(zbuf, xs_ref, zsem, u * MOE_STEP, MOE_STEP))
        return c

    lax.fori_loop(used // MOE_STEP, p_rows // MOE_STEP, trailing, 0)


def _dispatch_kernel(slot_ref, start_ref, cnt_ref, h_ref, xs_ref, zbuf, sem, zsem, *, tm, n_tok, p_rows):
    i = pl.program_id(0)
    base = i * tm

    @pl.when(i == 0)
    def _():
        zbuf[...] = jnp.zeros_like(zbuf)
        _zero_fill(start_ref, cnt_ref, zbuf, xs_ref, zsem, p_rows, True)

    def start(t, carry):
        for k in range(TOP_K):
            _dispatch_copy(slot_ref, h_ref, xs_ref, sem, k, t, base, n_tok).start()
        return carry

    def wait(t, carry):
        for k in range(TOP_K):
            _dispatch_copy(slot_ref, h_ref, xs_ref, sem, k, t, base, n_tok).wait()
        return carry

    lax.fori_loop(0, tm, start, 0, unroll=4)
    lax.fori_loop(0, tm, wait, 0, unroll=4)

    @pl.when(i == pl.num_programs(0) - 1)
    def _():
        _zero_fill(start_ref, cnt_ref, zbuf, xs_ref, zsem, p_rows, False)


def _dispatch(h2p, slot, start_pad, counts, p_rows):
    n_tok = h2p.shape[0] // TOKEN_ROWS
    tm = 512
    grid_spec = pltpu.PrefetchScalarGridSpec(
        num_scalar_prefetch=3,
        grid=(n_tok // tm,),
        in_specs=[pl.BlockSpec((tm * TOKEN_ROWS, LANES), lambda i, *_: (i, 0))],
        out_specs=pl.BlockSpec(memory_space=pl.ANY),
        scratch_shapes=[pltpu.VMEM((MOE_STEP * TOKEN_ROWS, LANES), h2p.dtype),
                        pltpu.SemaphoreType.DMA(()), pltpu.SemaphoreType.DMA(())],
    )
    return pl.pallas_call(
        functools.partial(_dispatch_kernel, tm=tm, n_tok=n_tok, p_rows=p_rows),
        grid_spec=grid_spec,
        out_shape=jax.ShapeDtypeStruct((p_rows * TOKEN_ROWS, LANES), h2p.dtype),
        compiler_params=_cparams(("arbitrary",)),
        name="dispatch",
    )(slot, start_pad, counts, h2p)


def _weight_copies(w_hbm, wbuf, sems, e, col_starts, slot):
    tn = wbuf.shape[-1]
    aligned = lambda c: c if isinstance(c, int) else pl.multiple_of(c, LANES)
    return [pltpu.make_async_copy(w_hbm.at[e, :, pl.ds(aligned(c), tn)], wbuf.at[slot, i], sems.at[slot])
            for i, c in enumerate(col_starts)]


def _advance_weight_pipeline(w_hbm, wbuf, sems, sched, j, nj, it, col_starts_of):
    e_ref, run_ref, last_ref, nxt_ref, nruns_ref = sched
    slot = (j * nruns_ref[0] + run_ref[it]) % 2

    @pl.when((j == 0) & (it == 0))
    def _():
        for cp in _weight_copies(w_hbm, wbuf, sems, e_ref[it], col_starts_of(j), slot):
            cp.start()

    for cp in _weight_copies(w_hbm, wbuf, sems, e_ref[it], col_starts_of(j), slot):
        cp.wait()
    is_last = last_ref[it] == 1

    @pl.when(jnp.logical_not(is_last))
    def _():
        for cp in _weight_copies(w_hbm, wbuf, sems, nxt_ref[it], col_starts_of(j), 1 - slot):
            cp.start()

    @pl.when(is_last & (j + 1 < nj))
    def _():
        for cp in _weight_copies(w_hbm, wbuf, sems, nxt_ref[it], col_starts_of(j + 1), 1 - slot):
            cp.start()

    return slot


def _moe_up_kernel(blk_ref, oblk_ref, e_ref, nv_ref, new_ref, run_ref, last_ref, nxt_ref, nruns_ref,
                   xs_ref, w_hbm, bg_ref, bl_ref, act_ref, wbuf, sems, wg_bf, wl_bf):
    j = pl.program_id(0)
    nj = pl.num_programs(0)
    it = pl.program_id(1)
    tn = wg_bf.shape[1]

    @pl.when(new_ref[it] == 1)
    def _():
        slot = _advance_weight_pipeline(w_hbm, wbuf, sems, (e_ref, run_ref, last_ref, nxt_ref, nruns_ref), j, nj, it,
                                        lambda jj: (jj * tn, (nj + jj) * tn))
        wg_bf[...] = wbuf[slot, 0].astype(BF16)
        wl_bf[...] = wbuf[slot, 1].astype(BF16)

    def rows_chunk(r0, n):
        x = _load_token_major(xs_ref, r0, n).astype(BF16)
        hg = jnp.dot(x, wg_bf[...], preferred_element_type=F32) + bg_ref[...]
        hl = jnp.dot(x, wl_bf[...], preferred_element_type=F32) + bl_ref[...]
        glu = jnp.minimum(hg, SWIGLU_LIMIT)
        lin = jnp.clip(hl, -SWIGLU_LIMIT, SWIGLU_LIMIT)
        act_ref[r0:r0 + n, :] = ((0.5 * glu) * (1.0 + jnp.tanh((0.5 * SWIGLU_ALPHA) * glu)) * (lin + 1.0)).astype(BF16)

    def rows_zero(r0, n):
        act_ref[r0:r0 + n, :] = jnp.zeros((n, act_ref.shape[1]), BF16)

    _moe_row_variants(nv_ref[it], rows_chunk, rows_zero)


def _moe_row_variants(nv, rows_chunk, rows_zero):
    for units in range(MOE_TILE // MOE_STEP + 1):
        rows = units * MOE_STEP

        @pl.when((nv > rows - MOE_STEP) & (nv <= rows) if units else nv <= 0)
        def _():
            r0 = 0
            while r0 < rows:
                n = min(MOE_HALF, rows - r0)
                rows_chunk(r0, n)
                r0 += n
            if rows < MOE_TILE:
                rows_zero(rows, MOE_TILE - rows)


def _moe_down_kernel(blk_ref, oblk_ref, e_ref, nv_ref, new_ref, run_ref, last_ref, nxt_ref, nruns_ref,
                     a_ref, w_hbm, b_ref, y_ref, wbuf, sems, w_bf):
    it = pl.program_id(0)

    @pl.when(new_ref[it] == 1)
    def _():
        slot = _advance_weight_pipeline(w_hbm, wbuf, sems, (e_ref, run_ref, last_ref, nxt_ref, nruns_ref), 0, 1, it,
                                        lambda jj: (0,))
        w_bf[...] = wbuf[slot, 0].astype(BF16)

    def rows_chunk(r0, n):
        y = jnp.dot(a_ref[r0:r0 + n, :], w_bf[...], preferred_element_type=F32) + b_ref[...]
        _store_token_major(y_ref, r0, y)

    def rows_zero(r0, n):
        y_ref[r0 * TOKEN_ROWS:(r0 + n) * TOKEN_ROWS, :] = jnp.zeros((n * TOKEN_ROWS, LANES), F32)

    _moe_row_variants(nv_ref[it], rows_chunk, rows_zero)


def _moe_up(items, xs, w1, b1):
    n_items = items[0].shape[0]
    ne, d, two_f = w1.shape
    f = two_f // 2
    nj = f // MOE_TN
    p = xs.shape[0] // TOKEN_ROWS
    b1r = b1.reshape(ne, 1, two_f)
    grid_spec = pltpu.PrefetchScalarGridSpec(
        num_scalar_prefetch=len(items),
        grid=(nj, n_items),
        in_specs=[pl.BlockSpec((MOE_TILE * TOKEN_ROWS, LANES), lambda j, it, blk, *_: (blk[it], 0)),
                  pl.BlockSpec(memory_space=pl.ANY),
                  pl.BlockSpec((None, 1, MOE_TN), lambda j, it, blk, oblk, e, *_: (e[it], 0, j)),
                  pl.BlockSpec((None, 1, MOE_TN), lambda j, it, blk, oblk, e, *_: (e[it], 0, nj + j))],
        out_specs=pl.BlockSpec((MOE_TILE, MOE_TN), lambda j, it, blk, oblk, *_: (oblk[it], j)),
        scratch_shapes=[pltpu.VMEM((2, 2, d, MOE_TN), F32), pltpu.SemaphoreType.DMA((2,)),
                        pltpu.VMEM((d, MOE_TN), BF16), pltpu.VMEM((d, MOE_TN), BF16)],
    )
    return pl.pallas_call(
        _moe_up_kernel,
        grid_spec=grid_spec,
        out_shape=jax.ShapeDtypeStruct((p, f), BF16),
        compiler_params=_cparams(("arbitrary", "arbitrary")),
        name="moe_up",
    )(*items, xs, w1, b1r, b1r)


def _moe_down(items, act, w2, b2):
    n_items = items[0].shape[0]
    ne, f, d = w2.shape
    p = act.shape[0]
    b2r = b2.reshape(ne, 1, d)
    grid_spec = pltpu.PrefetchScalarGridSpec(
        num_scalar_prefetch=len(items),
        grid=(n_items,),
        in_specs=[pl.BlockSpec((MOE_TILE, f), lambda it, blk, *_: (blk[it], 0)),
                  pl.BlockSpec(memory_space=pl.ANY),
                  pl.BlockSpec((None, 1, d), lambda it, blk, oblk, e, *_: (e[it], 0, 0))],
        out_specs=pl.BlockSpec((MOE_TILE * TOKEN_ROWS, LANES), lambda it, blk, oblk, *_: (oblk[it], 0)),
        scratch_shapes=[pltpu.VMEM((2, 1, f, d), F32), pltpu.SemaphoreType.DMA((2,)), pltpu.VMEM((f, d), BF16)],
    )
    return pl.pallas_call(
        _moe_down_kernel,
        grid_spec=grid_spec,
        out_shape=jax.ShapeDtypeStruct((p * TOKEN_ROWS, LANES), F32),
        compiler_params=_cparams(("arbitrary",)),
        name="moe_down",
    )(*items, act, w2, b2r)


FINAL_ROWS = 64


def _combine_copy(slot_ref, y_ref, buf, sems, k, t, tile, par, tm, n_tok):
    s0 = pl.multiple_of(slot_ref[k * n_tok + tile * tm + t] * TOKEN_ROWS, TOKEN_ROWS)
    d0 = pl.multiple_of(t * TOKEN_ROWS, TOKEN_ROWS)
    return pltpu.make_async_copy(y_ref.at[pl.ds(s0, TOKEN_ROWS), :], buf.at[par, k, pl.ds(d0, TOKEN_ROWS), :],
                                 sems.at[par])


def _final_kernel(slot_ref, y_ref, gates_ref, x1_ref, gate_ref, g_ref, o_ref, buf, sems, *, n_tok):
    tm, d = x1_ref.shape
    i = pl.program_id(0)
    par = i % 2

    def gather(tile, par_, start):
        def body(t, carry):
            for k in range(TOP_K):
                cp = _combine_copy(slot_ref, y_ref, buf, sems, k, t, tile, par_, tm, n_tok)
                cp.start() if start else cp.wait()
            return carry
        lax.fori_loop(0, tm, body, 0, unroll=4)

    @pl.when(i == 0)
    def _():
        gather(0, 0, True)

    @pl.when(i + 1 < pl.num_programs(0))
    def _():
        gather(i + 1, 1 - par, True)

    gather(i, par, False)

    for grp in range(tm // FINAL_ROWS):
        r0 = grp * FINAL_ROWS
        rs = slice(r0, r0 + FINAL_ROWS)
        gk = [jnp.broadcast_to(gates_ref[rs, k:k + 1], (FINAL_ROWS, LANES)) for k in range(TOP_K)]
        sq = jnp.zeros((FINAL_ROWS, LANES), F32)
        for s in range(TOKEN_ROWS):
            cs = slice(s * LANES, (s + 1) * LANES)
            rows = pl.ds(r0 * TOKEN_ROWS + s, FINAL_ROWS, stride=TOKEN_ROWS)
            acc = gk[0] * buf[par, 0, rows, :]
            for k in range(1, TOP_K):
                acc = acc + gk[k] * buf[par, k, rows, :]
            xf = x1_ref[rs, cs] + gate_ref[:, cs] * acc
            o_ref[rs, cs] = xf
            sq = sq + xf * xf
        scale = lax.rsqrt(jnp.sum(sq, axis=-1, keepdims=True) * (1.0 / d) + EPS)
        for s in range(TOKEN_ROWS):
            cs = slice(s * LANES, (s + 1) * LANES)
            o_ref[rs, cs] = o_ref[rs, cs] * scale * g_ref[:, cs]


def _final(slot, y, gates_tk, x1, mod3, mod_row_of_tile, final_g, tm):
    r, d = x1.shape
    grid_spec = pltpu.PrefetchScalarGridSpec(
        num_scalar_prefetch=1,
        grid=(r // tm,),
        in_specs=[pl.BlockSpec(memory_space=pl.ANY),
                  pl.BlockSpec((tm, TOP_K), lambda i, slot: (i, 0)),
                  pl.BlockSpec((tm, d), lambda i, slot: (i, 0)),
                  pl.BlockSpec((None, 1, d), lambda i, slot: (mod_row_of_tile(i), 0, 5)),
                  pl.BlockSpec((1, d), lambda i, slot: (0, 0))],
        out_specs=pl.BlockSpec((tm, d), lambda i, slot: (i, 0)),
        scratch_shapes=[pltpu.VMEM((2, TOP_K, tm * TOKEN_ROWS, LANES), F32), pltpu.SemaphoreType.DMA((2,))],
    )
    return pl.pallas_call(
        functools.partial(_final_kernel, n_tok=r),
        grid_spec=grid_spec,
        out_shape=jax.ShapeDtypeStruct((r, d), F32),
        compiler_params=_cparams(("arbitrary",)),
        name="final",
    )(slot, y, gates_tk, x1, mod3, final_g)


def _moe_schedule(counts, n_items):
    ne = counts.shape[0]
    padded = (counts + MOE_TILE - 1) // MOE_TILE * MOE_TILE
    pad_end = jnp.cumsum(padded)
    start_pad = pad_end - padded
    n_real = pad_end[-1] // MOE_TILE
    b = jnp.arange(n_items, dtype=I32)
    blk = jnp.minimum(b, jnp.maximum(n_real - 1, 0))
    blk_start = blk * MOE_TILE
    e_of = jnp.minimum(jnp.sum((pad_end[None, :] <= blk_start[:, None]).astype(I32), axis=1), ne - 1)
    nv = jnp.clip(counts[e_of] - (blk_start - start_pad[e_of]), 0, MOE_TILE)
    nv = jnp.where(b < n_real, nv, 0)
    e_of = e_of.astype(I32)
    prev_e = jnp.concatenate([jnp.full((1,), -1, I32), e_of[:-1]])
    new = ((e_of != prev_e) | (b == 0)).astype(I32)
    run = jnp.cumsum(new) - 1
    n_runs = run[-1] + 1
    last = (run == n_runs - 1).astype(I32)
    first_pos = jnp.where(new == 1, b, n_items)
    nxt_pos = jnp.concatenate([lax.cummin(first_pos, reverse=True)[1:], jnp.full((1,), n_items, I32)])
    nxt = jnp.where(nxt_pos < n_items, e_of[jnp.minimum(nxt_pos, n_items - 1)], e_of[0])
    return start_pad.astype(I32), (blk.astype(I32), b, e_of, nv.astype(I32), new, run.astype(I32), last,
                                   nxt.astype(I32), n_runs.reshape(1).astype(I32))


def kernel(x, c, ctx, c_ctx, mod_w, mod_b, norm1_g, norm2_g, w_in, gate_b, conv_w, conv_b, mlstm_norm_g,
           sgu_norm_g, sgu_w, sgu_b, proj_a, proj_b, w_out, router_w, router_b, exp_w1, exp_b1, exp_w2,
           exp_b2, final_g):
    bsz, t, d = x.shape
    tc = ctx.shape[1]
    depth = mod_w.shape[0]
    assert depth == 1, "single-layer block"
    assert d == TOKEN_ROWS * LANES, "token-major rows hold TOKEN_ROWS*128 features"
    l = 0
    r = bsz * t
    qkw = 2 * M_HEADS * QK_DIM
    vw = M_HEADS * V_DIM
    off_gates = qkw + vw
    n_gates = 4 * M_HEADS
    off_o = off_gates + n_gates

    cc = jnp.zeros((8, d), F32).at[:bsz].set(c).at[bsz].set(c_ctx)
    mod3 = _mod(cc, mod_w[l], mod_b[l]).reshape(8, 1, N_MOD * d)

    in_cols = w_in.shape[2]
    assert off_gates % INPROJ_TN == 0 and (in_cols - off_o) % INPROJ_TN == 0
    n_aligned = off_gates // INPROJ_TN
    n_tiles = n_aligned + (in_cols - off_o) // INPROJ_TN
    w_in_t = jnp.swapaxes(w_in, 1, 2)
    w_gates = jnp.pad(w_in_t[l, off_gates:off_o, :].T, ((0, 0), (0, LANES - n_gates)))
    gb = jnp.pad(gate_b[l].reshape(1, n_gates), ((0, 0), (0, LANES - n_gates)))
    g1 = norm1_g[l].reshape(1, d)

    tm_in = 512
    h_lat, g_lat = _norm(x.reshape(r, d), mod3, lambda i: i // (t // tm_in), g1, w_gates, gb, tm_in)
    h_ctx, g_ctx = _norm(ctx.reshape(bsz * tc, d), mod3, lambda i: bsz, g1, w_gates, gb, tc)
    z_lat = _inproj(h_lat, w_in_t, l, n_aligned, n_gates, n_tiles, 2048)
    z_ctx = _inproj(h_ctx, w_in_t, l, n_aligned, n_gates, n_aligned, bsz * tc)

    z_lat3 = z_lat.reshape(bsz, t, z_lat.shape[1])
    z_ctx3 = z_ctx.reshape(bsz, tc, z_ctx.shape[1])
    qk_lat = _conv(z_lat3, conv_w[l], conv_b[l])
    qk_ctx = _conv(z_ctx3, conv_w[l], conv_b[l])

    hf, hb = _mlstm(qk_ctx, qk_lat, z_ctx3, z_lat3, g_ctx.reshape(bsz, tc, LANES), g_lat.reshape(bsz, t, LANES))

    merged = _merge(hf.reshape(r, vw), hb.reshape(r, vw), z_lat,
                    mlstm_norm_g[l].reshape(1, vw), sgu_norm_g[l].reshape(1, -1),
                    sgu_w[l].astype(BF16), sgu_b[l].T, proj_a[l].astype(BF16), proj_b[l].astype(BF16))

    tm_out = 512
    x1, h2p, logits_t = _outproj(merged, w_out[l].astype(BF16), x.reshape(r, d), mod3,
                                 lambda i: i // (t // tm_out), norm2_g[l].reshape(1, d),
                                 router_w[l], router_b[l], tm_out)
    tm_fin = 256

    eid, gates, rank, cnt = _route(logits_t)
    counts = cnt[:, 0].astype(I32)
    n_items = (r * TOP_K) // MOE_TILE + N_EXPERTS
    start_pad, items = _moe_schedule(counts, n_items)
    slot = _slots(eid, rank, cnt).reshape(-1)
    xs = _dispatch(h2p, slot, start_pad, counts, n_items * MOE_TILE)
    act = _moe_up(items, xs, exp_w1[l], exp_b1[l])
    y = _moe_down(items, act, exp_w2[l], exp_b2[l])
    out = _final(slot, y, gates.T, x1, mod3, lambda i: i // (t // tm_fin), final_g.reshape(1, d), tm_fin)
    return out.reshape(bsz, t, d)
```

```python
import functools

import jax
import jax.numpy as jnp
from jax import lax
from jax.experimental import pallas as pl
from jax.experimental.pallas import tpu as pltpu

F32 = jnp.float32
BF16 = jnp.bfloat16
I32 = jnp.int32
U32 = jnp.uint32

EPS = 1e-6
M_HEADS = 8
QK_DIM = 128
V_DIM = 256
CONV_W = 5
CHUNK = 128
SGU_GROUPS = 8
N_EXPERTS = 32
TOP_K = 4
SWIGLU_ALPHA = 1.702
SWIGLU_LIMIT = 7.0
N_MOD = 6

V7X_VMEM_LIMIT_BYTES = 56 * 1024 * 1024
LANES = 128
MOE_TILE = 512
MOE_HALF = 256
MOE_STEP = 128
MOE_TN = 1024


def _cparams(sem):
    return pltpu.CompilerParams(dimension_semantics=sem, vmem_limit_bytes=V7X_VMEM_LIMIT_BYTES)


def _sigmoid(x):
    return 1.0 / (1.0 + jnp.exp(-x))


def _gelu_tanh(x):
    return 0.5 * x * (1.0 + jnp.tanh(0.7978845608028654 * (x + 0.044715 * (x * x * x))))


def _split3(a):
    a1 = a.astype(BF16)
    r1 = a - a1.astype(F32)
    a2 = r1.astype(BF16)
    a3 = (r1 - a2.astype(F32)).astype(BF16)
    return a1, a2, a3


TOKEN_ROWS = 16


def _store_token_major(ref, row0, vals):
    n = vals.shape[0]
    for s in range(TOKEN_ROWS):
        ref[pl.ds(row0 * TOKEN_ROWS + s, n, stride=TOKEN_ROWS), :] = vals[:, s * LANES:(s + 1) * LANES]


def _load_token_major(ref, row0, n, lead=()):
    cols = [ref[(*lead, pl.ds(row0 * TOKEN_ROWS + s, n, stride=TOKEN_ROWS), slice(None))] for s in range(TOKEN_ROWS)]
    return jnp.concatenate(cols, axis=1)


def _mod_kernel(s_ref, w_ref, b_ref, o_ref):
    s = s_ref[...]
    s = s * _sigmoid(s)
    o_ref[...] = jnp.dot(s.astype(BF16), w_ref[...].astype(BF16), preferred_element_type=F32) + b_ref[...]


def _mod(cc, mod_w, mod_b):
    d, n = mod_w.shape
    tn = 1024
    return pl.pallas_call(
        _mod_kernel,
        grid=(n // tn,),
        in_specs=[pl.BlockSpec((8, d), lambda j: (0, 0)),
                  pl.BlockSpec((d, tn), lambda j: (0, j)),
                  pl.BlockSpec((1, tn), lambda j: (0, j))],
        out_specs=pl.BlockSpec((8, tn), lambda j: (0, j)),
        out_shape=jax.ShapeDtypeStruct((8, n), F32),
        compiler_params=_cparams(("arbitrary",)),
        name="mod",
    )(cc, mod_w, mod_b.reshape(1, n))


def _norm_kernel(x_ref, shift_ref, scale_ref, g_ref, wg_ref, gb_ref, h_ref, gates_ref):
    x = x_ref[...]
    ms = jnp.mean(x * x, axis=-1, keepdims=True)
    h = (x * lax.rsqrt(ms + EPS) * g_ref[...]) * (1.0 + scale_ref[...]) + shift_ref[...]
    h1, h2, _ = _split3(h)
    h_ref[...] = h1
    w1, w2, _ = _split3(wg_ref[...])
    gates_ref[...] = (jnp.dot(h1, w1, preferred_element_type=F32)
                      + jnp.dot(h1, w2, preferred_element_type=F32)
                      + jnp.dot(h2, w1, preferred_element_type=F32)) + gb_ref[...]


def _norm(x2d, mod3, mod_row_of_tile, norm_g, w_gates, gate_b, tm):
    r, d = x2d.shape
    return pl.pallas_call(
        _norm_kernel,
        grid=(r // tm,),
        in_specs=[pl.BlockSpec((tm, d), lambda i: (i, 0)),
                  pl.BlockSpec((None, 1, d), lambda i: (mod_row_of_tile(i), 0, 0)),
                  pl.BlockSpec((None, 1, d), lambda i: (mod_row_of_tile(i), 0, 1)),
                  pl.BlockSpec((1, d), lambda i: (0, 0)),
                  pl.BlockSpec((d, LANES), lambda i: (0, 0)),
                  pl.BlockSpec((1, LANES), lambda i: (0, 0))],
        out_specs=[pl.BlockSpec((tm, d), lambda i: (i, 0)),
                   pl.BlockSpec((tm, LANES), lambda i: (i, 0))],
        out_shape=[jax.ShapeDtypeStruct((r, d), BF16),
                   jax.ShapeDtypeStruct((r, LANES), F32)],
        compiler_params=_cparams(("arbitrary",)),
        name="norm",
    )(x2d, mod3, mod3, norm_g, w_gates, gate_b)


INPROJ_TN = 1024
INPROJ_ROWS = 256


def _inproj_kernel(h_ref, wt_ref, z_ref, w_bf, *, tiles_per_seg):
    j = pl.program_id(0)
    i = pl.program_id(1)

    @pl.when(i == 0)
    def _():
        w_bf[...] = wt_ref[...].T.astype(BF16)

    seg = j // tiles_per_seg
    is_raw = seg < 2
    is_gelu = (seg == 3) | (seg == 4)
    c0 = 0.7978845608028654
    p0 = jnp.where(is_gelu, c0, 0.5).astype(F32)
    p1 = jnp.where(is_gelu, c0 * 0.044715, 0.0).astype(F32)
    q0 = jnp.where(is_gelu, 0.0, 1.0).astype(F32)
    q1 = jnp.where(is_gelu, 1.0, 0.0).astype(F32)
    w = w_bf[...]
    for c in range(h_ref.shape[0] // INPROJ_ROWS):
        rs = slice(c * INPROJ_ROWS, (c + 1) * INPROJ_ROWS)
        z = jnp.dot(h_ref[rs, :], w, preferred_element_type=F32)
        act = (0.5 * (q0 + q1 * z)) * (1.0 + jnp.tanh(z * (p0 + p1 * (z * z))))
        z_ref[rs, :] = jnp.where(is_raw, z, act).astype(BF16)


def _inproj(h, w_in_t, layer, n_aligned, shift, n_tiles, tm):
    r, d = h.shape
    tn = INPROJ_TN
    kern = functools.partial(_inproj_kernel, tiles_per_seg=d // tn)
    assert shift % 8 == 0
    row0 = lambda j: pl.multiple_of(j * tn + jnp.where(j >= n_aligned, shift, 0), 8)
    return pl.pallas_call(
        kern,
        grid=(n_tiles, r // tm),
        in_specs=[pl.BlockSpec((tm, d), lambda j, i: (i, 0)),
                  pl.BlockSpec((None, pl.Element(tn), pl.Element(d)), lambda j, i: (layer, row0(j), 0))],
        out_specs=pl.BlockSpec((tm, tn), lambda j, i: (i, j)),
        out_shape=jax.ShapeDtypeStruct((r, n_tiles * tn), BF16),
        scratch_shapes=[pltpu.VMEM((d, tn), BF16)],
        compiler_params=_cparams(("arbitrary", "arbitrary")),
        name="inproj",
    )(h, w_in_t)


def _conv_kernel(x_ref, w_ref, b_ref, o_ref, *, q_tiles):
    c = pl.program_id(1)
    x = x_ref[...].astype(F32)
    t = x.shape[0]
    rows = lax.broadcasted_iota(I32, x.shape, 0)
    w = w_ref[...]
    acc = x * w[CONV_W // 2:CONV_W // 2 + 1, :] + b_ref[...]
    for dlt in range(-(CONV_W // 2), CONV_W // 2 + 1):
        if dlt == 0:
            continue
        xs = pltpu.roll(x, shift=(-dlt) % t, axis=0)
        valid = (rows + dlt >= 0) & (rows + dlt < t)
        acc = acc + jnp.where(valid, xs, 0.0) * w[dlt + CONV_W // 2:dlt + CONV_W // 2 + 1, :]
    y = (0.5 * acc) * (1.0 + jnp.tanh(0.5 * acc))
    scale = jnp.where(c < q_tiles, QK_DIM ** -0.5, 1.0).astype(F32)
    o_ref[...] = (y * scale).astype(BF16)


def _conv(z3, conv_w, conv_b):
    b, t, _ = z3.shape
    width = conv_w.shape[1]
    tc = 256
    kern = functools.partial(_conv_kernel, q_tiles=(width // 2) // tc)
    return pl.pallas_call(
        kern,
        grid=(b, width // tc),
        in_specs=[pl.BlockSpec((None, t, tc), lambda i, c: (i, 0, c)),
                  pl.BlockSpec((CONV_W, tc), lambda i, c: (0, c)),
                  pl.BlockSpec((1, tc), lambda i, c: (0, c))],
        out_specs=pl.BlockSpec((None, t, tc), lambda i, c: (i, 0, c)),
        out_shape=jax.ShapeDtypeStruct((b, t, width), BF16),
        compiler_params=_cparams(("arbitrary", "arbitrary")),
        name="conv",
    )(z3, conv_w, conv_b.reshape(1, width))


def _mlstm_kernel(qkc_f, qkl_f, vc_f, vl_f, gc_f, gl_f, qkc_b, qkl_b, vc_b, vl_b, gc_b, gl_b,
                  hf_ref, hb_ref, *state, n_ctx_chunks):
    c_scr, m_scr = state[:2 * M_HEADS], state[2 * M_HEADS:]
    s = pl.program_id(1)

    @pl.when(s == 0)
    def _():
        for ref in state:
            ref[...] = jnp.zeros_like(ref)

    is_ctx = s < n_ctx_chunks
    rows = lax.broadcasted_iota(I32, (CHUNK, CHUNK), 0)
    lanes = lax.broadcasted_iota(I32, (CHUNK, CHUNK), 1)
    ones_col = jnp.where(lanes == 0, 1.0, 0.0).astype(BF16)
    dn_nt = (((1,), (1,)), ((), ()))
    dn_tn = (((0,), (0,)), ((), ()))
    qk_w = M_HEADS * QK_DIM

    dirs = ((qkc_f, qkl_f, vc_f, vl_f, gc_f, gl_f, hf_ref), (qkc_b, qkl_b, vc_b, vl_b, gc_b, gl_b, hb_ref))
    per_dir = []
    for d, (qkc, qkl, vc, vl, gc, gl, out_ref) in enumerate(dirs):
        qk = jnp.where(is_ctx, qkc[...], qkl[...])
        v = jnp.where(is_ctx, vc[...], vl[...])
        g = jnp.where(is_ctx, gc[...], gl[...])
        mask = (lanes <= rows) if d == 0 else (lanes >= rows)
        tri = jnp.where(mask, 1.0, 0.0).astype(BF16)
        ls = jnp.minimum(g, 0.0) - jnp.log(1.0 + jnp.exp(-jnp.abs(g)))
        l1, l2, l3 = _split3(ls)
        bcol = (jnp.dot(tri, l1, preferred_element_type=F32) + jnp.dot(tri, l2, preferred_element_type=F32)
                + jnp.dot(tri, l3, preferred_element_type=F32))
        brow, grow = bcol.T, g.T
        c0 = 2 * M_HEADS * d
        ib = grow[c0:c0 + M_HEADS, :] - brow[c0 + M_HEADS:c0 + 2 * M_HEADS, :]
        lane_h = lax.broadcasted_iota(I32, ib.shape, 1)
        rmax = ib
        step = 1
        while step < CHUNK:
            if d == 0:
                prev = jnp.where(lane_h >= step, pltpu.roll(rmax, step, axis=1), -jnp.inf)
            else:
                prev = jnp.where(lane_h < CHUNK - step, pltpu.roll(rmax, CHUNK - step, axis=1), -jnp.inf)
            rmax = jnp.maximum(rmax, prev)
            step *= 2
        rmax_col = jnp.concatenate([rmax, jnp.zeros((CHUNK - M_HEADS, CHUNK), F32)], axis=0).T
        per_dir.append((d, qk, v, g, mask, bcol, ib, rmax_col, out_ref))

    units = [(pd, h) for pd in per_dir for h in range(M_HEADS)]
    stage1 = []
    for (d, qk, v, g, mask, bcol, ib, rmax_col, out_ref), h in units:
        col_i = 2 * M_HEADS * d + h
        col_f = col_i + M_HEADS
        idx = d * M_HEADS + h
        end = CHUNK - 1 if d == 0 else 0
        bc = bcol[:, col_f:col_f + 1]
        ic = g[:, col_i:col_i + 1]
        ibr = ib[h:h + 1, :]
        m = m_scr[idx][0:1, 0:1]
        inter = bc + m
        m_t = jnp.maximum(inter, bc + rmax_col[:, h:h + 1])
        w_inter = jnp.exp(inter - m_t)
        b_end = bc[end:end + 1, :]
        m_new = jnp.maximum(b_end + m, b_end + rmax_col[end:end + 1, h:h + 1])
        decay = jnp.exp(b_end + m - m_new)
        w_col = jnp.exp(b_end - bc + ic - m_new)
        q = qk[:, h * QK_DIM:(h + 1) * QK_DIM]
        k = qk[:, qk_w + h * QK_DIM:qk_w + (h + 1) * QK_DIM]
        decay_mat = jnp.exp(jnp.where(mask, (bc - m_t) + ibr, -jnp.inf))
        sc = lax.dot_general(q, k, dn_nt, preferred_element_type=F32) * decay_mat
        lhs = jnp.concatenate([(q.astype(F32) * w_inter).astype(BF16), sc.astype(BF16)], axis=1)
        kw = (k.astype(F32) * w_col).astype(BF16)
        stage1.append((lhs, kw, m_t, m_new, decay))

    stage2 = []
    for ((d, qk, v, g, mask, bcol, brow, grow, out_ref), h), (lhs, kw, m_t, m_new, decay) in zip(units, stage1):
        idx = d * M_HEADS + h
        vaug = jnp.concatenate([v[:, h * V_DIM:(h + 1) * V_DIM], ones_col], axis=1)
        caug = c_scr[idx][...]
        num = jnp.dot(lhs, jnp.concatenate([caug.astype(BF16), vaug], axis=0), preferred_element_type=F32)
        upd = lax.dot_general(kw, vaug, dn_tn, preferred_element_type=F32)
        stage2.append((num, upd, caug))

    for ((d, qk, v, g, mask, bcol, brow, grow, out_ref), h), (lhs, kw, m_t, m_new, decay), (num, upd, caug) in zip(
            units, stage1, stage2):
        idx = d * M_HEADS + h
        den = num[:, V_DIM:V_DIM + 1]
        hout = num[:, :V_DIM] / jnp.maximum(jnp.abs(den), jnp.exp(-m_t))
        c_scr[idx][...] = decay * caug + upd
        m_scr[idx][...] = jnp.broadcast_to(m_new, m_scr[idx].shape)
        out_ref[:, h * V_DIM:(h + 1) * V_DIM] = hout.astype(out_ref.dtype)


def _mlstm(qk_ctx, qk_lat, z_ctx3, z_lat3, g_ctx3, g_lat3):
    b, tc, _ = qk_ctx.shape
    t = qk_lat.shape[1]
    nc, nl = tc // CHUNK, t // CHUNK
    vw = M_HEADS * V_DIM
    qkw = 2 * M_HEADS * QK_DIM

    f_ctx = lambda i, s: (i, jnp.minimum(s, nc - 1), 0)
    f_lat = lambda i, s: (i, jnp.clip(s - nc, 0, nl - 1), 0)
    b_ctx = lambda i, s: (i, jnp.maximum(nc - 1 - s, 0), 0)
    b_lat = lambda i, s: (i, jnp.clip(nc + nl - 1 - s, 0, nl - 1), 0)
    v_of = lambda f: (lambda i, s: (f(i, s)[0], f(i, s)[1], 1))

    def specs(fc, fl):
        return [pl.BlockSpec((None, CHUNK, qkw), fc), pl.BlockSpec((None, CHUNK, qkw), fl),
                pl.BlockSpec((None, CHUNK, vw), v_of(fc)), pl.BlockSpec((None, CHUNK, vw), v_of(fl)),
                pl.BlockSpec((None, CHUNK, LANES), fc), pl.BlockSpec((None, CHUNK, LANES), fl)]

    kern = functools.partial(_mlstm_kernel, n_ctx_chunks=nc)
    args = (qk_ctx, qk_lat, z_ctx3, z_lat3, g_ctx3, g_lat3)
    return pl.pallas_call(
        kern,
        grid=(b, nc + nl),
        in_specs=specs(f_ctx, f_lat) + specs(b_ctx, b_lat),
        out_specs=[pl.BlockSpec((None, CHUNK, vw), f_lat), pl.BlockSpec((None, CHUNK, vw), b_lat)],
        out_shape=[jax.ShapeDtypeStruct((b, t, vw), BF16), jax.ShapeDtypeStruct((b, t, vw), BF16)],
        scratch_shapes=([pltpu.VMEM((QK_DIM, V_DIM + LANES), F32)] * (2 * M_HEADS)
                        + [pltpu.VMEM((8, LANES), F32)] * (2 * M_HEADS)),
        compiler_params=_cparams(("arbitrary", "arbitrary")),
        name="mlstm",
    )(*args, *args)


MERGE_ROWS = 256


def _merge_kernel(hf_ref, hb_ref, o_ref, u_ref, sv_ref, ga_ref, gb_ref, gm_ref, gs_ref, sw_ref, sb_ref,
                  pa_ref, pb_ref, out_ref):
    tm, d = out_ref.shape
    gw = d // SGU_GROUPS
    for grp in range(tm // MERGE_ROWS):
        r0 = grp * MERGE_ROWS
        rs = slice(r0, r0 + MERGE_ROWS)
        a_parts = []
        for h in range(M_HEADS):
            sl = slice(h * V_DIM, (h + 1) * V_DIM)
            hs = hf_ref[rs, sl].astype(F32) + hb_ref[rs, sl].astype(F32)
            ms = jnp.mean(hs * hs, axis=-1, keepdims=True)
            y = hs * lax.rsqrt(ms + EPS) * gm_ref[:, sl]
            a_parts.append((y * o_ref[rs, sl].astype(F32)).astype(BF16))
        a = jnp.concatenate(a_parts, axis=1)
        sv = sv_ref[rs, :].astype(F32)
        ms = jnp.mean(sv * sv, axis=-1, keepdims=True)
        svn = (sv * lax.rsqrt(ms + EPS) * gs_ref[...]).astype(BF16)
        b_rows = []
        for c in range(MERGE_ROWS // CHUNK):
            cr = slice(c * CHUNK, (c + 1) * CHUNK)
            b_parts = []
            for gi in range(SGU_GROUPS):
                cs = slice(gi * gw, (gi + 1) * gw)
                zz = jnp.dot(sw_ref[gi], svn[cr, cs], preferred_element_type=F32) + sb_ref[:, gi:gi + 1]
                b_parts.append((u_ref[r0 + c * CHUNK:r0 + (c + 1) * CHUNK, cs].astype(F32) * zz).astype(BF16))
            b_rows.append(jnp.concatenate(b_parts, axis=1))
        b = jnp.concatenate(b_rows, axis=0)
        ya = jnp.dot(a, pa_ref[...], preferred_element_type=F32)
        yb = jnp.dot(b, pb_ref[...], preferred_element_type=F32)
        out_ref[rs, :] = (ga_ref[rs, :].astype(F32) * ya + gb_ref[rs, :].astype(F32) * yb).astype(BF16)


def _merge(hf, hb, z, gm, gs, sgu_w, sgu_bt, pa, pb):
    r, d = hf.shape
    tm = 512
    row = lambda k: (lambda i: (i, k))
    const = lambda shape: pl.BlockSpec(shape, lambda i: (0,) * len(shape), pipeline_mode=pl.Buffered(1))
    return pl.pallas_call(
        _merge_kernel,
        grid=(r // tm,),
        in_specs=[pl.BlockSpec((tm, d), row(0)), pl.BlockSpec((tm, d), row(0)),
                  pl.BlockSpec((tm, d), row(2)), pl.BlockSpec((tm, d), row(3)), pl.BlockSpec((tm, d), row(4)),
                  pl.BlockSpec((tm, d), row(5)), pl.BlockSpec((tm, d), row(6)),
                  const((1, d)), const((1, d)), const(sgu_w.shape), const(sgu_bt.shape),
                  const((d, d)), const((d, d))],
        out_specs=pl.BlockSpec((tm, d), row(0)),
        out_shape=jax.ShapeDtypeStruct((r, d), BF16),
        compiler_params=_cparams(("arbitrary",)),
        name="merge",
    )(hf, hb, z, z, z, z, z, gm, gs, sgu_w, sgu_bt, pa, pb)


def _outproj_kernel(m_ref, w_ref, x_ref, gate_ref, shift_ref, scale_ref, g_ref, rw_ref, rb_ref,
                    x1_ref, h2_ref, lg_ref):
    r1, r2, _ = _split3(rw_ref[...])
    ne = lg_ref.shape[0]
    for grp in range(x_ref.shape[0] // MERGE_ROWS):
        r0 = grp * MERGE_ROWS
        rs = slice(r0, r0 + MERGE_ROWS)
        out = jnp.dot(m_ref[rs, :], w_ref[...], preferred_element_type=F32)
        x1 = x_ref[rs, :] + gate_ref[...] * out
        x1_ref[rs, :] = x1
        ms = jnp.mean(x1 * x1, axis=-1, keepdims=True)
        h2 = (x1 * lax.rsqrt(ms + EPS) * g_ref[...]) * (1.0 + scale_ref[...]) + shift_ref[...]
        hi, lo, _ = _split3(h2)
        lg = (jnp.dot(hi, r1, preferred_element_type=F32) + jnp.dot(lo, r1, preferred_element_type=F32)
              + jnp.dot(hi, r2, preferred_element_type=F32)) + rb_ref[...]
        lg_ref[:, rs] = lg.T[:ne, :]
        _store_token_major(h2_ref, r0, h2)


def _outproj(merged, w_out, x2d, mod3, mod_row_of_tile, norm_g, rw, rb, tm):
    r, d = x2d.shape
    ne = rw.shape[1]
    rw_t = jnp.pad(rw, ((0, 0), (0, LANES - ne)))
    rb_col = jnp.pad(rb.reshape(1, ne), ((0, 0), (0, LANES - ne)))
    modspec = lambda k: pl.BlockSpec((None, 1, d), lambda i: (mod_row_of_tile(i), 0, k))
    return pl.pallas_call(
        _outproj_kernel,
        grid=(r // tm,),
        in_specs=[pl.BlockSpec((tm, d), lambda i: (i, 0)),
                  pl.BlockSpec((d, d), lambda i: (0, 0), pipeline_mode=pl.Buffered(1)),
                  pl.BlockSpec((tm, d), lambda i: (i, 0)),
                  modspec(2), modspec(3), modspec(4),
                  pl.BlockSpec((1, d), lambda i: (0, 0)),
                  pl.BlockSpec((d, LANES), lambda i: (0, 0)),
                  pl.BlockSpec((1, LANES), lambda i: (0, 0))],
        out_specs=[pl.BlockSpec((tm, d), lambda i: (i, 0)),
                   pl.BlockSpec((tm * TOKEN_ROWS, LANES), lambda i: (i, 0)),
                   pl.BlockSpec((ne, tm), lambda i: (0, i))],
        out_shape=[jax.ShapeDtypeStruct((r, d), F32),
                   jax.ShapeDtypeStruct((r * TOKEN_ROWS, LANES), F32),
                   jax.ShapeDtypeStruct((ne, r), F32)],
        compiler_params=_cparams(("arbitrary",)),
        name="outproj",
    )(merged, w_out, x2d, mod3, mod3, mod3, norm_g, rw_t, rb_col)


def _route_kernel(lg_ref, eid_ref, gate_ref, rank_ref, cnt_ref, carry_scr):
    i = pl.program_id(0)

    @pl.when(i == 0)
    def _():
        carry_scr[...] = jnp.zeros_like(carry_scr)

    l = lg_ref[...]
    ne, tm = l.shape
    e_iota = lax.broadcasted_iota(I32, (ne, tm), 0)
    vals, onehots = [], []
    for k in range(TOP_K):
        mx = jnp.max(l, axis=0, keepdims=True)
        idx = jnp.min(jnp.where(l == mx, e_iota, ne), axis=0, keepdims=True)
        oh = e_iota == idx
        l = jnp.where(oh, -jnp.inf, l)
        vals.append(mx)
        onehots.append(oh)
        eid_ref[k:k + 1, :] = idx
    ex = [jnp.exp(vk - vals[0]) for vk in vals]
    tot = ex[0] + ex[1] + ex[2] + ex[3]
    for k in range(TOP_K):
        gate_ref[k:k + 1, :] = ex[k] / tot

    oh_all = jnp.zeros((ne, tm), F32)
    for oh in onehots:
        oh_all = oh_all + jnp.where(oh, 1.0, 0.0)
    oh_all = oh_all.astype(BF16)
    r_i = lax.broadcasted_iota(I32, (LANES, LANES), 0)
    c_i = lax.broadcasted_iota(I32, (LANES, LANES), 1)
    tri_excl = jnp.where(r_i < c_i, 1.0, 0.0).astype(BF16)
    ones_m = jnp.ones((LANES, LANES), BF16)
    carry = carry_scr[...]
    for blk in range(tm // LANES):
        sl = slice(blk * LANES, (blk + 1) * LANES)
        ohb = oh_all[:, sl]
        cum = jnp.dot(ohb, tri_excl, preferred_element_type=F32) + carry
        for k in range(TOP_K):
            rk = jnp.sum(jnp.where(onehots[k][:, sl], cum, 0.0), axis=0, keepdims=True)
            rank_ref[k:k + 1, sl] = rk.astype(I32)
        carry = carry + jnp.dot(ohb, ones_m, preferred_element_type=F32)
    carry_scr[...] = carry
    cnt_ref[...] = carry


def _route(logits_t):
    ne, r = logits_t.shape
    tm = 1024
    return pl.pallas_call(
        _route_kernel,
        grid=(r // tm,),
        in_specs=[pl.BlockSpec((ne, tm), lambda i: (0, i))],
        out_specs=[pl.BlockSpec((TOP_K, tm), lambda i: (0, i)),
                   pl.BlockSpec((TOP_K, tm), lambda i: (0, i)),
                   pl.BlockSpec((TOP_K, tm), lambda i: (0, i)),
                   pl.BlockSpec((ne, LANES), lambda i: (0, 0))],
        out_shape=[jax.ShapeDtypeStruct((TOP_K, r), I32),
                   jax.ShapeDtypeStruct((TOP_K, r), F32),
                   jax.ShapeDtypeStruct((TOP_K, r), I32),
                   jax.ShapeDtypeStruct((ne, LANES), F32)],
        scratch_shapes=[pltpu.VMEM((ne, LANES), F32)],
        compiler_params=_cparams(("arbitrary",)),
        name="route",
    )(logits_t)


def _slot_kernel(eid_ref, rank_ref, cnt_ref, slot_ref):
    ne = cnt_ref.shape[0]
    nblk = jnp.floor((cnt_ref[...] + (MOE_TILE - 1.0)) * (1.0 / MOE_TILE))
    r_i = lax.broadcasted_iota(I32, (ne, ne), 0)
    c_i = lax.broadcasted_iota(I32, (ne, ne), 1)
    tri = jnp.where(c_i < r_i, 1.0, 0.0).astype(BF16)
    start = jnp.dot(tri, nblk.astype(BF16), preferred_element_type=F32) * MOE_TILE
    eid = eid_ref[...]
    acc = rank_ref[...]
    for e in range(ne):
        acc = acc + jnp.where(eid == e, start[e:e + 1, 0:1].astype(I32), 0)
    slot_ref[...] = acc


def _slots(eid, rank, cnt):
    return pl.pallas_call(
        _slot_kernel,
        out_shape=jax.ShapeDtypeStruct(eid.shape, I32),
        compiler_params=pltpu.CompilerParams(vmem_limit_bytes=V7X_VMEM_LIMIT_BYTES),
        name="slots",
    )(eid, rank, cnt)


def _dispatch_copy(slot_ref, h_ref, xs_ref, sem, k, t, base, n_tok):
    d0 = pl.multiple_of(slot_ref[k * n_tok + base + t] * TOKEN_ROWS, TOKEN_ROWS)
    s0 = pl.multiple_of(t * TOKEN_ROWS, TOKEN_ROWS)
    return pltpu.make_async_copy(h_ref.at[pl.ds(s0, TOKEN_ROWS), :], xs_ref.at[pl.ds(d0, TOKEN_ROWS), :], sem)


def _zero_copy(zbuf, xs_ref, zsem, row, n_rows):
    d0 = pl.multiple_of(row * TOKEN_ROWS, TOKEN_ROWS)
    return pltpu.make_async_copy(zbuf.at[pl.ds(0, n_rows * TOKEN_ROWS), :],
                                 xs_ref.at[pl.ds(d0, n_rows * TOKEN_ROWS), :], zsem)


def _zero_fill(start_ref, cnt_ref, zbuf, xs_ref, zsem, p_rows, begin):
    ne = start_ref.shape[0]

    def op(cp):
        cp.start() if begin else cp.wait()

    def per_expert(e, used):
        n = cnt_ref[e]
        s0 = start_ref[e]
        c_step = (n + MOE_STEP - 1) // MOE_STEP * MOE_STEP
        c_tile = (n + MOE_TILE - 1) // MOE_TILE * MOE_TILE

        def one_row(r, c):
            op(_zero_copy(zbuf, xs_ref, zsem, s0 + r, 1))
            return c

        def one_unit(u, c):
            op(_zero_copy(zbuf, xs_ref, zsem, s0 + u * MOE_STEP, MOE_STEP))
            return c

        lax.fori_loop(n, c_step, one_row, 0)
        lax.fori_loop(c_step // MOE_STEP, c_tile // MOE_STEP, one_unit, 0)
        return s0 + c_tile

    used = lax.fori_loop(0, ne, per_expert, 0)

    def trailing(u, c):
        op(_zero_copy(zbuf, xs_ref, zsem, u * MOE_STEP, MOE_STEP))
        return c

    lax.fori_loop(used // MOE_STEP, p_rows // MOE_STEP, trailing, 0)


def _dispatch_kernel(slot_ref, start_ref, cnt_ref, h_ref, xs_ref, zbuf, sem, zsem, *, tm, n_tok, p_rows):
    i = pl.program_id(0)
    base = i * tm

    @pl.when(i == 0)
    def _():
        zbuf[...] = jnp.zeros_like(zbuf)
        _zero_fill(start_ref, cnt_ref, zbuf, xs_ref, zsem, p_rows, True)

    def start(t, carry):
        for k in range(TOP_K):
            _dispatch_copy(slot_ref, h_ref, xs_ref, sem, k, t, base, n_tok).start()
        return carry

    def wait(t, carry):
        for k in range(TOP_K):
            _dispatch_copy(slot_ref, h_ref, xs_ref, sem, k, t, base, n_tok).wait()
        return carry

    lax.fori_loop(0, tm, start, 0, unroll=4)
    lax.fori_loop(0, tm, wait, 0, unroll=4)

    @pl.when(i == pl.num_programs(0) - 1)
    def _():
        _zero_fill(start_ref, cnt_ref, zbuf, xs_ref, zsem, p_rows, False)


def _dispatch(h2p, slot, start_pad, counts, p_rows):
    n_tok = h2p.shape[0] // TOKEN_ROWS
    tm = 512
    grid_spec = pltpu.PrefetchScalarGridSpec(
        num_scalar_prefetch=3,
        grid=(n_tok // tm,),
        in_specs=[pl.BlockSpec((tm * TOKEN_ROWS, LANES), lambda i, *_: (i, 0))],
        out_specs=pl.BlockSpec(memory_space=pl.ANY),
        scratch_shapes=[pltpu.VMEM((MOE_STEP * TOKEN_ROWS, LANES), h2p.dtype),
                        pltpu.SemaphoreType.DMA(()), pltpu.SemaphoreType.DMA(())],
    )
    return pl.pallas_call(
        functools.partial(_dispatch_kernel, tm=tm, n_tok=n_tok, p_rows=p_rows),
        grid_spec=grid_spec,
        out_shape=jax.ShapeDtypeStruct((p_rows * TOKEN_ROWS, LANES), h2p.dtype),
        compiler_params=_cparams(("arbitrary",)),
        name="dispatch",
    )(slot, start_pad, counts, h2p)


def _weight_copies(w_hbm, wbuf, sem, e, col_starts):
    tn = wbuf.shape[-1]
    aligned = lambda c: c if isinstance(c, int) else pl.multiple_of(c, LANES)
    return [pltpu.make_async_copy(w_hbm.at[e, :, pl.ds(aligned(c), tn)], wbuf.at[i], sem)
            for i, c in enumerate(col_starts)]


def _await_run_weights(w_hbm, wbuf, sem, e_ref, j, it, col_starts_of):
    @pl.when((j == 0) & (it == 0))
    def _():
        for cp in _weight_copies(w_hbm, wbuf, sem, e_ref[it], col_starts_of(j)):
            cp.start()

    for cp in _weight_copies(w_hbm, wbuf, sem, e_ref[it], col_starts_of(j)):
        cp.wait()


def _prefetch_next_run(w_hbm, wbuf, sem, last_ref, nxt_ref, j, nj, it, col_starts_of):
    is_last = last_ref[it] == 1

    @pl.when(jnp.logical_not(is_last))
    def _():
        for cp in _weight_copies(w_hbm, wbuf, sem, nxt_ref[it], col_starts_of(j)):
            cp.start()

    @pl.when(is_last & (j + 1 < nj))
    def _():
        for cp in _weight_copies(w_hbm, wbuf, sem, nxt_ref[it], col_starts_of(j + 1)):
            cp.start()


def _moe_up_kernel(blk_ref, oblk_ref, e_ref, nv_ref, new_ref, run_ref, last_ref, nxt_ref, nruns_ref,
                   xs_ref, w_hbm, bg_ref, bl_ref, act_ref, wbuf, sem, wg_bf, wl_bf):
    j = pl.program_id(0)
    nj = pl.num_programs(0)
    it = pl.program_id(1)
    tn = wg_bf.shape[1]
    is_new = new_ref[it] == 1
    cols = lambda jj: (jj * tn, (nj + jj) * tn)

    @pl.when(is_new)
    def _():
        _await_run_weights(w_hbm, wbuf, sem, e_ref, j, it, cols)

    def cast_weights():
        wg_bf[...] = wbuf[0].astype(BF16)
        wl_bf[...] = wbuf[1].astype(BF16)

    def rows_chunk(r0, n):
        x = _load_token_major(xs_ref, r0, n).astype(BF16)
        hg = jnp.dot(x, wg_bf[...], preferred_element_type=F32) + bg_ref[...]
        hl = jnp.dot(x, wl_bf[...], preferred_element_type=F32) + bl_ref[...]
        glu = jnp.minimum(hg, SWIGLU_LIMIT)
        lin = jnp.clip(hl, -SWIGLU_LIMIT, SWIGLU_LIMIT)
        act_ref[r0:r0 + n, :] = ((0.5 * glu) * (1.0 + jnp.tanh((0.5 * SWIGLU_ALPHA) * glu)) * (lin + 1.0)).astype(BF16)

    def rows_zero(r0, n):
        act_ref[r0:r0 + n, :] = jnp.zeros((n, act_ref.shape[1]), BF16)

    _moe_row_variants(nv_ref[it], is_new, cast_weights, rows_chunk, rows_zero)

    @pl.when(is_new)
    def _():
        _prefetch_next_run(w_hbm, wbuf, sem, last_ref, nxt_ref, j, nj, it, cols)


def _moe_row_variants(nv, is_new, cast_weights, rows_chunk, rows_zero):
    for fresh in (False, True):
        for units in range(MOE_TILE // MOE_STEP + 1):
            rows = units * MOE_STEP
            in_range = ((nv > rows - MOE_STEP) & (nv <= rows)) if units else (nv <= 0)

            @pl.when(in_range & (is_new if fresh else jnp.logical_not(is_new)))
            def _():
                if fresh:
                    cast_weights()
                r0 = 0
                while r0 < rows:
                    n = min(MOE_HALF, rows - r0)
                    rows_chunk(r0, n)
                    r0 += n
                if rows < MOE_TILE:
                    rows_zero(rows, MOE_TILE - rows)


def _moe_down_kernel(blk_ref, oblk_ref, e_ref, nv_ref, new_ref, run_ref, last_ref, nxt_ref, nruns_ref,
                     a_ref, w_hbm, b_ref, y_ref, wbuf, sem, w_bf):
    it = pl.program_id(0)
    is_new = new_ref[it] == 1
    cols = lambda jj: (0,)

    @pl.when(is_new)
    def _():
        _await_run_weights(w_hbm, wbuf, sem, e_ref, 0, it, cols)

    def cast_weights():
        w_bf[...] = wbuf[0].astype(BF16)

    def rows_chunk(r0, n):
        y = jnp.dot(a_ref[r0:r0 + n, :], w_bf[...], preferred_element_type=F32) + b_ref[...]
        _store_token_major(y_ref, r0, y)

    def rows_zero(r0, n):
        y_ref[r0 * TOKEN_ROWS:(r0 + n) * TOKEN_ROWS, :] = jnp.zeros((n * TOKEN_ROWS, LANES), F32)

    _moe_row_variants(nv_ref[it], is_new, cast_weights, rows_chunk, rows_zero)

    @pl.when(is_new)
    def _():
        _prefetch_next_run(w_hbm, wbuf, sem, last_ref, nxt_ref, 0, 1, it, cols)


def _moe_up(items, xs, w1, b1):
    n_items = items[0].shape[0]
    ne, d, two_f = w1.shape
    f = two_f // 2
    nj = f // MOE_TN
    p = xs.shape[0] // TOKEN_ROWS
    b1r = b1.reshape(ne, 1, two_f)
    grid_spec = pltpu.PrefetchScalarGridSpec(
        num_scalar_prefetch=len(items),
        grid=(nj, n_items),
        in_specs=[pl.BlockSpec((MOE_TILE * TOKEN_ROWS, LANES), lambda j, it, blk, *_: (blk[it], 0)),
                  pl.BlockSpec(memory_space=pl.ANY),
                  pl.BlockSpec((None, 1, MOE_TN), lambda j, it, blk, oblk, e, *_: (e[it], 0, j)),
                  pl.BlockSpec((None, 1, MOE_TN), lambda j, it, blk, oblk, e, *_: (e[it], 0, nj + j))],
        out_specs=pl.BlockSpec((MOE_TILE, MOE_TN), lambda j, it, blk, oblk, *_: (oblk[it], j)),
        scratch_shapes=[pltpu.VMEM((2, d, MOE_TN), F32), pltpu.SemaphoreType.DMA(()),
                        pltpu.VMEM((d, MOE_TN), BF16), pltpu.VMEM((d, MOE_TN), BF16)],
    )
    return pl.pallas_call(
        _moe_up_kernel,
        grid_spec=grid_spec,
        out_shape=jax.ShapeDtypeStruct((p, f), BF16),
        compiler_params=_cparams(("arbitrary", "arbitrary")),
        name="moe_up",
    )(*items, xs, w1, b1r, b1r)


def _moe_down(items, act, w2, b2):
    n_items = items[0].shape[0]
    ne, f, d = w2.shape
    p = act.shape[0]
    b2r = b2.reshape(ne, 1, d)
    grid_spec = pltpu.PrefetchScalarGridSpec(
        num_scalar_prefetch=len(items),
        grid=(n_items,),
        in_specs=[pl.BlockSpec((MOE_TILE, f), lambda it, blk, *_: (blk[it], 0)),
                  pl.BlockSpec(memory_space=pl.ANY),
                  pl.BlockSpec((None, 1, d), lambda it, blk, oblk, e, *_: (e[it], 0, 0))],
        out_specs=pl.BlockSpec((MOE_TILE * TOKEN_ROWS, LANES), lambda it, blk, oblk, *_: (oblk[it], 0)),
        scratch_shapes=[pltpu.VMEM((1, f, d), F32), pltpu.SemaphoreType.DMA(()), pltpu.VMEM((f, d), BF16)],
    )
    return pl.pallas_call(
        _moe_down_kernel,
        grid_spec=grid_spec,
        out_shape=jax.ShapeDtypeStruct((p * TOKEN_ROWS, LANES), F32),
        compiler_params=_cparams(("arbitrary",)),
        name="moe_down",
    )(*items, act, w2, b2r)


FINAL_ROWS = 64


def _combine_copy(slot_ref, y_ref, buf, sems, k, t, tile, par, tm, n_tok):
    s0 = pl.multiple_of(slot_ref[k * n_tok + tile * tm + t] * TOKEN_ROWS, TOKEN_ROWS)
    d0 = pl.multiple_of(t * TOKEN_ROWS, TOKEN_ROWS)
    return pltpu.make_async_copy(y_ref.at[pl.ds(s0, TOKEN_ROWS), :], buf.at[par, k, pl.ds(d0, TOKEN_ROWS), :],
                                 sems.at[par])


def _final_kernel(slot_ref, y_ref, gates_ref, x1_ref, gate_ref, g_ref, o_ref, buf, sems, *, n_tok):
    tm, d = x1_ref.shape
    i = pl.program_id(0)
    par = i % 2

    def gather(tile, par_, start):
        def body(t, carry):
            for k in range(TOP_K):
                cp = _combine_copy(slot_ref, y_ref, buf, sems, k, t, tile, par_, tm, n_tok)
                cp.start() if start else cp.wait()
            return carry
        lax.fori_loop(0, tm, body, 0, unroll=4)

    @pl.when(i == 0)
    def _():
        gather(0, 0, True)

    @pl.when(i + 1 < pl.num_programs(0))
    def _():
        gather(i + 1, 1 - par, True)

    gather(i, par, False)

    for grp in range(tm // FINAL_ROWS):
        r0 = grp * FINAL_ROWS
        rs = slice(r0, r0 + FINAL_ROWS)
        gk = [jnp.broadcast_to(gates_ref[rs, k:k + 1], (FINAL_ROWS, LANES)) for k in range(TOP_K)]
        sq = jnp.zeros((FINAL_ROWS, LANES), F32)
        for s in range(TOKEN_ROWS):
            cs = slice(s * LANES, (s + 1) * LANES)
            rows = pl.ds(r0 * TOKEN_ROWS + s, FINAL_ROWS, stride=TOKEN_ROWS)
            acc = gk[0] * buf[par, 0, rows, :]
            for k in range(1, TOP_K):
                acc = acc + gk[k] * buf[par, k, rows, :]
            xf = x1_ref[rs, cs] + gate_ref[:, cs] * acc
            o_ref[rs, cs] = xf
            sq = sq + xf * xf
        scale = lax.rsqrt(jnp.sum(sq, axis=-1, keepdims=True) * (1.0 / d) + EPS)
        for s in range(TOKEN_ROWS):
            cs = slice(s * LANES, (s + 1) * LANES)
            o_ref[rs, cs] = o_ref[rs, cs] * scale * g_ref[:, cs]


def _final(slot, y, gates_tk, x1, mod3, mod_row_of_tile, final_g, tm):
    r, d = x1.shape
    grid_spec = pltpu.PrefetchScalarGridSpec(
        num_scalar_prefetch=1,
        grid=(r // tm,),
        in_specs=[pl.BlockSpec(memory_space=pl.ANY),
                  pl.BlockSpec((tm, TOP_K), lambda i, slot: (i, 0)),
                  pl.BlockSpec((tm, d), lambda i, slot: (i, 0)),
                  pl.BlockSpec((None, 1, d), lambda i, slot: (mod_row_of_tile(i), 0, 5)),
                  pl.BlockSpec((1, d), lambda i, slot: (0, 0))],
        out_specs=pl.BlockSpec((tm, d), lambda i, slot: (i, 0)),
        scratch_shapes=[pltpu.VMEM((2, TOP_K, tm * TOKEN_ROWS, LANES), F32), pltpu.SemaphoreType.DMA((2,))],
    )
    return pl.pallas_call(
        functools.partial(_final_kernel, n_tok=r),
        grid_spec=grid_spec,
        out_shape=jax.ShapeDtypeStruct((r, d), F32),
        compiler_params=_cparams(("arbitrary",)),
        name="final",
    )(slot, y, gates_tk, x1, mod3, final_g)


def _moe_schedule(counts, n_items):
    ne = counts.shape[0]
    padded = (counts + MOE_TILE - 1) // MOE_TILE * MOE_TILE
    pad_end = jnp.cumsum(padded)
    start_pad = pad_end - padded
    n_real = pad_end[-1] // MOE_TILE
    b = jnp.arange(n_items, dtype=I32)
    blk = jnp.minimum(b, jnp.maximum(n_real - 1, 0))
    blk_start = blk * MOE_TILE
    e_of = jnp.minimum(jnp.sum((pad_end[None, :] <= blk_start[:, None]).astype(I32), axis=1), ne - 1)
    nv = jnp.clip(counts[e_of] - (blk_start - start_pad[e_of]), 0, MOE_TILE)
    nv = jnp.where(b < n_real, nv, 0)
    e_of = e_of.astype(I32)
    prev_e = jnp.concatenate([jnp.full((1,), -1, I32), e_of[:-1]])
    new = ((e_of != prev_e) | (b == 0)).astype(I32)
    run = jnp.cumsum(new) - 1
    n_runs = run[-1] + 1
    last = (run == n_runs - 1).astype(I32)
    first_pos = jnp.where(new == 1, b, n_items)
    nxt_pos = jnp.concatenate([lax.cummin(first_pos, reverse=True)[1:], jnp.full((1,), n_items, I32)])
    nxt = jnp.where(nxt_pos < n_items, e_of[jnp.minimum(nxt_pos, n_items - 1)], e_of[0])
    return start_pad.astype(I32), (blk.astype(I32), b, e_of, nv.astype(I32), new, run.astype(I32), last,
                                   nxt.astype(I32), n_runs.reshape(1).astype(I32))


def kernel(x, c, ctx, c_ctx, mod_w, mod_b, norm1_g, norm2_g, w_in, gate_b, conv_w, conv_b, mlstm_norm_g,
           sgu_norm_g, sgu_w, sgu_b, proj_a, proj_b, w_out, router_w, router_b, exp_w1, exp_b1, exp_w2,
           exp_b2, final_g):
    bsz, t, d = x.shape
    tc = ctx.shape[1]
    depth = mod_w.shape[0]
    assert depth == 1, "single-layer block"
    assert d == TOKEN_ROWS * LANES, "token-major rows hold TOKEN_ROWS*128 features"
    l = 0
    r = bsz * t
    qkw = 2 * M_HEADS * QK_DIM
    vw = M_HEADS * V_DIM
    off_gates = qkw + vw
    n_gates = 4 * M_HEADS
    off_o = off_gates + n_gates

    cc = jnp.zeros((8, d), F32).at[:bsz].set(c).at[bsz].set(c_ctx)
    mod3 = _mod(cc, mod_w[l], mod_b[l]).reshape(8, 1, N_MOD * d)

    in_cols = w_in.shape[2]
    assert off_gates % INPROJ_TN == 0 and (in_cols - off_o) % INPROJ_TN == 0
    n_aligned = off_gates // INPROJ_TN
    n_tiles = n_aligned + (in_cols - off_o) // INPROJ_TN
    w_in_t = jnp.swapaxes(w_in, 1, 2)
    w_gates = jnp.pad(w_in_t[l, off_gates:off_o, :].T, ((0, 0), (0, LANES - n_gates)))
    gb = jnp.pad(gate_b[l].reshape(1, n_gates), ((0, 0), (0, LANES - n_gates)))
    g1 = norm1_g[l].reshape(1, d)

    tm_in = 512
    h_lat, g_lat = _norm(x.reshape(r, d), mod3, lambda i: i // (t // tm_in), g1, w_gates, gb, tm_in)
    h_ctx, g_ctx = _norm(ctx.reshape(bsz * tc, d), mod3, lambda i: bsz, g1, w_gates, gb, tc)
    z_lat = _inproj(h_lat, w_in_t, l, n_aligned, n_gates, n_tiles, 2048)
    z_ctx = _inproj(h_ctx, w_in_t, l, n_aligned, n_gates, n_aligned, bsz * tc)

    z_lat3 = z_lat.reshape(bsz, t, z_lat.shape[1])
    z_ctx3 = z_ctx.reshape(bsz, tc, z_ctx.shape[1])
    qk_lat = _conv(z_lat3, conv_w[l], conv_b[l])
    qk_ctx = _conv(z_ctx3, conv_w[l], conv_b[l])

    hf, hb = _mlstm(qk_ctx, qk_lat, z_ctx3, z_lat3, g_ctx.reshape(bsz, tc, LANES), g_lat.reshape(bsz, t, LANES))

    merged = _merge(hf.reshape(r, vw), hb.reshape(r, vw), z_lat,
                    mlstm_norm_g[l].reshape(1, vw), sgu_norm_g[l].reshape(1, -1),
                    sgu_w[l].astype(BF16), sgu_b[l].T, proj_a[l].astype(BF16), proj_b[l].astype(BF16))

    tm_out = 512
    x1, h2p, logits_t = _outproj(merged, w_out[l].astype(BF16), x.reshape(r, d), mod3,
                                 lambda i: i // (t // tm_out), norm2_g[l].reshape(1, d),
                                 router_w[l], router_b[l], tm_out)
    tm_fin = 256

    eid, gates, rank, cnt = _route(logits_t)
    counts = cnt[:, 0].astype(I32)
    n_items = (r * TOP_K) // MOE_TILE + N_EXPERTS
    start_pad, items = _moe_schedule(counts, n_items)
    slot = _slots(eid, rank, cnt).reshape(-1)
    xs = _dispatch(h2p, slot, start_pad, counts, n_items * MOE_TILE)
    act = _moe_up(items, xs, exp_w1[l], exp_b1[l])
    y = _moe_down(items, act, exp_w2[l], exp_b2[l])
    out = _final(slot, y, gates.T, x1, mod3, lambda i: i // (t // tm_fin), final_g.reshape(1, d), tm_fin)
    return out.reshape(bsz, t, d)
```

```python
import functools

import jax
import jax.numpy as jnp
from jax import lax
from jax.experimental import pallas as pl
from jax.experimental.pallas import tpu as pltpu

F32 = jnp.float32
BF16 = jnp.bfloat16
I32 = jnp.int32
U32 = jnp.uint32

EPS = 1e-6
M_HEADS = 8
QK_DIM = 128
V_DIM = 256
CONV_W = 5
CHUNK = 128
SGU_GROUPS = 8
N_EXPERTS = 32
TOP_K = 4
SWIGLU_ALPHA = 1.702
SWIGLU_LIMIT = 7.0
N_MOD = 6

V7X_VMEM_LIMIT_BYTES = 56 * 1024 * 1024
LANES = 128
MOE_TILE = 512
MOE_HALF = 256
MOE_STEP = 128
MOE_TN = 1024


V7X_MOE_VMEM_LIMIT_BYTES = 60 * 1024 * 1024


def _cparams(sem, limit=V7X_VMEM_LIMIT_BYTES):
    return pltpu.CompilerParams(dimension_semantics=sem, vmem_limit_bytes=limit)


def _sigmoid(x):
    return 1.0 / (1.0 + jnp.exp(-x))


def _gelu_tanh(x):
    return 0.5 * x * (1.0 + jnp.tanh(0.7978845608028654 * (x + 0.044715 * (x * x * x))))


def _split3(a):
    a1 = a.astype(BF16)
    r1 = a - a1.astype(F32)
    a2 = r1.astype(BF16)
    a3 = (r1 - a2.astype(F32)).astype(BF16)
    return a1, a2, a3


TOKEN_ROWS = 16


def _store_token_major(ref, row0, vals):
    n = vals.shape[0]
    for s in range(TOKEN_ROWS):
        ref[pl.ds(row0 * TOKEN_ROWS + s, n, stride=TOKEN_ROWS), :] = vals[:, s * LANES:(s + 1) * LANES]


def _load_token_major(ref, row0, n, lead=()):
    cols = [ref[(*lead, pl.ds(row0 * TOKEN_ROWS + s, n, stride=TOKEN_ROWS), slice(None))] for s in range(TOKEN_ROWS)]
    return jnp.concatenate(cols, axis=1)


def _mod_kernel(s_ref, w_ref, b_ref, o_ref):
    s = s_ref[...]
    s = s * _sigmoid(s)
    o_ref[...] = jnp.dot(s.astype(BF16), w_ref[...].astype(BF16), preferred_element_type=F32) + b_ref[...]


def _mod(cc, mod_w, mod_b):
    d, n = mod_w.shape
    tn = 1024
    return pl.pallas_call(
        _mod_kernel,
        grid=(n // tn,),
        in_specs=[pl.BlockSpec((8, d), lambda j: (0, 0)),
                  pl.BlockSpec((d, tn), lambda j: (0, j)),
                  pl.BlockSpec((1, tn), lambda j: (0, j))],
        out_specs=pl.BlockSpec((8, tn), lambda j: (0, j)),
        out_shape=jax.ShapeDtypeStruct((8, n), F32),
        compiler_params=_cparams(("arbitrary",)),
        name="mod",
    )(cc, mod_w, mod_b.reshape(1, n))


def _norm_kernel(x_ref, shift_ref, scale_ref, g_ref, wg_ref, gb_ref, h_ref, gates_ref):
    x = x_ref[...]
    ms = jnp.mean(x * x, axis=-1, keepdims=True)
    h = (x * lax.rsqrt(ms + EPS) * g_ref[...]) * (1.0 + scale_ref[...]) + shift_ref[...]
    h1, h2, _ = _split3(h)
    h_ref[...] = h1
    w1, w2, _ = _split3(wg_ref[...])
    gates_ref[...] = (jnp.dot(h1, w1, preferred_element_type=F32)
                      + jnp.dot(h1, w2, preferred_element_type=F32)
                      + jnp.dot(h2, w1, preferred_element_type=F32)) + gb_ref[...]


def _norm(x2d, mod3, mod_row_of_tile, norm_g, w_gates, gate_b, tm):
    r, d = x2d.shape
    return pl.pallas_call(
        _norm_kernel,
        grid=(r // tm,),
        in_specs=[pl.BlockSpec((tm, d), lambda i: (i, 0)),
                  pl.BlockSpec((None, 1, d), lambda i: (mod_row_of_tile(i), 0, 0)),
                  pl.BlockSpec((None, 1, d), lambda i: (mod_row_of_tile(i), 0, 1)),
                  pl.BlockSpec((1, d), lambda i: (0, 0)),
                  pl.BlockSpec((d, LANES), lambda i: (0, 0)),
                  pl.BlockSpec((1, LANES), lambda i: (0, 0))],
        out_specs=[pl.BlockSpec((tm, d), lambda i: (i, 0)),
                   pl.BlockSpec((tm, LANES), lambda i: (i, 0))],
        out_shape=[jax.ShapeDtypeStruct((r, d), BF16),
                   jax.ShapeDtypeStruct((r, LANES), F32)],
        compiler_params=_cparams(("arbitrary",)),
        name="norm",
    )(x2d, mod3, mod3, norm_g, w_gates, gate_b)


INPROJ_TN = 1024
INPROJ_ROWS = 256


def _inproj_kernel(h_ref, wt_ref, z_ref, w_bf, *, tiles_per_seg):
    j = pl.program_id(0)
    i = pl.program_id(1)

    @pl.when(i == 0)
    def _():
        w_bf[...] = wt_ref[...].T.astype(BF16)

    seg = j // tiles_per_seg
    is_raw = seg < 2
    is_gelu = (seg == 3) | (seg == 4)
    c0 = 0.7978845608028654
    p0 = jnp.where(is_gelu, c0, 0.5).astype(F32)
    p1 = jnp.where(is_gelu, c0 * 0.044715, 0.0).astype(F32)
    q0 = jnp.where(is_gelu, 0.0, 1.0).astype(F32)
    q1 = jnp.where(is_gelu, 1.0, 0.0).astype(F32)
    w = w_bf[...]
    for c in range(h_ref.shape[0] // INPROJ_ROWS):
        rs = slice(c * INPROJ_ROWS, (c + 1) * INPROJ_ROWS)
        z = jnp.dot(h_ref[rs, :], w, preferred_element_type=F32)
        act = (0.5 * (q0 + q1 * z)) * (1.0 + jnp.tanh(z * (p0 + p1 * (z * z))))
        z_ref[rs, :] = jnp.where(is_raw, z, act).astype(BF16)


def _inproj(h, w_in_t, layer, n_aligned, shift, n_tiles, tm):
    r, d = h.shape
    tn = INPROJ_TN
    kern = functools.partial(_inproj_kernel, tiles_per_seg=d // tn)
    assert shift % 8 == 0
    row0 = lambda j: pl.multiple_of(j * tn + jnp.where(j >= n_aligned, shift, 0), 8)
    return pl.pallas_call(
        kern,
        grid=(n_tiles, r // tm),
        in_specs=[pl.BlockSpec((tm, d), lambda j, i: (i, 0)),
                  pl.BlockSpec((None, pl.Element(tn), pl.Element(d)), lambda j, i: (layer, row0(j), 0))],
        out_specs=pl.BlockSpec((tm, tn), lambda j, i: (i, j)),
        out_shape=jax.ShapeDtypeStruct((r, n_tiles * tn), BF16),
        scratch_shapes=[pltpu.VMEM((d, tn), BF16)],
        compiler_params=_cparams(("arbitrary", "arbitrary")),
        name="inproj",
    )(h, w_in_t)


def _conv_kernel(x_ref, w_ref, b_ref, o_ref, *, q_tiles):
    c = pl.program_id(1)
    x = x_ref[...].astype(F32)
    t = x.shape[0]
    rows = lax.broadcasted_iota(I32, x.shape, 0)
    w = w_ref[...]
    acc = x * w[CONV_W // 2:CONV_W // 2 + 1, :] + b_ref[...]
    for dlt in range(-(CONV_W // 2), CONV_W // 2 + 1):
        if dlt == 0:
            continue
        xs = pltpu.roll(x, shift=(-dlt) % t, axis=0)
        valid = (rows + dlt >= 0) & (rows + dlt < t)
        acc = acc + jnp.where(valid, xs, 0.0) * w[dlt + CONV_W // 2:dlt + CONV_W // 2 + 1, :]
    y = acc * _sigmoid(acc)
    scale = jnp.where(c < q_tiles, QK_DIM ** -0.5, 1.0).astype(F32)
    o_ref[...] = (y * scale).astype(BF16)


def _conv(z3, conv_w, conv_b):
    b, t, _ = z3.shape
    width = conv_w.shape[1]
    tc = 256
    kern = functools.partial(_conv_kernel, q_tiles=(width // 2) // tc)
    return pl.pallas_call(
        kern,
        grid=(b, width // tc),
        in_specs=[pl.BlockSpec((None, t, tc), lambda i, c: (i, 0, c)),
                  pl.BlockSpec((CONV_W, tc), lambda i, c: (0, c)),
                  pl.BlockSpec((1, tc), lambda i, c: (0, c))],
        out_specs=pl.BlockSpec((None, t, tc), lambda i, c: (i, 0, c)),
        out_shape=jax.ShapeDtypeStruct((b, t, width), BF16),
        compiler_params=_cparams(("arbitrary", "arbitrary")),
        name="conv",
    )(z3, conv_w, conv_b.reshape(1, width))


def _mlstm_kernel(qkc_f, qkl_f, vc_f, vl_f, gc_f, gl_f, qkc_b, qkl_b, vc_b, vl_b, gc_b, gl_b,
                  hf_ref, hb_ref, *state, n_ctx_chunks):
    c_scr, m_scr = state[:2 * M_HEADS], state[2 * M_HEADS:]
    s = pl.program_id(1)

    @pl.when(s == 0)
    def _():
        for ref in state:
            ref[...] = jnp.zeros_like(ref)

    is_ctx = s < n_ctx_chunks
    rows = lax.broadcasted_iota(I32, (CHUNK, CHUNK), 0)
    lanes = lax.broadcasted_iota(I32, (CHUNK, CHUNK), 1)
    ones_col = jnp.where(lanes == 0, 1.0, 0.0).astype(BF16)
    dn_nt = (((1,), (1,)), ((), ()))
    dn_tn = (((0,), (0,)), ((), ()))
    qk_w = M_HEADS * QK_DIM

    dirs = ((qkc_f, qkl_f, vc_f, vl_f, gc_f, gl_f, hf_ref), (qkc_b, qkl_b, vc_b, vl_b, gc_b, gl_b, hb_ref))
    per_dir = []
    for d, (qkc, qkl, vc, vl, gc, gl, out_ref) in enumerate(dirs):
        qk = jnp.where(is_ctx, qkc[...], qkl[...])
        v = jnp.where(is_ctx, vc[...], vl[...])
        g = jnp.where(is_ctx, gc[...], gl[...])
        mask = (lanes <= rows) if d == 0 else (lanes >= rows)
        tri = jnp.where(mask, 1.0, 0.0).astype(BF16)
        ls = jnp.minimum(g, 0.0) - jnp.log(1.0 + jnp.exp(-jnp.abs(g)))
        l1, l2, l3 = _split3(ls)
        bcol = (jnp.dot(tri, l1, preferred_element_type=F32) + jnp.dot(tri, l2, preferred_element_type=F32)
                + jnp.dot(tri, l3, preferred_element_type=F32))
        brow, grow = bcol.T, g.T
        c0 = 2 * M_HEADS * d
        ib = grow[c0:c0 + M_HEADS, :] - brow[c0 + M_HEADS:c0 + 2 * M_HEADS, :]
        lane_h = lax.broadcasted_iota(I32, ib.shape, 1)
        rmax = ib
        step = 1
        while step < CHUNK:
            if d == 0:
                prev = jnp.where(lane_h >= step, pltpu.roll(rmax, step, axis=1), -jnp.inf)
            else:
                prev = jnp.where(lane_h < CHUNK - step, pltpu.roll(rmax, CHUNK - step, axis=1), -jnp.inf)
            rmax = jnp.maximum(rmax, prev)
            step *= 2
        rmax_col = jnp.concatenate([rmax, jnp.zeros((CHUNK - M_HEADS, CHUNK), F32)], axis=0).T
        per_dir.append((d, qk, v, g, mask, bcol, ib, rmax_col, out_ref))

    units = [(pd, h) for pd in per_dir for h in range(M_HEADS)]
    stage1 = []
    for (d, qk, v, g, mask, bcol, ib, rmax_col, out_ref), h in units:
        col_i = 2 * M_HEADS * d + h
        col_f = col_i + M_HEADS
        idx = d * M_HEADS + h
        end = CHUNK - 1 if d == 0 else 0
        bc = bcol[:, col_f:col_f + 1]
        ic = g[:, col_i:col_i + 1]
        ibr = ib[h:h + 1, :]
        m = m_scr[idx][0:1, 0:1]
        inter = bc + m
        m_t = jnp.maximum(inter, bc + rmax_col[:, h:h + 1])
        w_inter = jnp.exp(inter - m_t)
        b_end = bc[end:end + 1, :]
        m_new = jnp.maximum(b_end + m, b_end + rmax_col[end:end + 1, h:h + 1])
        decay = jnp.exp(b_end + m - m_new)
        w_col = jnp.exp(b_end - bc + ic - m_new)
        q = qk[:, h * QK_DIM:(h + 1) * QK_DIM]
        k = qk[:, qk_w + h * QK_DIM:qk_w + (h + 1) * QK_DIM]
        decay_mat = jnp.exp(jnp.where(mask, (bc - m_t) + ibr, -jnp.inf))
        sc = lax.dot_general(q, k, dn_nt, preferred_element_type=F32) * decay_mat
        lhs = jnp.concatenate([(q.astype(F32) * w_inter).astype(BF16), sc.astype(BF16)], axis=1)
        kw = (k.astype(F32) * w_col).astype(BF16)
        stage1.append((lhs, kw, m_t, m_new, decay))

    stage2 = []
    for ((d, qk, v, g, mask, bcol, brow, grow, out_ref), h), (lhs, kw, m_t, m_new, decay) in zip(units, stage1):
        idx = d * M_HEADS + h
        vaug = jnp.concatenate([v[:, h * V_DIM:(h + 1) * V_DIM], ones_col], axis=1)
        caug = c_scr[idx][...]
        num = jnp.dot(lhs, jnp.concatenate([caug.astype(BF16), vaug], axis=0), preferred_element_type=F32)
        upd = lax.dot_general(kw, vaug, dn_tn, preferred_element_type=F32)
        stage2.append((num, upd, caug))

    for ((d, qk, v, g, mask, bcol, brow, grow, out_ref), h), (lhs, kw, m_t, m_new, decay), (num, upd, caug) in zip(
            units, stage1, stage2):
        idx = d * M_HEADS + h
        den = num[:, V_DIM:V_DIM + 1]
        hout = num[:, :V_DIM] / jnp.maximum(jnp.abs(den), jnp.exp(-m_t))
        c_scr[idx][...] = decay * caug + upd
        m_scr[idx][...] = jnp.broadcast_to(m_new, m_scr[idx].shape)
        out_ref[:, h * V_DIM:(h + 1) * V_DIM] = hout.astype(out_ref.dtype)


def _mlstm(qk_ctx, qk_lat, z_ctx3, z_lat3, g_ctx3, g_lat3):
    b, tc, _ = qk_ctx.shape
    t = qk_lat.shape[1]
    nc, nl = tc // CHUNK, t // CHUNK
    vw = M_HEADS * V_DIM
    qkw = 2 * M_HEADS * QK_DIM

    f_ctx = lambda i, s: (i, jnp.minimum(s, nc - 1), 0)
    f_lat = lambda i, s: (i, jnp.clip(s - nc, 0, nl - 1), 0)
    b_ctx = lambda i, s: (i, jnp.maximum(nc - 1 - s, 0), 0)
    b_lat = lambda i, s: (i, jnp.clip(nc + nl - 1 - s, 0, nl - 1), 0)
    v_of = lambda f: (lambda i, s: (f(i, s)[0], f(i, s)[1], 1))

    def specs(fc, fl):
        return [pl.BlockSpec((None, CHUNK, qkw), fc), pl.BlockSpec((None, CHUNK, qkw), fl),
                pl.BlockSpec((None, CHUNK, vw), v_of(fc)), pl.BlockSpec((None, CHUNK, vw), v_of(fl)),
                pl.BlockSpec((None, CHUNK, LANES), fc), pl.BlockSpec((None, CHUNK, LANES), fl)]

    kern = functools.partial(_mlstm_kernel, n_ctx_chunks=nc)
    args = (qk_ctx, qk_lat, z_ctx3, z_lat3, g_ctx3, g_lat3)
    return pl.pallas_call(
        kern,
        grid=(b, nc + nl),
        in_specs=specs(f_ctx, f_lat) + specs(b_ctx, b_lat),
        out_specs=[pl.BlockSpec((None, CHUNK, vw), f_lat), pl.BlockSpec((None, CHUNK, vw), b_lat)],
        out_shape=[jax.ShapeDtypeStruct((b, t, vw), BF16), jax.ShapeDtypeStruct((b, t, vw), BF16)],
        scratch_shapes=([pltpu.VMEM((QK_DIM, V_DIM + LANES), F32)] * (2 * M_HEADS)
                        + [pltpu.VMEM((8, LANES), F32)] * (2 * M_HEADS)),
        compiler_params=_cparams(("arbitrary", "arbitrary")),
        name="mlstm",
    )(*args, *args)


MERGE_ROWS = 256


def _merge_kernel(hf_ref, hb_ref, o_ref, u_ref, sv_ref, ga_ref, gb_ref, gm_ref, gs_ref, sw_ref, sb_ref,
                  pa_ref, pb_ref, out_ref):
    tm, d = out_ref.shape
    gw = d // SGU_GROUPS
    for grp in range(tm // MERGE_ROWS):
        r0 = grp * MERGE_ROWS
        rs = slice(r0, r0 + MERGE_ROWS)
        a_parts = []
        for h in range(M_HEADS):
            sl = slice(h * V_DIM, (h + 1) * V_DIM)
            hs = hf_ref[rs, sl].astype(F32) + hb_ref[rs, sl].astype(F32)
            ms = jnp.mean(hs * hs, axis=-1, keepdims=True)
            y = hs * lax.rsqrt(ms + EPS) * gm_ref[:, sl]
            a_parts.append((y * o_ref[rs, sl].astype(F32)).astype(BF16))
        a = jnp.concatenate(a_parts, axis=1)
        sv = sv_ref[rs, :].astype(F32)
        ms = jnp.mean(sv * sv, axis=-1, keepdims=True)
        svn = (sv * lax.rsqrt(ms + EPS) * gs_ref[...]).astype(BF16)
        b_rows = []
        for c in range(MERGE_ROWS // CHUNK):
            cr = slice(c * CHUNK, (c + 1) * CHUNK)
            b_parts = []
            for gi in range(SGU_GROUPS):
                cs = slice(gi * gw, (gi + 1) * gw)
                zz = jnp.dot(sw_ref[gi], svn[cr, cs], preferred_element_type=F32) + sb_ref[:, gi:gi + 1]
                b_parts.append((u_ref[r0 + c * CHUNK:r0 + (c + 1) * CHUNK, cs].astype(F32) * zz).astype(BF16))
            b_rows.append(jnp.concatenate(b_parts, axis=1))
        b = jnp.concatenate(b_rows, axis=0)
        ya = jnp.dot(a, pa_ref[...], preferred_element_type=F32)
        yb = jnp.dot(b, pb_ref[...], preferred_element_type=F32)
        out_ref[rs, :] = (ga_ref[rs, :].astype(F32) * ya + gb_ref[rs, :].astype(F32) * yb).astype(BF16)


def _merge(hf, hb, z, gm, gs, sgu_w, sgu_bt, pa, pb):
    r, d = hf.shape
    tm = 512
    row = lambda k: (lambda i: (i, k))
    const = lambda shape: pl.BlockSpec(shape, lambda i: (0,) * len(shape), pipeline_mode=pl.Buffered(1))
    return pl.pallas_call(
        _merge_kernel,
        grid=(r // tm,),
        in_specs=[pl.BlockSpec((tm, d), row(0)), pl.BlockSpec((tm, d), row(0)),
                  pl.BlockSpec((tm, d), row(2)), pl.BlockSpec((tm, d), row(3)), pl.BlockSpec((tm, d), row(4)),
                  pl.BlockSpec((tm, d), row(5)), pl.BlockSpec((tm, d), row(6)),
                  const((1, d)), const((1, d)), const(sgu_w.shape), const(sgu_bt.shape),
                  const((d, d)), const((d, d))],
        out_specs=pl.BlockSpec((tm, d), row(0)),
        out_shape=jax.ShapeDtypeStruct((r, d), BF16),
        compiler_params=_cparams(("arbitrary",)),
        name="merge",
    )(hf, hb, z, z, z, z, z, gm, gs, sgu_w, sgu_bt, pa, pb)


def _outproj_kernel(m_ref, w_ref, x_ref, gate_ref, shift_ref, scale_ref, g_ref, rw_ref, rb_ref,
                    x1_ref, h2_ref, lg_ref):
    r1, r2, _ = _split3(rw_ref[...])
    ne = lg_ref.shape[0]
    for grp in range(x_ref.shape[0] // MERGE_ROWS):
        r0 = grp * MERGE_ROWS
        rs = slice(r0, r0 + MERGE_ROWS)
        out = jnp.dot(m_ref[rs, :], w_ref[...], preferred_element_type=F32)
        x1 = x_ref[rs, :] + gate_ref[...] * out
        x1_ref[rs, :] = x1
        ms = jnp.mean(x1 * x1, axis=-1, keepdims=True)
        h2 = (x1 * lax.rsqrt(ms + EPS) * g_ref[...]) * (1.0 + scale_ref[...]) + shift_ref[...]
        hi, lo, _ = _split3(h2)
        lg = (jnp.dot(hi, r1, preferred_element_type=F32) + jnp.dot(lo, r1, preferred_element_type=F32)
              + jnp.dot(hi, r2, preferred_element_type=F32)) + rb_ref[...]
        lg_ref[:, rs] = lg.T[:ne, :]
        _store_token_major(h2_ref, r0, h2)


def _outproj(merged, w_out, x2d, mod3, mod_row_of_tile, norm_g, rw, rb, tm):
    r, d = x2d.shape
    ne = rw.shape[1]
    rw_t = jnp.pad(rw, ((0, 0), (0, LANES - ne)))
    rb_col = jnp.pad(rb.reshape(1, ne), ((0, 0), (0, LANES - ne)))
    modspec = lambda k: pl.BlockSpec((None, 1, d), lambda i: (mod_row_of_tile(i), 0, k))
    return pl.pallas_call(
        _outproj_kernel,
        grid=(r // tm,),
        in_specs=[pl.BlockSpec((tm, d), lambda i: (i, 0)),
                  pl.BlockSpec((d, d), lambda i: (0, 0), pipeline_mode=pl.Buffered(1)),
                  pl.BlockSpec((tm, d), lambda i: (i, 0)),
                  modspec(2), modspec(3), modspec(4),
                  pl.BlockSpec((1, d), lambda i: (0, 0)),
                  pl.BlockSpec((d, LANES), lambda i: (0, 0)),
                  pl.BlockSpec((1, LANES), lambda i: (0, 0))],
        out_specs=[pl.BlockSpec((tm, d), lambda i: (i, 0)),
                   pl.BlockSpec((tm * TOKEN_ROWS, LANES), lambda i: (i, 0)),
                   pl.BlockSpec((ne, tm), lambda i: (0, i))],
        out_shape=[jax.ShapeDtypeStruct((r, d), F32),
                   jax.ShapeDtypeStruct((r * TOKEN_ROWS, LANES), F32),
                   jax.ShapeDtypeStruct((ne, r), F32)],
        compiler_params=_cparams(("arbitrary",)),
        name="outproj",
    )(merged, w_out, x2d, mod3, mod3, mod3, norm_g, rw_t, rb_col)


def _route_kernel(lg_ref, eid_ref, gate_ref, rank_ref, cnt_ref, carry_scr):
    i = pl.program_id(0)

    @pl.when(i == 0)
    def _():
        carry_scr[...] = jnp.zeros_like(carry_scr)

    l = lg_ref[...]
    ne, tm = l.shape
    e_iota = lax.broadcasted_iota(I32, (ne, tm), 0)
    vals, onehots = [], []
    for k in range(TOP_K):
        mx = jnp.max(l, axis=0, keepdims=True)
        idx = jnp.min(jnp.where(l == mx, e_iota, ne), axis=0, keepdims=True)
        oh = e_iota == idx
        l = jnp.where(oh, -jnp.inf, l)
        vals.append(mx)
        onehots.append(oh)
        eid_ref[k:k + 1, :] = idx
    ex = [jnp.exp(vk - vals[0]) for vk in vals]
    tot = ex[0] + ex[1] + ex[2] + ex[3]
    for k in range(TOP_K):
        gate_ref[k:k + 1, :] = ex[k] / tot

    oh_all = jnp.zeros((ne, tm), F32)
    for oh in onehots:
        oh_all = oh_all + jnp.where(oh, 1.0, 0.0)
    oh_all = oh_all.astype(BF16)
    r_i = lax.broadcasted_iota(I32, (LANES, LANES), 0)
    c_i = lax.broadcasted_iota(I32, (LANES, LANES), 1)
    tri_excl = jnp.where(r_i < c_i, 1.0, 0.0).astype(BF16)
    ones_m = jnp.ones((LANES, LANES), BF16)
    carry = carry_scr[...]
    for blk in range(tm // LANES):
        sl = slice(blk * LANES, (blk + 1) * LANES)
        ohb = oh_all[:, sl]
        cum = jnp.dot(ohb, tri_excl, preferred_element_type=F32) + carry
        for k in range(TOP_K):
            rk = jnp.sum(jnp.where(onehots[k][:, sl], cum, 0.0), axis=0, keepdims=True)
            rank_ref[k:k + 1, sl] = rk.astype(I32)
        carry = carry + jnp.dot(ohb, ones_m, preferred_element_type=F32)
    carry_scr[...] = carry
    cnt_ref[...] = carry


def _route(logits_t):
    ne, r = logits_t.shape
    tm = 1024
    return pl.pallas_call(
        _route_kernel,
        grid=(r // tm,),
        in_specs=[pl.BlockSpec((ne, tm), lambda i: (0, i))],
        out_specs=[pl.BlockSpec((TOP_K, tm), lambda i: (0, i)),
                   pl.BlockSpec((TOP_K, tm), lambda i: (0, i)),
                   pl.BlockSpec((TOP_K, tm), lambda i: (0, i)),
                   pl.BlockSpec((ne, LANES), lambda i: (0, 0))],
        out_shape=[jax.ShapeDtypeStruct((TOP_K, r), I32),
                   jax.ShapeDtypeStruct((TOP_K, r), F32),
                   jax.ShapeDtypeStruct((TOP_K, r), I32),
                   jax.ShapeDtypeStruct((ne, LANES), F32)],
        scratch_shapes=[pltpu.VMEM((ne, LANES), F32)],
        compiler_params=_cparams(("arbitrary",)),
        name="route",
    )(logits_t)


def _slot_kernel(eid_ref, rank_ref, cnt_ref, slot_ref):
    ne = cnt_ref.shape[0]
    nblk = jnp.floor((cnt_ref[...] + (MOE_TILE - 1.0)) * (1.0 / MOE_TILE))
    r_i = lax.broadcasted_iota(I32, (ne, ne), 0)
    c_i = lax.broadcasted_iota(I32, (ne, ne), 1)
    tri = jnp.where(c_i < r_i, 1.0, 0.0).astype(BF16)
    start = jnp.dot(tri, nblk.astype(BF16), preferred_element_type=F32) * MOE_TILE
    eid = eid_ref[...]
    acc = rank_ref[...]
    for e in range(ne):
        acc = acc + jnp.where(eid == e, start[e:e + 1, 0:1].astype(I32), 0)
    slot_ref[...] = acc


def _slots(eid, rank, cnt):
    return pl.pallas_call(
        _slot_kernel,
        out_shape=jax.ShapeDtypeStruct(eid.shape, I32),
        compiler_params=pltpu.CompilerParams(vmem_limit_bytes=V7X_VMEM_LIMIT_BYTES),
        name="slots",
    )(eid, rank, cnt)


def _dispatch_copy(slot_ref, h_ref, xs_ref, sem, k, t, base, n_tok):
    d0 = pl.multiple_of(slot_ref[k * n_tok + base + t] * TOKEN_ROWS, TOKEN_ROWS)
    s0 = pl.multiple_of(t * TOKEN_ROWS, TOKEN_ROWS)
    return pltpu.make_async_copy(h_ref.at[pl.ds(s0, TOKEN_ROWS), :], xs_ref.at[pl.ds(d0, TOKEN_ROWS), :], sem)


def _zero_copy(zbuf, xs_ref, zsem, row, n_rows):
    d0 = pl.multiple_of(row * TOKEN_ROWS, TOKEN_ROWS)
    return pltpu.make_async_copy(zbuf.at[pl.ds(0, n_rows * TOKEN_ROWS), :],
                                 xs_ref.at[pl.ds(d0, n_rows * TOKEN_ROWS), :], zsem)


def _zero_fill(start_ref, cnt_ref, zbuf, xs_ref, zsem, p_rows, begin):
    ne = start_ref.shape[0]

    def op(cp):
        cp.start() if begin else cp.wait()

    def per_expert(e, used):
        n = cnt_ref[e]
        s0 = start_ref[e]
        c_step = (n + MOE_STEP - 1) // MOE_STEP * MOE_STEP
        c_tile = (n + MOE_TILE - 1) // MOE_TILE * MOE_TILE

        def one_row(r, c):
            op(_zero_copy(zbuf, xs_ref, zsem, s0 + r, 1))
            return c

        def one_unit(u, c):
            op(_zero_copy(zbuf, xs_ref, zsem, s0 + u * MOE_STEP, MOE_STEP))
            return c

        lax.fori_loop(n, c_step, one_row, 0)
        lax.fori_loop(c_step // MOE_STEP, c_tile // MOE_STEP, one_unit, 0)
        return s0 + c_tile

    used = lax.fori_loop(0, ne, per_expert, 0)

    def trailing(u, c):
        op(_zero_copy(zbuf, xs_ref, zsem, u * MOE_STEP, MOE_STEP))
        return c

    lax.fori_loop(used // MOE_STEP, p_rows // MOE_STEP, trailing, 0)


def _dispatch_kernel(slot_ref, start_ref, cnt_ref, h_ref, xs_ref, zbuf, sem, zsem, *, tm, n_tok, p_rows):
    i = pl.program_id(0)
    base = i * tm

    @pl.when(i == 0)
    def _():
        zbuf[...] = jnp.zeros_like(zbuf)
        _zero_fill(start_ref, cnt_ref, zbuf, xs_ref, zsem, p_rows, True)

    def start(t, carry):
        for k in range(TOP_K):
            _dispatch_copy(slot_ref, h_ref, xs_ref, sem, k, t, base, n_tok).start()
        return carry

    def wait(t, carry):
        for k in range(TOP_K):
            _dispatch_copy(slot_ref, h_ref, xs_ref, sem, k, t, base, n_tok).wait()
        return carry

    lax.fori_loop(0, tm, start, 0, unroll=4)
    lax.fori_loop(0, tm, wait, 0, unroll=4)

    @pl.when(i == 0)
    def _():
        _zero_fill(start_ref, cnt_ref, zbuf, xs_ref, zsem, p_rows, False)


def _dispatch(h2p, slot, start_pad, counts, p_rows):
    n_tok = h2p.shape[0] // TOKEN_ROWS
    tm = 512
    grid_spec = pltpu.PrefetchScalarGridSpec(
        num_scalar_prefetch=3,
        grid=(n_tok // tm,),
        in_specs=[pl.BlockSpec((tm * TOKEN_ROWS, LANES), lambda i, *_: (i, 0))],
        out_specs=pl.BlockSpec(memory_space=pl.ANY),
        scratch_shapes=[pltpu.VMEM((MOE_STEP * TOKEN_ROWS, LANES), h2p.dtype),
                        pltpu.SemaphoreType.DMA(()), pltpu.SemaphoreType.DMA(())],
    )
    return pl.pallas_call(
        functools.partial(_dispatch_kernel, tm=tm, n_tok=n_tok, p_rows=p_rows),
        grid_spec=grid_spec,
        out_shape=jax.ShapeDtypeStruct((p_rows * TOKEN_ROWS, LANES), h2p.dtype),
        compiler_params=_cparams(("arbitrary",)),
        name="dispatch",
    )(slot, start_pad, counts, h2p)


def _weight_copies(w_hbm, wbuf, sems, e, col_starts, slot):
    tn = wbuf.shape[-1]
    aligned = lambda c: c if isinstance(c, int) else pl.multiple_of(c, LANES)
    return [pltpu.make_async_copy(w_hbm.at[e, :, pl.ds(aligned(c), tn)], wbuf.at[slot, i], sems.at[slot])
            for i, c in enumerate(col_starts)]


def _advance_weight_pipeline(w_hbm, wbuf, sems, sched, j, nj, it, col_starts_of):
    e_ref, run_ref, last_ref, nxt_ref, nruns_ref = sched
    slot = (j * nruns_ref[0] + run_ref[it]) % 2

    @pl.when((j == 0) & (it == 0))
    def _():
        for cp in _weight_copies(w_hbm, wbuf, sems, e_ref[it], col_starts_of(j), slot):
            cp.start()

    for cp in _weight_copies(w_hbm, wbuf, sems, e_ref[it], col_starts_of(j), slot):
        cp.wait()
    is_last = last_ref[it] == 1

    @pl.when(jnp.logical_not(is_last))
    def _():
        for cp in _weight_copies(w_hbm, wbuf, sems, nxt_ref[it], col_starts_of(j), 1 - slot):
            cp.start()

    @pl.when(is_last & (j + 1 < nj))
    def _():
        for cp in _weight_copies(w_hbm, wbuf, sems, nxt_ref[it], col_starts_of(j + 1), 1 - slot):
            cp.start()

    return slot


def _moe_up_kernel(blk_ref, oblk_ref, e_ref, nv_ref, new_ref, run_ref, last_ref, nxt_ref, nruns_ref,
                   xs_ref, w_hbm, bg_ref, bl_ref, act_ref, wbuf, sems, wg_bf, wl_bf):
    j = pl.program_id(0)
    nj = pl.num_programs(0)
    it = pl.program_id(1)
    tn = wg_bf.shape[1]

    is_new = new_ref[it] == 1

    @pl.when(is_new)
    def _():
        _advance_weight_pipeline(w_hbm, wbuf, sems, (e_ref, run_ref, last_ref, nxt_ref, nruns_ref), j, nj, it,
                                 lambda jj: (jj * tn, (nj + jj) * tn))

    slot = (j * nruns_ref[0] + run_ref[it]) % 2

    def cast_weights():
        wg_bf[...] = wbuf[slot, 0].astype(BF16)
        wl_bf[...] = wbuf[slot, 1].astype(BF16)

    def rows_chunk(r0, n):
        x = _load_token_major(xs_ref, r0, n).astype(BF16)
        hg = jnp.dot(x, wg_bf[...], preferred_element_type=F32) + bg_ref[...]
        hl = jnp.dot(x, wl_bf[...], preferred_element_type=F32) + bl_ref[...]
        glu = jnp.minimum(hg, SWIGLU_LIMIT)
        lin = jnp.clip(hl, -SWIGLU_LIMIT, SWIGLU_LIMIT)
        act_ref[r0:r0 + n, :] = ((0.5 * glu) * (1.0 + jnp.tanh((0.5 * SWIGLU_ALPHA) * glu)) * (lin + 1.0)).astype(BF16)

    def rows_zero(r0, n):
        act_ref[r0:r0 + n, :] = jnp.zeros((n, act_ref.shape[1]), BF16)

    _moe_row_variants(nv_ref[it], is_new, cast_weights, rows_chunk, rows_zero)


def _moe_row_variants(nv, is_new, cast_weights, rows_chunk, rows_zero):
    for fresh in (False, True):
        for units in range(MOE_TILE // MOE_STEP + 1):
            rows = units * MOE_STEP
            in_range = ((nv > rows - MOE_STEP) & (nv <= rows)) if units else (nv <= 0)

            @pl.when(in_range & (is_new if fresh else jnp.logical_not(is_new)))
            def _():
                if fresh:
                    cast_weights()
                r0 = 0
                while r0 < rows:
                    n = min(MOE_HALF, rows - r0)
                    rows_chunk(r0, n)
                    r0 += n
                if rows < MOE_TILE:
                    rows_zero(rows, MOE_TILE - rows)


def _moe_down_kernel(blk_ref, oblk_ref, e_ref, nv_ref, new_ref, run_ref, last_ref, nxt_ref, nruns_ref,
                     a_ref, w_hbm, b_ref, y_ref, wbuf, sems, w_bf):
    it = pl.program_id(0)

    is_new = new_ref[it] == 1

    @pl.when(is_new)
    def _():
        _advance_weight_pipeline(w_hbm, wbuf, sems, (e_ref, run_ref, last_ref, nxt_ref, nruns_ref), 0, 1, it,
                                 lambda jj: (0,))

    slot = run_ref[it] % 2

    def cast_weights():
        w_bf[...] = wbuf[slot, 0].astype(BF16)

    def rows_chunk(r0, n):
        y = jnp.dot(a_ref[r0:r0 + n, :], w_bf[...], preferred_element_type=F32) + b_ref[...]
        _store_token_major(y_ref, r0, y)

    def rows_zero(r0, n):
        y_ref[r0 * TOKEN_ROWS:(r0 + n) * TOKEN_ROWS, :] = jnp.zeros((n * TOKEN_ROWS, LANES), F32)

    _moe_row_variants(nv_ref[it], is_new, cast_weights, rows_chunk, rows_zero)


def _moe_up(items, xs, w1, b1):
    n_items = items[0].shape[0]
    ne, d, two_f = w1.shape
    f = two_f // 2
    nj = f // MOE_TN
    p = xs.shape[0] // TOKEN_ROWS
    b1r = b1.reshape(ne, 1, two_f)
    grid_spec = pltpu.PrefetchScalarGridSpec(
        num_scalar_prefetch=len(items),
        grid=(nj, n_items),
        in_specs=[pl.BlockSpec((MOE_TILE * TOKEN_ROWS, LANES), lambda j, it, blk, *_: (blk[it], 0)),
                  pl.BlockSpec(memory_space=pl.ANY),
                  pl.BlockSpec((None, 1, MOE_TN), lambda j, it, blk, oblk, e, *_: (e[it], 0, j)),
                  pl.BlockSpec((None, 1, MOE_TN), lambda j, it, blk, oblk, e, *_: (e[it], 0, nj + j))],
        out_specs=pl.BlockSpec((MOE_TILE, MOE_TN), lambda j, it, blk, oblk, *_: (oblk[it], j)),
        scratch_shapes=[pltpu.VMEM((2, 2, d, MOE_TN), F32), pltpu.SemaphoreType.DMA((2,)),
                        pltpu.VMEM((d, MOE_TN), BF16), pltpu.VMEM((d, MOE_TN), BF16)],
    )
    return pl.pallas_call(
        _moe_up_kernel,
        grid_spec=grid_spec,
        out_shape=jax.ShapeDtypeStruct((p, f), BF16),
        compiler_params=_cparams(("arbitrary", "arbitrary"), V7X_MOE_VMEM_LIMIT_BYTES),
        name="moe_up",
    )(*items, xs, w1, b1r, b1r)


def _moe_down(items, act, w2, b2):
    n_items = items[0].shape[0]
    ne, f, d = w2.shape
    p = act.shape[0]
    b2r = b2.reshape(ne, 1, d)
    grid_spec = pltpu.PrefetchScalarGridSpec(
        num_scalar_prefetch=len(items),
        grid=(n_items,),
        in_specs=[pl.BlockSpec((MOE_TILE, f), lambda it, blk, *_: (blk[it], 0)),
                  pl.BlockSpec(memory_space=pl.ANY),
                  pl.BlockSpec((None, 1, d), lambda it, blk, oblk, e, *_: (e[it], 0, 0))],
        out_specs=pl.BlockSpec((MOE_TILE * TOKEN_ROWS, LANES), lambda it, blk, oblk, *_: (oblk[it], 0)),
        scratch_shapes=[pltpu.VMEM((2, 1, f, d), F32), pltpu.SemaphoreType.DMA((2,)), pltpu.VMEM((f, d), BF16)],
    )
    return pl.pallas_call(
        _moe_down_kernel,
        grid_spec=grid_spec,
        out_shape=jax.ShapeDtypeStruct((p * TOKEN_ROWS, LANES), F32),
        compiler_params=_cparams(("arbitrary",), V7X_MOE_VMEM_LIMIT_BYTES),
        name="moe_down",
    )(*items, act, w2, b2r)


FINAL_ROWS = 64


def _combine_copy(slot_ref, y_ref, buf, sems, k, t, tile, par, tm, n_tok):
    s0 = pl.multiple_of(slot_ref[k * n_tok + tile * tm + t] * TOKEN_ROWS, TOKEN_ROWS)
    d0 = pl.multiple_of(t * TOKEN_ROWS, TOKEN_ROWS)
    return pltpu.make_async_copy(y_ref.at[pl.ds(s0, TOKEN_ROWS), :], buf.at[par, k, pl.ds(d0, TOKEN_ROWS), :],
                                 sems.at[par])


def _final_kernel(slot_ref, y_ref, gates_ref, x1_ref, gate_ref, g_ref, o_ref, buf, sems, *, n_tok):
    tm, d = x1_ref.shape
    i = pl.program_id(0)
    par = i % 2

    def gather(tile, par_, start):
        def body(t, carry):
            for k in range(TOP_K):
                cp = _combine_copy(slot_ref, y_ref, buf, sems, k, t, tile, par_, tm, n_tok)
                cp.start() if start else cp.wait()
            return carry
        lax.fori_loop(0, tm, body, 0, unroll=4)

    @pl.when(i == 0)
    def _():
        gather(0, 0, True)

    @pl.when(i + 1 < pl.num_programs(0))
    def _():
        gather(i + 1, 1 - par, True)

    gather(i, par, False)

    for grp in range(tm // FINAL_ROWS):
        r0 = grp * FINAL_ROWS
        rs = slice(r0, r0 + FINAL_ROWS)
        gk = [jnp.broadcast_to(gates_ref[rs, k:k + 1], (FINAL_ROWS, LANES)) for k in range(TOP_K)]
        sq = jnp.zeros((FINAL_ROWS, LANES), F32)
        for s in range(TOKEN_ROWS):
            cs = slice(s * LANES, (s + 1) * LANES)
            rows = pl.ds(r0 * TOKEN_ROWS + s, FINAL_ROWS, stride=TOKEN_ROWS)
            acc = gk[0] * buf[par, 0, rows, :]
            for k in range(1, TOP_K):
                acc = acc + gk[k] * buf[par, k, rows, :]
            xf = x1_ref[rs, cs] + gate_ref[:, cs] * acc
            o_ref[rs, cs] = xf
            sq = sq + xf * xf
        scale = lax.rsqrt(jnp.sum(sq, axis=-1, keepdims=True) * (1.0 / d) + EPS)
        for s in range(TOKEN_ROWS):
            cs = slice(s * LANES, (s + 1) * LANES)
            o_ref[rs, cs] = o_ref[rs, cs] * scale * g_ref[:, cs]


def _final(slot, y, gates_tk, x1, mod3, mod_row_of_tile, final_g, tm):
    r, d = x1.shape
    grid_spec = pltpu.PrefetchScalarGridSpec(
        num_scalar_prefetch=1,
        grid=(r // tm,),
        in_specs=[pl.BlockSpec(memory_space=pl.ANY),
                  pl.BlockSpec((tm, TOP_K), lambda i, slot: (i, 0)),
                  pl.BlockSpec((tm, d), lambda i, slot: (i, 0)),
                  pl.BlockSpec((None, 1, d), lambda i, slot: (mod_row_of_tile(i), 0, 5)),
                  pl.BlockSpec((1, d), lambda i, slot: (0, 0))],
        out_specs=pl.BlockSpec((tm, d), lambda i, slot: (i, 0)),
        scratch_shapes=[pltpu.VMEM((2, TOP_K, tm * TOKEN_ROWS, LANES), F32), pltpu.SemaphoreType.DMA((2,))],
    )
    return pl.pallas_call(
        functools.partial(_final_kernel, n_tok=r),
        grid_spec=grid_spec,
        out_shape=jax.ShapeDtypeStruct((r, d), F32),
        compiler_params=_cparams(("arbitrary",)),
        name="final",
    )(slot, y, gates_tk, x1, mod3, final_g)


def _moe_schedule(counts, n_items):
    ne = counts.shape[0]
    padded = (counts + MOE_TILE - 1) // MOE_TILE * MOE_TILE
    pad_end = jnp.cumsum(padded)
    start_pad = pad_end - padded
    n_real = pad_end[-1] // MOE_TILE
    b = jnp.arange(n_items, dtype=I32)
    blk = jnp.minimum(b, jnp.maximum(n_real - 1, 0))
    blk_start = blk * MOE_TILE
    e_of = jnp.minimum(jnp.sum((pad_end[None, :] <= blk_start[:, None]).astype(I32), axis=1), ne - 1)
    nv = jnp.clip(counts[e_of] - (blk_start - start_pad[e_of]), 0, MOE_TILE)
    nv = jnp.where(b < n_real, nv, 0)
    e_of = e_of.astype(I32)
    prev_e = jnp.concatenate([jnp.full((1,), -1, I32), e_of[:-1]])
    new = ((e_of != prev_e) | (b == 0)).astype(I32)
    run = jnp.cumsum(new) - 1
    n_runs = run[-1] + 1
    last = (run == n_runs - 1).astype(I32)
    first_pos = jnp.where(new == 1, b, n_items)
    nxt_pos = jnp.concatenate([lax.cummin(first_pos, reverse=True)[1:], jnp.full((1,), n_items, I32)])
    nxt = jnp.where(nxt_pos < n_items, e_of[jnp.minimum(nxt_pos, n_items - 1)], e_of[0])
    return start_pad.astype(I32), (blk.astype(I32), b, e_of, nv.astype(I32), new, run.astype(I32), last,
                                   nxt.astype(I32), n_runs.reshape(1).astype(I32))


def kernel(x, c, ctx, c_ctx, mod_w, mod_b, norm1_g, norm2_g, w_in, gate_b, conv_w, conv_b, mlstm_norm_g,
           sgu_norm_g, sgu_w, sgu_b, proj_a, proj_b, w_out, router_w, router_b, exp_w1, exp_b1, exp_w2,
           exp_b2, final_g):
    bsz, t, d = x.shape
    tc = ctx.shape[1]
    depth = mod_w.shape[0]
    assert depth == 1, "single-layer block"
    assert d == TOKEN_ROWS * LANES, "token-major rows hold TOKEN_ROWS*128 features"
    l = 0
    r = bsz * t
    qkw = 2 * M_HEADS * QK_DIM
    vw = M_HEADS * V_DIM
    off_gates = qkw + vw
    n_gates = 4 * M_HEADS
    off_o = off_gates + n_gates

    cc = jnp.zeros((8, d), F32).at[:bsz].set(c).at[bsz].set(c_ctx)
    mod3 = _mod(cc, mod_w[l], mod_b[l]).reshape(8, 1, N_MOD * d)

    in_cols = w_in.shape[2]
    assert off_gates % INPROJ_TN == 0 and (in_cols - off_o) % INPROJ_TN == 0
    n_aligned = off_gates // INPROJ_TN
    n_tiles = n_aligned + (in_cols - off_o) // INPROJ_TN
    w_in_t = jnp.swapaxes(w_in, 1, 2)
    w_gates = jnp.pad(w_in_t[l, off_gates:off_o, :].T, ((0, 0), (0, LANES - n_gates)))
    gb = jnp.pad(gate_b[l].reshape(1, n_gates), ((0, 0), (0, LANES - n_gates)))
    g1 = norm1_g[l].reshape(1, d)

    tm_in = 512
    h_lat, g_lat = _norm(x.reshape(r, d), mod3, lambda i: i // (t // tm_in), g1, w_gates, gb, tm_in)
    h_ctx, g_ctx = _norm(ctx.reshape(bsz * tc, d), mod3, lambda i: bsz, g1, w_gates, gb, tc)
    z_lat = _inproj(h_lat, w_in_t, l, n_aligned, n_gates, n_tiles, 2048)
    z_ctx = _inproj(h_ctx, w_in_t, l, n_aligned, n_gates, n_aligned, bsz * tc)

    z_lat3 = z_lat.reshape(bsz, t, z_lat.shape[1])
    z_ctx3 = z_ctx.reshape(bsz, tc, z_ctx.shape[1])
    qk_lat = _conv(z_lat3, conv_w[l], conv_b[l])
    qk_ctx = _conv(z_ctx3, conv_w[l], conv_b[l])

    hf, hb = _mlstm(qk_ctx, qk_lat, z_ctx3, z_lat3, g_ctx.reshape(bsz, tc, LANES), g_lat.reshape(bsz, t, LANES))

    merged = _merge(hf.reshape(r, vw), hb.reshape(r, vw), z_lat,
                    mlstm_norm_g[l].reshape(1, vw), sgu_norm_g[l].reshape(1, -1),
                    sgu_w[l].astype(BF16), sgu_b[l].T, proj_a[l].astype(BF16), proj_b[l].astype(BF16))

    tm_out = 512
    x1, h2p, logits_t = _outproj(merged, w_out[l].astype(BF16), x.reshape(r, d), mod3,
                                 lambda i: i // (t // tm_out), norm2_g[l].reshape(1, d),
                                 router_w[l], router_b[l], tm_out)
    tm_fin = 256

    eid, gates, rank, cnt = _route(logits_t)
    counts = cnt[:, 0].astype(I32)
    n_items = (r * TOP_K) // MOE_TILE + N_EXPERTS
    start_pad, items = _moe_schedule(counts, n_items)
    slot = _slots(eid, rank, cnt).reshape(-1)
    xs = _dispatch(h2p, slot, start_pad, counts, n_items * MOE_TILE)
    act = _moe_up(items, xs, exp_w1[l], exp_b1[l])
    y = _moe_down(items, act, exp_w2[l], exp_b2[l])
    out = _final(slot, y, gates.T, x1, mod3, lambda i: i // (t // tm_fin), final_g.reshape(1, d), tm_fin)
    return out.reshape(bsz, t, d)
```

```python
import functools

import jax
import jax.numpy as jnp
from jax import lax
from jax.experimental import pallas as pl
from jax.experimental.pallas import tpu as pltpu

F32 = jnp.float32
BF16 = jnp.bfloat16
I32 = jnp.int32

EPS = 1e-6
M_HEADS = 8
QK_DIM = 128
V_DIM = 256
CONV_W = 5
CHUNK = 128
SGU_GROUPS = 8
N_EXPERTS = 32
TOP_K = 4
SWIGLU_ALPHA = 1.702
SWIGLU_LIMIT = 7.0
N_MOD = 6

V7X_VMEM_LIMIT_BYTES = 56 * 1024 * 1024
LANES = 128
MOE_TILE = 512
MOE_UP_CHUNK = 256
MOE_DOWN_CHUNK = 256
MOE_STEP = 128
MOE_TN = 1024


def _cparams(sem):
    return pltpu.CompilerParams(dimension_semantics=sem, vmem_limit_bytes=V7X_VMEM_LIMIT_BYTES)


def _sigmoid(x):
    return 1.0 / (1.0 + jnp.exp(-x))


def _split3(a):
    a1 = a.astype(BF16)
    r1 = a - a1.astype(F32)
    a2 = r1.astype(BF16)
    a3 = (r1 - a2.astype(F32)).astype(BF16)
    return a1, a2, a3


TOKEN_ROWS = 16


def _store_token_major(ref, row0, vals):
    n = vals.shape[0]
    for s in range(TOKEN_ROWS):
        ref[pl.ds(row0 * TOKEN_ROWS + s, n, stride=TOKEN_ROWS), :] = vals[:, s * LANES:(s + 1) * LANES]


def _load_token_major(ref, row0, n, lead=()):
    cols = [ref[(*lead, pl.ds(row0 * TOKEN_ROWS + s, n, stride=TOKEN_ROWS), slice(None))] for s in range(TOKEN_ROWS)]
    return jnp.concatenate(cols, axis=1)


def _mod_kernel(s_ref, w_ref, b_ref, o_ref):
    s = s_ref[...]
    s = s * _sigmoid(s)
    o_ref[...] = jnp.dot(s.astype(BF16), w_ref[...].astype(BF16), preferred_element_type=F32) + b_ref[...]


def _mod(cc, mod_w, mod_b):
    d, n = mod_w.shape
    tn = 1024
    return pl.pallas_call(
        _mod_kernel,
        grid=(n // tn,),
        in_specs=[pl.BlockSpec((8, d), lambda j: (0, 0)),
                  pl.BlockSpec((d, tn), lambda j: (0, j)),
                  pl.BlockSpec((1, tn), lambda j: (0, j))],
        out_specs=pl.BlockSpec((8, tn), lambda j: (0, j)),
        out_shape=jax.ShapeDtypeStruct((8, n), F32),
        compiler_params=_cparams(("arbitrary",)),
        name="mod",
    )(cc, mod_w, mod_b.reshape(1, n))


def _norm_kernel(x_ref, shift_ref, scale_ref, g_ref, wg_ref, gb_ref, h_ref, gates_ref):
    x = x_ref[...]
    ms = jnp.mean(x * x, axis=-1, keepdims=True)
    h = (x * lax.rsqrt(ms + EPS) * g_ref[...]) * (1.0 + scale_ref[...]) + shift_ref[...]
    h1, h2, _ = _split3(h)
    h_ref[...] = h1
    w1, w2, _ = _split3(wg_ref[...])
    gates_ref[...] = (jnp.dot(h1, w1, preferred_element_type=F32)
                      + jnp.dot(h1, w2, preferred_element_type=F32)
                      + jnp.dot(h2, w1, preferred_element_type=F32)) + gb_ref[...]


def _norm(x2d, mod3, mod_row_of_tile, norm_g, w_gates, gate_b, tm):
    r, d = x2d.shape
    return pl.pallas_call(
        _norm_kernel,
        grid=(r // tm,),
        in_specs=[pl.BlockSpec((tm, d), lambda i: (i, 0)),
                  pl.BlockSpec((None, 1, d), lambda i: (mod_row_of_tile(i), 0, 0)),
                  pl.BlockSpec((None, 1, d), lambda i: (mod_row_of_tile(i), 0, 1)),
                  pl.BlockSpec((1, d), lambda i: (0, 0)),
                  pl.BlockSpec((d, LANES), lambda i: (0, 0)),
                  pl.BlockSpec((1, LANES), lambda i: (0, 0))],
        out_specs=[pl.BlockSpec((tm, d), lambda i: (i, 0)),
                   pl.BlockSpec((tm, LANES), lambda i: (i, 0))],
        out_shape=[jax.ShapeDtypeStruct((r, d), BF16),
                   jax.ShapeDtypeStruct((r, LANES), F32)],
        compiler_params=_cparams(("arbitrary",)),
        name="norm",
    )(x2d, mod3, mod3, norm_g, w_gates, gate_b)


INPROJ_TN = 1024
INPROJ_ROWS = 256


def _inproj_kernel(h_ref, wt_ref, z_ref, w_bf, *, tiles_per_seg):
    j = pl.program_id(0)
    i = pl.program_id(1)

    @pl.when(i == 0)
    def _():
        w_bf[...] = wt_ref[...].T.astype(BF16)

    seg = j // tiles_per_seg
    is_raw = seg < 2
    is_gelu = (seg == 3) | (seg == 4)
    c0 = 0.7978845608028654
    p0 = jnp.where(is_gelu, c0, 0.5).astype(F32)
    p1 = jnp.where(is_gelu, c0 * 0.044715, 0.0).astype(F32)
    q0 = jnp.where(is_gelu, 0.0, 1.0).astype(F32)
    q1 = jnp.where(is_gelu, 1.0, 0.0).astype(F32)
    w = w_bf[...]
    for c in range(h_ref.shape[0] // INPROJ_ROWS):
        rs = slice(c * INPROJ_ROWS, (c + 1) * INPROJ_ROWS)
        z = jnp.dot(h_ref[rs, :], w, preferred_element_type=F32)
        act = (0.5 * (q0 + q1 * z)) * (1.0 + jnp.tanh(z * (p0 + p1 * (z * z))))
        z_ref[rs, :] = jnp.where(is_raw, z, act).astype(BF16)


def _inproj(h, w_in_t, layer, n_aligned, shift, n_tiles, tm):
    r, d = h.shape
    tn = INPROJ_TN
    kern = functools.partial(_inproj_kernel, tiles_per_seg=d // tn)
    assert shift % 8 == 0
    row0 = lambda j: pl.multiple_of(j * tn + jnp.where(j >= n_aligned, shift, 0), 8)
    return pl.pallas_call(
        kern,
        grid=(n_tiles, r // tm),
        in_specs=[pl.BlockSpec((tm, d), lambda j, i: (i, 0)),
                  pl.BlockSpec((None, pl.Element(tn), pl.Element(d)), lambda j, i: (layer, row0(j), 0))],
        out_specs=pl.BlockSpec((tm, tn), lambda j, i: (i, j)),
        out_shape=jax.ShapeDtypeStruct((r, n_tiles * tn), BF16),
        scratch_shapes=[pltpu.VMEM((d, tn), BF16)],
        compiler_params=_cparams(("arbitrary", "arbitrary")),
        name="inproj",
    )(h, w_in_t)


def _conv_kernel(x_ref, w_ref, b_ref, o_ref, *, q_tiles):
    c = pl.program_id(1)
    x = x_ref[...].astype(F32)
    t = x.shape[0]
    rows = lax.broadcasted_iota(I32, x.shape, 0)
    w = w_ref[...]
    acc = x * w[CONV_W // 2:CONV_W // 2 + 1, :] + b_ref[...]
    for dlt in range(-(CONV_W // 2), CONV_W // 2 + 1):
        if dlt == 0:
            continue
        xs = pltpu.roll(x, shift=(-dlt) % t, axis=0)
        valid = (rows + dlt >= 0) & (rows + dlt < t)
        acc = acc + jnp.where(valid, xs, 0.0) * w[dlt + CONV_W // 2:dlt + CONV_W // 2 + 1, :]
    y = (0.5 * acc) * (1.0 + jnp.tanh(0.5 * acc))
    scale = jnp.where(c < q_tiles, QK_DIM ** -0.5, 1.0).astype(F32)
    o_ref[...] = (y * scale).astype(BF16)


def _conv(z3, conv_w, conv_b):
    b, t, _ = z3.shape
    width = conv_w.shape[1]
    tc = 256
    kern = functools.partial(_conv_kernel, q_tiles=(width // 2) // tc)
    return pl.pallas_call(
        kern,
        grid=(b, width // tc),
        in_specs=[pl.BlockSpec((None, t, tc), lambda i, c: (i, 0, c)),
                  pl.BlockSpec((CONV_W, tc), lambda i, c: (0, c)),
                  pl.BlockSpec((1, tc), lambda i, c: (0, c))],
        out_specs=pl.BlockSpec((None, t, tc), lambda i, c: (i, 0, c)),
        out_shape=jax.ShapeDtypeStruct((b, t, width), BF16),
        compiler_params=_cparams(("arbitrary", "arbitrary")),
        name="conv",
    )(z3, conv_w, conv_b.reshape(1, width))


def _mlstm_kernel(qkc_f, qkl_f, vc_f, vl_f, gc_f, gl_f, qkc_b, qkl_b, vc_b, vl_b, gc_b, gl_b,
                  hf_ref, hb_ref, *state, n_ctx_chunks):
    c_scr, m_scr = state[:2 * M_HEADS], state[2 * M_HEADS:]
    s = pl.program_id(1)

    @pl.when(s == 0)
    def _():
        for ref in state:
            ref[...] = jnp.zeros_like(ref)

    is_ctx = s < n_ctx_chunks
    rows = lax.broadcasted_iota(I32, (CHUNK, CHUNK), 0)
    lanes = lax.broadcasted_iota(I32, (CHUNK, CHUNK), 1)
    ones_col = jnp.where(lanes == 0, 1.0, 0.0).astype(BF16)
    dn_nt = (((1,), (1,)), ((), ()))
    dn_tn = (((0,), (0,)), ((), ()))
    qk_w = M_HEADS * QK_DIM

    dirs = ((qkc_f, qkl_f, vc_f, vl_f, gc_f, gl_f, hf_ref), (qkc_b, qkl_b, vc_b, vl_b, gc_b, gl_b, hb_ref))
    per_dir = []
    for d, (qkc, qkl, vc, vl, gc, gl, out_ref) in enumerate(dirs):
        qk = jnp.where(is_ctx, qkc[...], qkl[...])
        v = jnp.where(is_ctx, vc[...], vl[...])
        g = jnp.where(is_ctx, gc[...], gl[...])
        mask = (lanes <= rows) if d == 0 else (lanes >= rows)
        tri = jnp.where(mask, 1.0, 0.0).astype(BF16)
        ls = jnp.minimum(g, 0.0) - jnp.log(1.0 + jnp.exp(-jnp.abs(g)))
        l1, l2, l3 = _split3(ls)
        bcol = (jnp.dot(tri, l1, preferred_element_type=F32) + jnp.dot(tri, l2, preferred_element_type=F32)
                + jnp.dot(tri, l3, preferred_element_type=F32))
        brow, grow = bcol.T, g.T
        c0 = 2 * M_HEADS * d
        ib = grow[c0:c0 + M_HEADS, :] - brow[c0 + M_HEADS:c0 + 2 * M_HEADS, :]
        lane_h = lax.broadcasted_iota(I32, ib.shape, 1)
        rmax = ib
        step = 1
        while step < CHUNK:
            if d == 0:
                prev = jnp.where(lane_h >= step, pltpu.roll(rmax, step, axis=1), -jnp.inf)
            else:
                prev = jnp.where(lane_h < CHUNK - step, pltpu.roll(rmax, CHUNK - step, axis=1), -jnp.inf)
            rmax = jnp.maximum(rmax, prev)
            step *= 2
        rmax_col = jnp.concatenate([rmax, jnp.zeros((CHUNK - M_HEADS, CHUNK), F32)], axis=0).T
        per_dir.append((d, qk, v, g, mask, bcol, ib, rmax_col, out_ref))

    units = [(pd, h) for pd in per_dir for h in range(M_HEADS)]
    stage1 = []
    for (d, qk, v, g, mask, bcol, ib, rmax_col, out_ref), h in units:
        col_i = 2 * M_HEADS * d + h
        col_f = col_i + M_HEADS
        idx = d * M_HEADS + h
        end = CHUNK - 1 if d == 0 else 0
        bc = bcol[:, col_f:col_f + 1]
        ic = g[:, col_i:col_i + 1]
        ibr = ib[h:h + 1, :]
        m = m_scr[idx][0:1, 0:1]
        inter = bc + m
        m_t = jnp.maximum(inter, bc + rmax_col[:, h:h + 1])
        w_inter = jnp.exp(inter - m_t)
        b_end = bc[end:end + 1, :]
        m_new = jnp.maximum(b_end + m, b_end + rmax_col[end:end + 1, h:h + 1])
        decay = jnp.exp(b_end + m - m_new)
        w_col = jnp.exp(b_end - bc + ic - m_new)
        q = qk[:, h * QK_DIM:(h + 1) * QK_DIM]
        k = qk[:, qk_w + h * QK_DIM:qk_w + (h + 1) * QK_DIM]
        decay_mat = jnp.exp(jnp.where(mask, (bc - m_t) + ibr, -jnp.inf))
        sc = lax.dot_general(q, k, dn_nt, preferred_element_type=F32) * decay_mat
        lhs = jnp.concatenate([(q.astype(F32) * w_inter).astype(BF16), sc.astype(BF16)], axis=1)
        kw = (k.astype(F32) * w_col).astype(BF16)
        stage1.append((lhs, kw, m_t, m_new, decay))

    stage2 = []
    for ((d, qk, v, g, mask, bcol, brow, grow, out_ref), h), (lhs, kw, m_t, m_new, decay) in zip(units, stage1):
        idx = d * M_HEADS + h
        vaug = jnp.concatenate([v[:, h * V_DIM:(h + 1) * V_DIM], ones_col], axis=1)
        caug = c_scr[idx][...]
        num = jnp.dot(lhs, jnp.concatenate([caug.astype(BF16), vaug], axis=0), preferred_element_type=F32)
        upd = lax.dot_general(kw, vaug, dn_tn, preferred_element_type=F32)
        stage2.append((num, upd, caug))

    for ((d, qk, v, g, mask, bcol, brow, grow, out_ref), h), (lhs, kw, m_t, m_new, decay), (num, upd, caug) in zip(
            units, stage1, stage2):
        idx = d * M_HEADS + h
        den = num[:, V_DIM:V_DIM + 1]
        hout = num[:, :V_DIM] / jnp.maximum(jnp.abs(den), jnp.exp(-m_t))
        c_scr[idx][...] = decay * caug + upd
        m_scr[idx][...] = jnp.broadcast_to(m_new, m_scr[idx].shape)
        out_ref[:, h * V_DIM:(h + 1) * V_DIM] = hout.astype(out_ref.dtype)


def _mlstm(qk_ctx, qk_lat, z_ctx3, z_lat3, g_ctx3, g_lat3):
    b, tc, _ = qk_ctx.shape
    t = qk_lat.shape[1]
    nc, nl = tc // CHUNK, t // CHUNK
    vw = M_HEADS * V_DIM
    qkw = 2 * M_HEADS * QK_DIM

    f_ctx = lambda i, s: (i, jnp.minimum(s, nc - 1), 0)
    f_lat = lambda i, s: (i, jnp.clip(s - nc, 0, nl - 1), 0)
    b_ctx = lambda i, s: (i, jnp.maximum(nc - 1 - s, 0), 0)
    b_lat = lambda i, s: (i, jnp.clip(nc + nl - 1 - s, 0, nl - 1), 0)
    v_of = lambda f: (lambda i, s: (f(i, s)[0], f(i, s)[1], 1))

    def specs(fc, fl):
        return [pl.BlockSpec((None, CHUNK, qkw), fc), pl.BlockSpec((None, CHUNK, qkw), fl),
                pl.BlockSpec((None, CHUNK, vw), v_of(fc)), pl.BlockSpec((None, CHUNK, vw), v_of(fl)),
                pl.BlockSpec((None, CHUNK, LANES), fc), pl.BlockSpec((None, CHUNK, LANES), fl)]

    kern = functools.partial(_mlstm_kernel, n_ctx_chunks=nc)
    args = (qk_ctx, qk_lat, z_ctx3, z_lat3, g_ctx3, g_lat3)
    return pl.pallas_call(
        kern,
        grid=(b, nc + nl),
        in_specs=specs(f_ctx, f_lat) + specs(b_ctx, b_lat),
        out_specs=[pl.BlockSpec((None, CHUNK, vw), f_lat), pl.BlockSpec((None, CHUNK, vw), b_lat)],
        out_shape=[jax.ShapeDtypeStruct((b, t, vw), BF16), jax.ShapeDtypeStruct((b, t, vw), BF16)],
        scratch_shapes=([pltpu.VMEM((QK_DIM, V_DIM + LANES), F32)] * (2 * M_HEADS)
                        + [pltpu.VMEM((8, LANES), F32)] * (2 * M_HEADS)),
        compiler_params=_cparams(("arbitrary", "arbitrary")),
        name="mlstm",
    )(*args, *args)


MERGE_ROWS = 256


def _merge_kernel(hf_ref, hb_ref, o_ref, u_ref, sv_ref, ga_ref, gb_ref, gm_ref, gs_ref, sw_ref, sb_ref,
                  pa_ref, pb_ref, out_ref):
    tm, d = out_ref.shape
    gw = d // SGU_GROUPS
    for grp in range(tm // MERGE_ROWS):
        r0 = grp * MERGE_ROWS
        rs = slice(r0, r0 + MERGE_ROWS)
        a_parts = []
        for h in range(M_HEADS):
            sl = slice(h * V_DIM, (h + 1) * V_DIM)
            hs = hf_ref[rs, sl].astype(F32) + hb_ref[rs, sl].astype(F32)
            ms = jnp.mean(hs * hs, axis=-1, keepdims=True)
            y = hs * lax.rsqrt(ms + EPS) * gm_ref[:, sl]
            a_parts.append((y * o_ref[rs, sl].astype(F32)).astype(BF16))
        a = jnp.concatenate(a_parts, axis=1)
        sv = sv_ref[rs, :].astype(F32)
        ms = jnp.mean(sv * sv, axis=-1, keepdims=True)
        svn = (sv * lax.rsqrt(ms + EPS) * gs_ref[...]).astype(BF16)
        b_rows = []
        for c in range(MERGE_ROWS // CHUNK):
            cr = slice(c * CHUNK, (c + 1) * CHUNK)
            b_parts = []
            for gi in range(SGU_GROUPS):
                cs = slice(gi * gw, (gi + 1) * gw)
                zz = jnp.dot(sw_ref[gi], svn[cr, cs], preferred_element_type=F32) + sb_ref[:, gi:gi + 1]
                b_parts.append((u_ref[r0 + c * CHUNK:r0 + (c + 1) * CHUNK, cs].astype(F32) * zz).astype(BF16))
            b_rows.append(jnp.concatenate(b_parts, axis=1))
        b = jnp.concatenate(b_rows, axis=0)
        ya = jnp.dot(a, pa_ref[...], preferred_element_type=F32)
        yb = jnp.dot(b, pb_ref[...], preferred_element_type=F32)
        out_ref[rs, :] = (ga_ref[rs, :].astype(F32) * ya + gb_ref[rs, :].astype(F32) * yb).astype(BF16)


def _merge(hf, hb, z, gm, gs, sgu_w, sgu_bt, pa, pb):
    r, d = hf.shape
    tm = 512
    row = lambda k: (lambda i: (i, k))
    const = lambda shape: pl.BlockSpec(shape, lambda i: (0,) * len(shape), pipeline_mode=pl.Buffered(1))
    return pl.pallas_call(
        _merge_kernel,
        grid=(r // tm,),
        in_specs=[pl.BlockSpec((tm, d), row(0)), pl.BlockSpec((tm, d), row(0)),
                  pl.BlockSpec((tm, d), row(2)), pl.BlockSpec((tm, d), row(3)), pl.BlockSpec((tm, d), row(4)),
                  pl.BlockSpec((tm, d), row(5)), pl.BlockSpec((tm, d), row(6)),
                  const((1, d)), const((1, d)), const(sgu_w.shape), const(sgu_bt.shape),
                  const((d, d)), const((d, d))],
        out_specs=pl.BlockSpec((tm, d), row(0)),
        out_shape=jax.ShapeDtypeStruct((r, d), BF16),
        compiler_params=_cparams(("arbitrary",)),
        name="merge",
    )(hf, hb, z, z, z, z, z, gm, gs, sgu_w, sgu_bt, pa, pb)


def _outproj_kernel(m_ref, w_ref, x_ref, gate_ref, shift_ref, scale_ref, g_ref, rw_ref, rb_ref,
                    x1_ref, h2_ref, lg_ref):
    r1, r2, _ = _split3(rw_ref[...])
    ne = lg_ref.shape[0]
    for grp in range(x_ref.shape[0] // MERGE_ROWS):
        r0 = grp * MERGE_ROWS
        rs = slice(r0, r0 + MERGE_ROWS)
        out = jnp.dot(m_ref[rs, :], w_ref[...], preferred_element_type=F32)
        x1 = x_ref[rs, :] + gate_ref[...] * out
        x1_ref[rs, :] = x1
        ms = jnp.mean(x1 * x1, axis=-1, keepdims=True)
        h2 = (x1 * lax.rsqrt(ms + EPS) * g_ref[...]) * (1.0 + scale_ref[...]) + shift_ref[...]
        hi, lo, _ = _split3(h2)
        lg = (jnp.dot(hi, r1, preferred_element_type=F32) + jnp.dot(lo, r1, preferred_element_type=F32)
              + jnp.dot(hi, r2, preferred_element_type=F32)) + rb_ref[...]
        lg_ref[:, rs] = lg.T[:ne, :]
        _store_token_major(h2_ref, r0, h2)


def _outproj(merged, w_out, x2d, mod3, mod_row_of_tile, norm_g, rw, rb, tm):
    r, d = x2d.shape
    ne = rw.shape[1]
    rw_t = jnp.pad(rw, ((0, 0), (0, LANES - ne)))
    rb_col = jnp.pad(rb.reshape(1, ne), ((0, 0), (0, LANES - ne)))
    modspec = lambda k: pl.BlockSpec((None, 1, d), lambda i: (mod_row_of_tile(i), 0, k))
    return pl.pallas_call(
        _outproj_kernel,
        grid=(r // tm,),
        in_specs=[pl.BlockSpec((tm, d), lambda i: (i, 0)),
                  pl.BlockSpec((d, d), lambda i: (0, 0), pipeline_mode=pl.Buffered(1)),
                  pl.BlockSpec((tm, d), lambda i: (i, 0)),
                  modspec(2), modspec(3), modspec(4),
                  pl.BlockSpec((1, d), lambda i: (0, 0)),
                  pl.BlockSpec((d, LANES), lambda i: (0, 0)),
                  pl.BlockSpec((1, LANES), lambda i: (0, 0))],
        out_specs=[pl.BlockSpec((tm, d), lambda i: (i, 0)),
                   pl.BlockSpec((tm * TOKEN_ROWS, LANES), lambda i: (i, 0)),
                   pl.BlockSpec((ne, tm), lambda i: (0, i))],
        out_shape=[jax.ShapeDtypeStruct((r, d), F32),
                   jax.ShapeDtypeStruct((r * TOKEN_ROWS, LANES), F32),
                   jax.ShapeDtypeStruct((ne, r), F32)],
        compiler_params=_cparams(("arbitrary",)),
        name="outproj",
    )(merged, w_out, x2d, mod3, mod3, mod3, norm_g, rw_t, rb_col)


def _route_kernel(lg_ref, eid_ref, gate_ref, rank_ref, cnt_ref, carry_scr):
    i = pl.program_id(0)

    @pl.when(i == 0)
    def _():
        carry_scr[...] = jnp.zeros_like(carry_scr)

    l = lg_ref[...]
    ne, tm = l.shape
    e_iota = lax.broadcasted_iota(I32, (ne, tm), 0)
    vals, onehots = [], []
    for k in range(TOP_K):
        mx = jnp.max(l, axis=0, keepdims=True)
        idx = jnp.min(jnp.where(l == mx, e_iota, ne), axis=0, keepdims=True)
        oh = e_iota == idx
        l = jnp.where(oh, -jnp.inf, l)
        vals.append(mx)
        onehots.append(oh)
        eid_ref[k:k + 1, :] = idx
    ex = [jnp.exp(vk - vals[0]) for vk in vals]
    tot = ex[0] + ex[1] + ex[2] + ex[3]
    for k in range(TOP_K):
        gate_ref[k:k + 1, :] = ex[k] / tot

    oh_all = jnp.zeros((ne, tm), F32)
    for oh in onehots:
        oh_all = oh_all + jnp.where(oh, 1.0, 0.0)
    oh_all = oh_all.astype(BF16)
    r_i = lax.broadcasted_iota(I32, (LANES, LANES), 0)
    c_i = lax.broadcasted_iota(I32, (LANES, LANES), 1)
    tri_excl = jnp.where(r_i < c_i, 1.0, 0.0).astype(BF16)
    ones_m = jnp.ones((LANES, LANES), BF16)
    carry = carry_scr[...]
    for blk in range(tm // LANES):
        sl = slice(blk * LANES, (blk + 1) * LANES)
        ohb = oh_all[:, sl]
        cum = jnp.dot(ohb, tri_excl, preferred_element_type=F32) + carry
        for k in range(TOP_K):
            rk = jnp.sum(jnp.where(onehots[k][:, sl], cum, 0.0), axis=0, keepdims=True)
            rank_ref[k:k + 1, sl] = rk.astype(I32)
        carry = carry + jnp.dot(ohb, ones_m, preferred_element_type=F32)
    carry_scr[...] = carry
    cnt_ref[...] = carry


def _route(logits_t):
    ne, r = logits_t.shape
    tm = 1024
    return pl.pallas_call(
        _route_kernel,
        grid=(r // tm,),
        in_specs=[pl.BlockSpec((ne, tm), lambda i: (0, i))],
        out_specs=[pl.BlockSpec((TOP_K, tm), lambda i: (0, i)),
                   pl.BlockSpec((TOP_K, tm), lambda i: (0, i)),
                   pl.BlockSpec((TOP_K, tm), lambda i: (0, i)),
                   pl.BlockSpec((ne, LANES), lambda i: (0, 0))],
        out_shape=[jax.ShapeDtypeStruct((TOP_K, r), I32),
                   jax.ShapeDtypeStruct((TOP_K, r), F32),
                   jax.ShapeDtypeStruct((TOP_K, r), I32),
                   jax.ShapeDtypeStruct((ne, LANES), F32)],
        scratch_shapes=[pltpu.VMEM((ne, LANES), F32)],
        compiler_params=_cparams(("arbitrary",)),
        name="route",
    )(logits_t)


def _slot_kernel(eid_ref, rank_ref, cnt_ref, slot_ref):
    ne = cnt_ref.shape[0]
    nblk = jnp.floor((cnt_ref[...] + (MOE_TILE - 1.0)) * (1.0 / MOE_TILE))
    r_i = lax.broadcasted_iota(I32, (ne, ne), 0)
    c_i = lax.broadcasted_iota(I32, (ne, ne), 1)
    tri = jnp.where(c_i < r_i, 1.0, 0.0).astype(BF16)
    start = jnp.dot(tri, nblk.astype(BF16), preferred_element_type=F32) * MOE_TILE
    eid = eid_ref[...]
    acc = rank_ref[...]
    for e in range(ne):
        acc = acc + jnp.where(eid == e, start[e:e + 1, 0:1].astype(I32), 0)
    slot_ref[...] = acc


def _slots(eid, rank, cnt):
    return pl.pallas_call(
        _slot_kernel,
        out_shape=jax.ShapeDtypeStruct(eid.shape, I32),
        compiler_params=pltpu.CompilerParams(vmem_limit_bytes=V7X_VMEM_LIMIT_BYTES),
        name="slots",
    )(eid, rank, cnt)


def _dispatch_copy(slot_ref, h_ref, xs_ref, sem, k, t, base, n_tok):
    d0 = pl.multiple_of(slot_ref[k * n_tok + base + t] * TOKEN_ROWS, TOKEN_ROWS)
    s0 = pl.multiple_of(t * TOKEN_ROWS, TOKEN_ROWS)
    return pltpu.make_async_copy(h_ref.at[pl.ds(s0, TOKEN_ROWS), :], xs_ref.at[pl.ds(d0, TOKEN_ROWS), :], sem)


def _zero_copy(zbuf, xs_ref, zsem, row, n_rows):
    d0 = pl.multiple_of(row * TOKEN_ROWS, TOKEN_ROWS)
    return pltpu.make_async_copy(zbuf.at[pl.ds(0, n_rows * TOKEN_ROWS), :],
                                 xs_ref.at[pl.ds(d0, n_rows * TOKEN_ROWS), :], zsem)


def _zero_fill(start_ref, cnt_ref, zbuf, xs_ref, zsem, p_rows, begin):
    ne = start_ref.shape[0]

    def op(cp):
        cp.start() if begin else cp.wait()

    def per_expert(e, used):
        n = cnt_ref[e]
        s0 = start_ref[e]
        c_step = (n + MOE_STEP - 1) // MOE_STEP * MOE_STEP
        c_tile = (n + MOE_TILE - 1) // MOE_TILE * MOE_TILE

        def one_row(r, c):
            op(_zero_copy(zbuf, xs_ref, zsem, s0 + r, 1))
            return c

        def one_unit(u, c):
            op(_zero_copy(zbuf, xs_ref, zsem, s0 + u * MOE_STEP, MOE_STEP))
            return c

        lax.fori_loop(n, c_step, one_row, 0)
        lax.fori_loop(c_step // MOE_STEP, c_tile // MOE_STEP, one_unit, 0)
        return s0 + c_tile

    used = lax.fori_loop(0, ne, per_expert, 0)

    def trailing(u, c):
        op(_zero_copy(zbuf, xs_ref, zsem, u * MOE_STEP, MOE_STEP))
        return c

    lax.fori_loop(used // MOE_STEP, p_rows // MOE_STEP, trailing, 0)


def _dispatch_kernel(slot_ref, start_ref, cnt_ref, h_ref, xs_ref, zbuf, sem, zsem, *, tm, n_tok, p_rows):
    i = pl.program_id(0)
    base = i * tm

    @pl.when(i == 0)
    def _():
        zbuf[...] = jnp.zeros_like(zbuf)
        _zero_fill(start_ref, cnt_ref, zbuf, xs_ref, zsem, p_rows, True)

    def start(t, carry):
        for k in range(TOP_K):
            _dispatch_copy(slot_ref, h_ref, xs_ref, sem, k, t, base, n_tok).start()
        return carry

    def wait(t, carry):
        for k in range(TOP_K):
            _dispatch_copy(slot_ref, h_ref, xs_ref, sem, k, t, base, n_tok).wait()
        return carry

    lax.fori_loop(0, tm, start, 0, unroll=4)
    lax.fori_loop(0, tm, wait, 0, unroll=4)

    @pl.when(i == pl.num_programs(0) - 1)
    def _():
        _zero_fill(start_ref, cnt_ref, zbuf, xs_ref, zsem, p_rows, False)


def _dispatch(h2p, slot, start_pad, counts, p_rows):
    n_tok = h2p.shape[0] // TOKEN_ROWS
    tm = 512
    grid_spec = pltpu.PrefetchScalarGridSpec(
        num_scalar_prefetch=3,
        grid=(n_tok // tm,),
        in_specs=[pl.BlockSpec((tm * TOKEN_ROWS, LANES), lambda i, *_: (i, 0))],
        out_specs=pl.BlockSpec(memory_space=pl.ANY),
        scratch_shapes=[pltpu.VMEM((MOE_STEP * TOKEN_ROWS, LANES), h2p.dtype),
                        pltpu.SemaphoreType.DMA(()), pltpu.SemaphoreType.DMA(())],
    )
    return pl.pallas_call(
        functools.partial(_dispatch_kernel, tm=tm, n_tok=n_tok, p_rows=p_rows),
        grid_spec=grid_spec,
        out_shape=jax.ShapeDtypeStruct((p_rows * TOKEN_ROWS, LANES), h2p.dtype),
        compiler_params=_cparams(("arbitrary",)),
        name="dispatch",
    )(slot, start_pad, counts, h2p)


def _weight_copies(w_hbm, wbuf, sems, e, col_starts, slot):
    tn = wbuf.shape[-1]
    aligned = lambda c: c if isinstance(c, int) else pl.multiple_of(c, LANES)
    return [pltpu.make_async_copy(w_hbm.at[e, :, pl.ds(aligned(c), tn)], wbuf.at[slot, i], sems.at[slot])
            for i, c in enumerate(col_starts)]


def _advance_weight_pipeline(w_hbm, wbuf, sems, sched, j, nj, it, col_starts_of):
    e_ref, run_ref, last_ref, nxt_ref, nruns_ref = sched
    slot = (j * nruns_ref[0] + run_ref[it]) % 2

    @pl.when((j == 0) & (it == 0))
    def _():
        for cp in _weight_copies(w_hbm, wbuf, sems, e_ref[it], col_starts_of(j), slot):
            cp.start()

    for cp in _weight_copies(w_hbm, wbuf, sems, e_ref[it], col_starts_of(j), slot):
        cp.wait()
    is_last = last_ref[it] == 1

    @pl.when(jnp.logical_not(is_last))
    def _():
        for cp in _weight_copies(w_hbm, wbuf, sems, nxt_ref[it], col_starts_of(j), 1 - slot):
            cp.start()

    @pl.when(is_last & (j + 1 < nj))
    def _():
        for cp in _weight_copies(w_hbm, wbuf, sems, nxt_ref[it], col_starts_of(j + 1), 1 - slot):
            cp.start()

    return slot


def _moe_up_kernel(blk_ref, oblk_ref, e_ref, nv_ref, new_ref, run_ref, last_ref, nxt_ref, nruns_ref,
                   xs_ref, w_hbm, bg_ref, bl_ref, act_ref, wbuf, sems, wg_bf, wl_bf):
    j = pl.program_id(0)
    nj = pl.num_programs(0)
    it = pl.program_id(1)
    tn = wg_bf.shape[1]

    @pl.when(new_ref[it] == 1)
    def _():
        slot = _advance_weight_pipeline(w_hbm, wbuf, sems, (e_ref, run_ref, last_ref, nxt_ref, nruns_ref), j, nj, it,
                                        lambda jj: (jj * tn, (nj + jj) * tn))
        wg_bf[...] = wbuf[slot, 0].astype(BF16)
        wl_bf[...] = wbuf[slot, 1].astype(BF16)

    def rows_chunk(r0, n):
        x = _load_token_major(xs_ref, r0, n).astype(BF16)
        hg = jnp.dot(x, wg_bf[...], preferred_element_type=F32) + bg_ref[...]
        hl = jnp.dot(x, wl_bf[...], preferred_element_type=F32) + bl_ref[...]
        glu = jnp.minimum(hg, SWIGLU_LIMIT)
        lin = jnp.clip(hl, -SWIGLU_LIMIT, SWIGLU_LIMIT)
        act_ref[r0:r0 + n, :] = ((0.5 * glu) * (1.0 + jnp.tanh((0.5 * SWIGLU_ALPHA) * glu)) * (lin + 1.0)).astype(BF16)

    def rows_zero(r0, n):
        act_ref[r0:r0 + n, :] = jnp.zeros((n, act_ref.shape[1]), BF16)

    _moe_row_variants(nv_ref[it], rows_chunk, rows_zero, MOE_UP_CHUNK)


def _moe_row_variants(nv, rows_chunk, rows_zero, chunk):
    for units in range(MOE_TILE // MOE_STEP + 1):
        rows = units * MOE_STEP

        @pl.when((nv > rows - MOE_STEP) & (nv <= rows) if units else nv <= 0)
        def _():
            r0 = 0
            while r0 < rows:
                n = min(chunk, rows - r0)
                rows_chunk(r0, n)
                r0 += n
            if rows < MOE_TILE:
                rows_zero(rows, MOE_TILE - rows)


def _moe_down_kernel(blk_ref, oblk_ref, e_ref, nv_ref, new_ref, run_ref, last_ref, nxt_ref, nruns_ref,
                     a_ref, w_hbm, b_ref, y_ref, wbuf, sems, w_bf):
    it = pl.program_id(0)

    @pl.when(new_ref[it] == 1)
    def _():
        slot = _advance_weight_pipeline(w_hbm, wbuf, sems, (e_ref, run_ref, last_ref, nxt_ref, nruns_ref), 0, 1, it,
                                        lambda jj: (0,))
        w_bf[...] = wbuf[slot, 0].astype(BF16)

    def rows_chunk(r0, n):
        y = jnp.dot(a_ref[r0:r0 + n, :], w_bf[...], preferred_element_type=F32) + b_ref[...]
        _store_token_major(y_ref, r0, y)

    def rows_zero(r0, n):
        y_ref[r0 * TOKEN_ROWS:(r0 + n) * TOKEN_ROWS, :] = jnp.zeros((n * TOKEN_ROWS, LANES), F32)

    _moe_row_variants(nv_ref[it], rows_chunk, rows_zero, MOE_DOWN_CHUNK)


def _moe_up(items, xs, w1, b1):
    n_items = items[0].shape[0]
    ne, d, two_f = w1.shape
    f = two_f // 2
    nj = f // MOE_TN
    p = xs.shape[0] // TOKEN_ROWS
    b1r = b1.reshape(ne, 1, two_f)
    grid_spec = pltpu.PrefetchScalarGridSpec(
        num_scalar_prefetch=len(items),
        grid=(nj, n_items),
        in_specs=[pl.BlockSpec((MOE_TILE * TOKEN_ROWS, LANES), lambda j, it, blk, *_: (blk[it], 0)),
                  pl.BlockSpec(memory_space=pl.ANY),
                  pl.BlockSpec((None, 1, MOE_TN), lambda j, it, blk, oblk, e, *_: (e[it], 0, j)),
                  pl.BlockSpec((None, 1, MOE_TN), lambda j, it, blk, oblk, e, *_: (e[it], 0, nj + j))],
        out_specs=pl.BlockSpec((MOE_TILE, MOE_TN), lambda j, it, blk, oblk, *_: (oblk[it], j)),
        scratch_shapes=[pltpu.VMEM((2, 2, d, MOE_TN), F32), pltpu.SemaphoreType.DMA((2,)),
                        pltpu.VMEM((d, MOE_TN), BF16), pltpu.VMEM((d, MOE_TN), BF16)],
    )
    return pl.pallas_call(
        _moe_up_kernel,
        grid_spec=grid_spec,
        out_shape=jax.ShapeDtypeStruct((p, f), BF16),
        compiler_params=_cparams(("arbitrary", "arbitrary")),
        name="moe_up",
    )(*items, xs, w1, b1r, b1r)


def _moe_down(items, act, w2, b2):
    n_items = items[0].shape[0]
    ne, f, d = w2.shape
    p = act.shape[0]
    b2r = b2.reshape(ne, 1, d)
    grid_spec = pltpu.PrefetchScalarGridSpec(
        num_scalar_prefetch=len(items),
        grid=(n_items,),
        in_specs=[pl.BlockSpec((MOE_TILE, f), lambda it, blk, *_: (blk[it], 0)),
                  pl.BlockSpec(memory_space=pl.ANY),
                  pl.BlockSpec((None, 1, d), lambda it, blk, oblk, e, *_: (e[it], 0, 0))],
        out_specs=pl.BlockSpec((MOE_TILE * TOKEN_ROWS, LANES), lambda it, blk, oblk, *_: (oblk[it], 0)),
        scratch_shapes=[pltpu.VMEM((2, 1, f, d), F32), pltpu.SemaphoreType.DMA((2,)), pltpu.VMEM((f, d), BF16)],
    )
    return pl.pallas_call(
        _moe_down_kernel,
        grid_spec=grid_spec,
        out_shape=jax.ShapeDtypeStruct((p * TOKEN_ROWS, LANES), F32),
        compiler_params=_cparams(("arbitrary",)),
        name="moe_down",
    )(*items, act, w2, b2r)


FINAL_ROWS = 64


def _combine_copy(slot_ref, y_ref, buf, sems, k, t, tile, par, tm, n_tok):
    s0 = pl.multiple_of(slot_ref[k * n_tok + tile * tm + t] * TOKEN_ROWS, TOKEN_ROWS)
    d0 = pl.multiple_of(t * TOKEN_ROWS, TOKEN_ROWS)
    return pltpu.make_async_copy(y_ref.at[pl.ds(s0, TOKEN_ROWS), :], buf.at[par, k, pl.ds(d0, TOKEN_ROWS), :],
                                 sems.at[par])


def _final_kernel(slot_ref, y_ref, gates_ref, x1_ref, gate_ref, g_ref, o_ref, buf, sems, *, n_tok):
    tm, d = x1_ref.shape
    i = pl.program_id(0)
    par = i % 2

    def gather(tile, par_, start):
        def body(t, carry):
            for k in range(TOP_K):
                cp = _combine_copy(slot_ref, y_ref, buf, sems, k, t, tile, par_, tm, n_tok)
                cp.start() if start else cp.wait()
            return carry
        lax.fori_loop(0, tm, body, 0, unroll=4)

    @pl.when(i == 0)
    def _():
        gather(0, 0, True)

    @pl.when(i + 1 < pl.num_programs(0))
    def _():
        gather(i + 1, 1 - par, True)

    gather(i, par, False)

    for grp in range(tm // FINAL_ROWS):
        r0 = grp * FINAL_ROWS
        rs = slice(r0, r0 + FINAL_ROWS)
        gk = [jnp.broadcast_to(gates_ref[rs, k:k + 1], (FINAL_ROWS, LANES)) for k in range(TOP_K)]
        sq = jnp.zeros((FINAL_ROWS, LANES), F32)
        for s in range(TOKEN_ROWS):
            cs = slice(s * LANES, (s + 1) * LANES)
            rows = pl.ds(r0 * TOKEN_ROWS + s, FINAL_ROWS, stride=TOKEN_ROWS)
            acc = gk[0] * buf[par, 0, rows, :]
            for k in range(1, TOP_K):
                acc = acc + gk[k] * buf[par, k, rows, :]
            xf = x1_ref[rs, cs] + gate_ref[:, cs] * acc
            o_ref[rs, cs] = xf
            sq = sq + xf * xf
        scale = lax.rsqrt(jnp.sum(sq, axis=-1, keepdims=True) * (1.0 / d) + EPS)
        for s in range(TOKEN_ROWS):
            cs = slice(s * LANES, (s + 1) * LANES)
            o_ref[rs, cs] = o_ref[rs, cs] * scale * g_ref[:, cs]


def _final(slot, y, gates_tk, x1, mod3, mod_row_of_tile, final_g, tm):
    r, d = x1.shape
    grid_spec = pltpu.PrefetchScalarGridSpec(
        num_scalar_prefetch=1,
        grid=(r // tm,),
        in_specs=[pl.BlockSpec(memory_space=pl.ANY),
                  pl.BlockSpec((tm, TOP_K), lambda i, slot: (i, 0)),
                  pl.BlockSpec((tm, d), lambda i, slot: (i, 0)),
                  pl.BlockSpec((None, 1, d), lambda i, slot: (mod_row_of_tile(i), 0, 5)),
                  pl.BlockSpec((1, d), lambda i, slot: (0, 0))],
        out_specs=pl.BlockSpec((tm, d), lambda i, slot: (i, 0)),
        scratch_shapes=[pltpu.VMEM((2, TOP_K, tm * TOKEN_ROWS, LANES), F32), pltpu.SemaphoreType.DMA((2,))],
    )
    return pl.pallas_call(
        functools.partial(_final_kernel, n_tok=r),
        grid_spec=grid_spec,
        out_shape=jax.ShapeDtypeStruct((r, d), F32),
        compiler_params=_cparams(("arbitrary",)),
        name="final",
    )(slot, y, gates_tk, x1, mod3, final_g)


def _moe_schedule(counts, n_items):
    ne = counts.shape[0]
    padded = (counts + MOE_TILE - 1) // MOE_TILE * MOE_TILE
    pad_end = jnp.cumsum(padded)
    start_pad = pad_end - padded
    n_real = pad_end[-1] // MOE_TILE
    b = jnp.arange(n_items, dtype=I32)
    blk = jnp.minimum(b, jnp.maximum(n_real - 1, 0))
    blk_start = blk * MOE_TILE
    e_of = jnp.minimum(jnp.sum((pad_end[None, :] <= blk_start[:, None]).astype(I32), axis=1), ne - 1)
    nv = jnp.clip(counts[e_of] - (blk_start - start_pad[e_of]), 0, MOE_TILE)
    nv = jnp.where(b < n_real, nv, 0)
    e_of = e_of.astype(I32)
    prev_e = jnp.concatenate([jnp.full((1,), -1, I32), e_of[:-1]])
    new = ((e_of != prev_e) | (b == 0)).astype(I32)
    run = jnp.cumsum(new) - 1
    n_runs = run[-1] + 1
    last = (run == n_runs - 1).astype(I32)
    first_pos = jnp.where(new == 1, b, n_items)
    nxt_pos = jnp.concatenate([lax.cummin(first_pos, reverse=True)[1:], jnp.full((1,), n_items, I32)])
    nxt = jnp.where(nxt_pos < n_items, e_of[jnp.minimum(nxt_pos, n_items - 1)], e_of[0])
    return start_pad.astype(I32), (blk.astype(I32), b, e_of, nv.astype(I32), new, run.astype(I32), last,
                                   nxt.astype(I32), n_runs.reshape(1).astype(I32))


def kernel(x, c, ctx, c_ctx, mod_w, mod_b, norm1_g, norm2_g, w_in, gate_b, conv_w, conv_b, mlstm_norm_g,
           sgu_norm_g, sgu_w, sgu_b, proj_a, proj_b, w_out, router_w, router_b, exp_w1, exp_b1, exp_w2,
           exp_b2, final_g):
    bsz, t, d = x.shape
    tc = ctx.shape[1]
    depth = mod_w.shape[0]
    assert depth == 1, "single-layer block"
    assert d == TOKEN_ROWS * LANES, "token-major rows hold TOKEN_ROWS*128 features"
    l = 0
    r = bsz * t
    qkw = 2 * M_HEADS * QK_DIM
    vw = M_HEADS * V_DIM
    off_gates = qkw + vw
    n_gates = 4 * M_HEADS
    off_o = off_gates + n_gates

    cc = jnp.zeros((8, d), F32).at[:bsz].set(c).at[bsz].set(c_ctx)
    mod3 = _mod(cc, mod_w[l], mod_b[l]).reshape(8, 1, N_MOD * d)

    in_cols = w_in.shape[2]
    assert off_gates % INPROJ_TN == 0 and (in_cols - off_o) % INPROJ_TN == 0
    n_aligned = off_gates // INPROJ_TN
    n_tiles = n_aligned + (in_cols - off_o) // INPROJ_TN
    w_in_t = jnp.swapaxes(w_in, 1, 2)
    w_gates = jnp.pad(w_in_t[l, off_gates:off_o, :].T, ((0, 0), (0, LANES - n_gates)))
    gb = jnp.pad(gate_b[l].reshape(1, n_gates), ((0, 0), (0, LANES - n_gates)))
    g1 = norm1_g[l].reshape(1, d)

    tm_in = 512
    h_lat, g_lat = _norm(x.reshape(r, d), mod3, lambda i: i // (t // tm_in), g1, w_gates, gb, tm_in)
    h_ctx, g_ctx = _norm(ctx.reshape(bsz * tc, d), mod3, lambda i: bsz, g1, w_gates, gb, tc)
    z_lat = _inproj(h_lat, w_in_t, l, n_aligned, n_gates, n_tiles, 2048)
    z_ctx = _inproj(h_ctx, w_in_t, l, n_aligned, n_gates, n_aligned, bsz * tc)

    z_lat3 = z_lat.reshape(bsz, t, z_lat.shape[1])
    z_ctx3 = z_ctx.reshape(bsz, tc, z_ctx.shape[1])
    qk_lat = _conv(z_lat3, conv_w[l], conv_b[l])
    qk_ctx = _conv(z_ctx3, conv_w[l], conv_b[l])

    hf, hb = _mlstm(qk_ctx, qk_lat, z_ctx3, z_lat3, g_ctx.reshape(bsz, tc, LANES), g_lat.reshape(bsz, t, LANES))

    merged = _merge(hf.reshape(r, vw), hb.reshape(r, vw), z_lat,
                    mlstm_norm_g[l].reshape(1, vw), sgu_norm_g[l].reshape(1, -1),
                    sgu_w[l].astype(BF16), sgu_b[l].T, proj_a[l].astype(BF16), proj_b[l].astype(BF16))

    tm_out = 512
    x1, h2p, logits_t = _outproj(merged, w_out[l].astype(BF16), x.reshape(r, d), mod3,
                                 lambda i: i // (t // tm_out), norm2_g[l].reshape(1, d),
                                 router_w[l], router_b[l], tm_out)
    tm_fin = 256

    eid, gates, rank, cnt = _route(logits_t)
    counts = cnt[:, 0].astype(I32)
    n_items = (r * TOP_K) // MOE_TILE + N_EXPERTS
    start_pad, items = _moe_schedule(counts, n_items)
    slot = _slots(eid, rank, cnt).reshape(-1)
    xs = _dispatch(h2p, slot, start_pad, counts, n_items * MOE_TILE)
    act = _moe_up(items, xs, exp_w1[l], exp_b1[l])
    y = _moe_down(items, act, exp_w2[l], exp_b2[l])
    out = _final(slot, y, gates.T, x1, mod3, lambda i: i // (t // tm_fin), final_g.reshape(1, d), tm_fin)
    return out.reshape(bsz, t, d)
```

```python
import functools

import jax
import jax.numpy as jnp
from jax import lax
from jax.experimental import pallas as pl
from jax.experimental.pallas import tpu as pltpu

F32 = jnp.float32
BF16 = jnp.bfloat16
I32 = jnp.int32
U32 = jnp.uint32

EPS = 1e-6
M_HEADS = 8
QK_DIM = 128
V_DIM = 256
CONV_W = 5
CHUNK = 128
SGU_GROUPS = 8
N_EXPERTS = 32
TOP_K = 4
SWIGLU_ALPHA = 1.702
SWIGLU_LIMIT = 7.0
N_MOD = 6

V7X_VMEM_LIMIT_BYTES = 56 * 1024 * 1024
LANES = 128
MOE_TILE = 512
MOE_HALF = 256
MOE_STEP = 128
MOE_TN = 1024


def _cparams(sem):
    return pltpu.CompilerParams(dimension_semantics=sem, vmem_limit_bytes=V7X_VMEM_LIMIT_BYTES)


def _sigmoid(x):
    return 1.0 / (1.0 + jnp.exp(-x))


def _gelu_tanh(x):
    return 0.5 * x * (1.0 + jnp.tanh(0.7978845608028654 * (x + 0.044715 * (x * x * x))))


def _split3(a):
    a1 = a.astype(BF16)
    r1 = a - a1.astype(F32)
    a2 = r1.astype(BF16)
    a3 = (r1 - a2.astype(F32)).astype(BF16)
    return a1, a2, a3


TOKEN_ROWS = 16


def _store_token_major(ref, row0, vals):
    n = vals.shape[0]
    for s in range(TOKEN_ROWS):
        ref[pl.ds(row0 * TOKEN_ROWS + s, n, stride=TOKEN_ROWS), :] = vals[:, s * LANES:(s + 1) * LANES]


def _load_token_major(ref, row0, n, lead=()):
    cols = [ref[(*lead, pl.ds(row0 * TOKEN_ROWS + s, n, stride=TOKEN_ROWS), slice(None))] for s in range(TOKEN_ROWS)]
    return jnp.concatenate(cols, axis=1)


def _mod_kernel(s_ref, w_ref, b_ref, o_ref):
    s = s_ref[...]
    s = s * _sigmoid(s)
    o_ref[...] = jnp.dot(s.astype(BF16), w_ref[...].astype(BF16), preferred_element_type=F32) + b_ref[...]


def _mod(cc, mod_w, mod_b):
    d, n = mod_w.shape
    tn = 1024
    return pl.pallas_call(
        _mod_kernel,
        grid=(n // tn,),
        in_specs=[pl.BlockSpec((8, d), lambda j: (0, 0)),
                  pl.BlockSpec((d, tn), lambda j: (0, j)),
                  pl.BlockSpec((1, tn), lambda j: (0, j))],
        out_specs=pl.BlockSpec((8, tn), lambda j: (0, j)),
        out_shape=jax.ShapeDtypeStruct((8, n), F32),
        compiler_params=_cparams(("arbitrary",)),
        name="mod",
    )(cc, mod_w, mod_b.reshape(1, n))


def _norm_kernel(x_ref, shift_ref, scale_ref, g_ref, wg_ref, gb_ref, h_ref, gates_ref):
    x = x_ref[...]
    ms = jnp.mean(x * x, axis=-1, keepdims=True)
    h = (x * lax.rsqrt(ms + EPS) * g_ref[...]) * (1.0 + scale_ref[...]) + shift_ref[...]
    h1, h2, _ = _split3(h)
    h_ref[...] = h1
    w1, w2, _ = _split3(wg_ref[...])
    gates_ref[...] = (jnp.dot(h1, w1, preferred_element_type=F32)
                      + jnp.dot(h1, w2, preferred_element_type=F32)
                      + jnp.dot(h2, w1, preferred_element_type=F32)) + gb_ref[...]


def _norm(x2d, mod3, mod_row_of_tile, norm_g, w_gates, gate_b, tm):
    r, d = x2d.shape
    return pl.pallas_call(
        _norm_kernel,
        grid=(r // tm,),
        in_specs=[pl.BlockSpec((tm, d), lambda i: (i, 0)),
                  pl.BlockSpec((None, 1, d), lambda i: (mod_row_of_tile(i), 0, 0)),
                  pl.BlockSpec((None, 1, d), lambda i: (mod_row_of_tile(i), 0, 1)),
                  pl.BlockSpec((1, d), lambda i: (0, 0)),
                  pl.BlockSpec((d, LANES), lambda i: (0, 0)),
                  pl.BlockSpec((1, LANES), lambda i: (0, 0))],
        out_specs=[pl.BlockSpec((tm, d), lambda i: (i, 0)),
                   pl.BlockSpec((tm, LANES), lambda i: (i, 0))],
        out_shape=[jax.ShapeDtypeStruct((r, d), BF16),
                   jax.ShapeDtypeStruct((r, LANES), F32)],
        compiler_params=_cparams(("arbitrary",)),
        name="norm",
    )(x2d, mod3, mod3, norm_g, w_gates, gate_b)


INPROJ_TN = 1024
INPROJ_ROWS = 256


def _inproj_kernel(h_ref, wt_ref, z_ref, w_bf, *, tiles_per_seg):
    j = pl.program_id(0)
    i = pl.program_id(1)

    @pl.when(i == 0)
    def _():
        w_bf[...] = wt_ref[...].T.astype(BF16)

    seg = j // tiles_per_seg
    is_raw = seg < 2
    is_gelu = (seg == 3) | (seg == 4)
    c0 = 0.7978845608028654
    p0 = jnp.where(is_gelu, c0, 0.5).astype(F32)
    p1 = jnp.where(is_gelu, c0 * 0.044715, 0.0).astype(F32)
    q0 = jnp.where(is_gelu, 0.0, 1.0).astype(F32)
    q1 = jnp.where(is_gelu, 1.0, 0.0).astype(F32)
    w = w_bf[...]
    for c in range(h_ref.shape[0] // INPROJ_ROWS):
        rs = slice(c * INPROJ_ROWS, (c + 1) * INPROJ_ROWS)
        z = jnp.dot(h_ref[rs, :], w, preferred_element_type=F32)
        act = (0.5 * (q0 + q1 * z)) * (1.0 + jnp.tanh(z * (p0 + p1 * (z * z))))
        z_ref[rs, :] = jnp.where(is_raw, z, act).astype(BF16)


def _inproj(h, w_in_t, layer, n_aligned, shift, n_tiles, tm):
    r, d = h.shape
    tn = INPROJ_TN
    kern = functools.partial(_inproj_kernel, tiles_per_seg=d // tn)
    assert shift % 8 == 0
    row0 = lambda j: pl.multiple_of(j * tn + jnp.where(j >= n_aligned, shift, 0), 8)
    return pl.pallas_call(
        kern,
        grid=(n_tiles, r // tm),
        in_specs=[pl.BlockSpec((tm, d), lambda j, i: (i, 0)),
                  pl.BlockSpec((None, pl.Element(tn), pl.Element(d)), lambda j, i: (layer, row0(j), 0))],
        out_specs=pl.BlockSpec((tm, tn), lambda j, i: (i, j)),
        out_shape=jax.ShapeDtypeStruct((r, n_tiles * tn), BF16),
        scratch_shapes=[pltpu.VMEM((d, tn), BF16)],
        compiler_params=_cparams(("arbitrary", "arbitrary")),
        name="inproj",
    )(h, w_in_t)


def _conv_kernel(x_ref, w_ref, b_ref, o_ref, *, q_tiles):
    c = pl.program_id(1)
    x = x_ref[...].astype(F32)
    t = x.shape[0]
    rows = lax.broadcasted_iota(I32, x.shape, 0)
    w = w_ref[...]
    acc = x * w[CONV_W // 2:CONV_W // 2 + 1, :] + b_ref[...]
    for dlt in range(-(CONV_W // 2), CONV_W // 2 + 1):
        if dlt == 0:
            continue
        xs = pltpu.roll(x, shift=(-dlt) % t, axis=0)
        valid = (rows + dlt >= 0) & (rows + dlt < t)
        acc = acc + jnp.where(valid, xs, 0.0) * w[dlt + CONV_W // 2:dlt + CONV_W // 2 + 1, :]
    y = acc * _sigmoid(acc)
    scale = jnp.where(c < q_tiles, QK_DIM ** -0.5, 1.0).astype(F32)
    o_ref[...] = (y * scale).astype(BF16)


def _conv(z3, conv_w, conv_b):
    b, t, _ = z3.shape
    width = conv_w.shape[1]
    tc = 256
    kern = functools.partial(_conv_kernel, q_tiles=(width // 2) // tc)
    return pl.pallas_call(
        kern,
        grid=(b, width // tc),
        in_specs=[pl.BlockSpec((None, t, tc), lambda i, c: (i, 0, c)),
                  pl.BlockSpec((CONV_W, tc), lambda i, c: (0, c)),
                  pl.BlockSpec((1, tc), lambda i, c: (0, c))],
        out_specs=pl.BlockSpec((None, t, tc), lambda i, c: (i, 0, c)),
        out_shape=jax.ShapeDtypeStruct((b, t, width), BF16),
        compiler_params=_cparams(("arbitrary", "arbitrary")),
        name="conv",
    )(z3, conv_w, conv_b.reshape(1, width))


def _mlstm_kernel(qkc_f, qkl_f, vc_f, vl_f, gc_f, gl_f, qkc_b, qkl_b, vc_b, vl_b, gc_b, gl_b,
                  hf_ref, hb_ref, *state, n_ctx_chunks):
    c_scr, m_scr = state[:2 * M_HEADS], state[2 * M_HEADS:]
    s = pl.program_id(1)

    @pl.when(s == 0)
    def _():
        for ref in state:
            ref[...] = jnp.zeros_like(ref)

    is_ctx = s < n_ctx_chunks
    rows = lax.broadcasted_iota(I32, (CHUNK, CHUNK), 0)
    lanes = lax.broadcasted_iota(I32, (CHUNK, CHUNK), 1)
    ones_col = jnp.where(lanes == 0, 1.0, 0.0).astype(BF16)
    dn_nt = (((1,), (1,)), ((), ()))
    dn_tn = (((0,), (0,)), ((), ()))
    qk_w = M_HEADS * QK_DIM

    dirs = ((qkc_f, qkl_f, vc_f, vl_f, gc_f, gl_f, hf_ref), (qkc_b, qkl_b, vc_b, vl_b, gc_b, gl_b, hb_ref))
    per_dir = []
    for d, (qkc, qkl, vc, vl, gc, gl, out_ref) in enumerate(dirs):
        qk = jnp.where(is_ctx, qkc[...], qkl[...])
        v = jnp.where(is_ctx, vc[...], vl[...])
        g = jnp.where(is_ctx, gc[...], gl[...])
        mask = (lanes <= rows) if d == 0 else (lanes >= rows)
        tri = jnp.where(mask, 1.0, 0.0).astype(BF16)
        ls = jnp.minimum(g, 0.0) - jnp.log(1.0 + jnp.exp(-jnp.abs(g)))
        l1, l2, l3 = _split3(ls)
        bcol = (jnp.dot(tri, l1, preferred_element_type=F32) + jnp.dot(tri, l2, preferred_element_type=F32)
                + jnp.dot(tri, l3, preferred_element_type=F32))
        brow, grow = bcol.T, g.T
        c0 = 2 * M_HEADS * d
        ib = grow[c0:c0 + M_HEADS, :] - brow[c0 + M_HEADS:c0 + 2 * M_HEADS, :]
        lane_h = lax.broadcasted_iota(I32, ib.shape, 1)
        rmax = ib
        step = 1
        while step < CHUNK:
            if d == 0:
                prev = jnp.where(lane_h >= step, pltpu.roll(rmax, step, axis=1), -jnp.inf)
            else:
                prev = jnp.where(lane_h < CHUNK - step, pltpu.roll(rmax, CHUNK - step, axis=1), -jnp.inf)
            rmax = jnp.maximum(rmax, prev)
            step *= 2
        rmax_col = jnp.concatenate([rmax, jnp.zeros((CHUNK - M_HEADS, CHUNK), F32)], axis=0).T
        per_dir.append((d, qk, v, g, mask, bcol, ib, rmax_col, out_ref))

    units = [(pd, h) for pd in per_dir for h in range(M_HEADS)]
    stage1 = []
    for (d, qk, v, g, mask, bcol, ib, rmax_col, out_ref), h in units:
        col_i = 2 * M_HEADS * d + h
        col_f = col_i + M_HEADS
        idx = d * M_HEADS + h
        end = CHUNK - 1 if d == 0 else 0
        bc = bcol[:, col_f:col_f + 1]
        ic = g[:, col_i:col_i + 1]
        ibr = ib[h:h + 1, :]
        m = m_scr[idx][0:1, 0:1]
        inter = bc + m
        m_t = jnp.maximum(inter, bc + rmax_col[:, h:h + 1])
        w_inter = jnp.exp(inter - m_t)
        b_end = bc[end:end + 1, :]
        m_new = jnp.maximum(b_end + m, b_end + rmax_col[end:end + 1, h:h + 1])
        decay = jnp.exp(b_end + m - m_new)
        w_col = jnp.exp(b_end - bc + ic - m_new)
        q = qk[:, h * QK_DIM:(h + 1) * QK_DIM]
        k = qk[:, qk_w + h * QK_DIM:qk_w + (h + 1) * QK_DIM]
        decay_mat = jnp.exp(jnp.where(mask, (bc - m_t) + ibr, -jnp.inf))
        sc = lax.dot_general(q, k, dn_nt, preferred_element_type=F32) * decay_mat
        lhs = jnp.concatenate([(q.astype(F32) * w_inter).astype(BF16), sc.astype(BF16)], axis=1)
        kw = (k.astype(F32) * w_col).astype(BF16)
        stage1.append((lhs, kw, m_t, m_new, decay))

    stage2 = []
    for ((d, qk, v, g, mask, bcol, brow, grow, out_ref), h), (lhs, kw, m_t, m_new, decay) in zip(units, stage1):
        idx = d * M_HEADS + h
        vaug = jnp.concatenate([v[:, h * V_DIM:(h + 1) * V_DIM], ones_col], axis=1)
        caug = c_scr[idx][...]
        num = jnp.dot(lhs, jnp.concatenate([caug.astype(BF16), vaug], axis=0), preferred_element_type=F32)
        upd = lax.dot_general(kw, vaug, dn_tn, preferred_element_type=F32)
        stage2.append((num, upd, caug))

    for ((d, qk, v, g, mask, bcol, brow, grow, out_ref), h), (lhs, kw, m_t, m_new, decay), (num, upd, caug) in zip(
            units, stage1, stage2):
        idx = d * M_HEADS + h
        den = num[:, V_DIM:V_DIM + 1]
        hout = num[:, :V_DIM] / jnp.maximum(jnp.abs(den), jnp.exp(-m_t))
        c_scr[idx][...] = decay * caug + upd
        m_scr[idx][...] = jnp.broadcast_to(m_new, m_scr[idx].shape)
        out_ref[:, h * V_DIM:(h + 1) * V_DIM] = hout.astype(out_ref.dtype)


def _mlstm(qk_ctx, qk_lat, z_ctx3, z_lat3, g_ctx3, g_lat3):
    b, tc, _ = qk_ctx.shape
    t = qk_lat.shape[1]
    nc, nl = tc // CHUNK, t // CHUNK
    vw = M_HEADS * V_DIM
    qkw = 2 * M_HEADS * QK_DIM

    f_ctx = lambda i, s: (i, jnp.minimum(s, nc - 1), 0)
    f_lat = lambda i, s: (i, jnp.clip(s - nc, 0, nl - 1), 0)
    b_ctx = lambda i, s: (i, jnp.maximum(nc - 1 - s, 0), 0)
    b_lat = lambda i, s: (i, jnp.clip(nc + nl - 1 - s, 0, nl - 1), 0)
    v_of = lambda f: (lambda i, s: (f(i, s)[0], f(i, s)[1], 1))

    def specs(fc, fl):
        return [pl.BlockSpec((None, CHUNK, qkw), fc), pl.BlockSpec((None, CHUNK, qkw), fl),
                pl.BlockSpec((None, CHUNK, vw), v_of(fc)), pl.BlockSpec((None, CHUNK, vw), v_of(fl)),
                pl.BlockSpec((None, CHUNK, LANES), fc), pl.BlockSpec((None, CHUNK, LANES), fl)]

    kern = functools.partial(_mlstm_kernel, n_ctx_chunks=nc)
    args = (qk_ctx, qk_lat, z_ctx3, z_lat3, g_ctx3, g_lat3)
    return pl.pallas_call(
        kern,
        grid=(b, nc + nl),
        in_specs=specs(f_ctx, f_lat) + specs(b_ctx, b_lat),
        out_specs=[pl.BlockSpec((None, CHUNK, vw), f_lat), pl.BlockSpec((None, CHUNK, vw), b_lat)],
        out_shape=[jax.ShapeDtypeStruct((b, t, vw), BF16), jax.ShapeDtypeStruct((b, t, vw), BF16)],
        scratch_shapes=([pltpu.VMEM((QK_DIM, V_DIM + LANES), F32)] * (2 * M_HEADS)
                        + [pltpu.VMEM((8, LANES), F32)] * (2 * M_HEADS)),
        compiler_params=_cparams(("arbitrary", "arbitrary")),
        name="mlstm",
    )(*args, *args)


MERGE_ROWS = 256


def _merge_kernel(hf_ref, hb_ref, o_ref, u_ref, sv_ref, ga_ref, gb_ref, gm_ref, gs_ref, sw_ref, sb_ref,
                  pa_ref, pb_ref, out_ref):
    tm, d = out_ref.shape
    gw = d // SGU_GROUPS
    for grp in range(tm // MERGE_ROWS):
        r0 = grp * MERGE_ROWS
        rs = slice(r0, r0 + MERGE_ROWS)
        a_parts = []
        for h in range(M_HEADS):
            sl = slice(h * V_DIM, (h + 1) * V_DIM)
            hs = hf_ref[rs, sl].astype(F32) + hb_ref[rs, sl].astype(F32)
            ms = jnp.mean(hs * hs, axis=-1, keepdims=True)
            y = hs * lax.rsqrt(ms + EPS) * gm_ref[:, sl]
            a_parts.append((y * o_ref[rs, sl].astype(F32)).astype(BF16))
        a = jnp.concatenate(a_parts, axis=1)
        sv = sv_ref[rs, :].astype(F32)
        ms = jnp.mean(sv * sv, axis=-1, keepdims=True)
        svn = (sv * lax.rsqrt(ms + EPS) * gs_ref[...]).astype(BF16)
        b_rows = []
        for c in range(MERGE_ROWS // CHUNK):
            cr = slice(c * CHUNK, (c + 1) * CHUNK)
            b_parts = []
            for gi in range(SGU_GROUPS):
                cs = slice(gi * gw, (gi + 1) * gw)
                zz = jnp.dot(sw_ref[gi], svn[cr, cs], preferred_element_type=F32) + sb_ref[:, gi:gi + 1]
                b_parts.append((u_ref[r0 + c * CHUNK:r0 + (c + 1) * CHUNK, cs].astype(F32) * zz).astype(BF16))
            b_rows.append(jnp.concatenate(b_parts, axis=1))
        b = jnp.concatenate(b_rows, axis=0)
        ya = jnp.dot(a, pa_ref[...], preferred_element_type=F32)
        yb = jnp.dot(b, pb_ref[...], preferred_element_type=F32)
        out_ref[rs, :] = (ga_ref[rs, :].astype(F32) * ya + gb_ref[rs, :].astype(F32) * yb).astype(BF16)


def _merge(hf, hb, z, gm, gs, sgu_w, sgu_bt, pa, pb):
    r, d = hf.shape
    tm = 512
    row = lambda k: (lambda i: (i, k))
    const = lambda shape: pl.BlockSpec(shape, lambda i: (0,) * len(shape), pipeline_mode=pl.Buffered(1))
    return pl.pallas_call(
        _merge_kernel,
        grid=(r // tm,),
        in_specs=[pl.BlockSpec((tm, d), row(0)), pl.BlockSpec((tm, d), row(0)),
                  pl.BlockSpec((tm, d), row(2)), pl.BlockSpec((tm, d), row(3)), pl.BlockSpec((tm, d), row(4)),
                  pl.BlockSpec((tm, d), row(5)), pl.BlockSpec((tm, d), row(6)),
                  const((1, d)), const((1, d)), const(sgu_w.shape), const(sgu_bt.shape),
                  const((d, d)), const((d, d))],
        out_specs=pl.BlockSpec((tm, d), row(0)),
        out_shape=jax.ShapeDtypeStruct((r, d), BF16),
        compiler_params=_cparams(("arbitrary",)),
        name="merge",
    )(hf, hb, z, z, z, z, z, gm, gs, sgu_w, sgu_bt, pa, pb)


def _outproj_kernel(m_ref, w_ref, x_ref, gate_ref, shift_ref, scale_ref, g_ref, rw_ref, rb_ref,
                    x1_ref, h2_ref, lg_ref):
    r1, r2, _ = _split3(rw_ref[...])
    ne = lg_ref.shape[0]
    for grp in range(x_ref.shape[0] // MERGE_ROWS):
        r0 = grp * MERGE_ROWS
        rs = slice(r0, r0 + MERGE_ROWS)
        out = jnp.dot(m_ref[rs, :], w_ref[...], preferred_element_type=F32)
        x1 = x_ref[rs, :] + gate_ref[...] * out
        x1_ref[rs, :] = x1
        ms = jnp.mean(x1 * x1, axis=-1, keepdims=True)
        h2 = (x1 * lax.rsqrt(ms + EPS) * g_ref[...]) * (1.0 + scale_ref[...]) + shift_ref[...]
        hi, lo, _ = _split3(h2)
        lg = (jnp.dot(hi, r1, preferred_element_type=F32) + jnp.dot(lo, r1, preferred_element_type=F32)
              + jnp.dot(hi, r2, preferred_element_type=F32)) + rb_ref[...]
        lg_ref[:, rs] = lg.T[:ne, :]
        _store_token_major(h2_ref, r0, h2)


def _outproj(merged, w_out, x2d, mod3, mod_row_of_tile, norm_g, rw, rb, tm):
    r, d = x2d.shape
    ne = rw.shape[1]
    rw_t = jnp.pad(rw, ((0, 0), (0, LANES - ne)))
    rb_col = jnp.pad(rb.reshape(1, ne), ((0, 0), (0, LANES - ne)))
    modspec = lambda k: pl.BlockSpec((None, 1, d), lambda i: (mod_row_of_tile(i), 0, k))
    return pl.pallas_call(
        _outproj_kernel,
        grid=(r // tm,),
        in_specs=[pl.BlockSpec((tm, d), lambda i: (i, 0)),
                  pl.BlockSpec((d, d), lambda i: (0, 0), pipeline_mode=pl.Buffered(1)),
                  pl.BlockSpec((tm, d), lambda i: (i, 0)),
                  modspec(2), modspec(3), modspec(4),
                  pl.BlockSpec((1, d), lambda i: (0, 0)),
                  pl.BlockSpec((d, LANES), lambda i: (0, 0)),
                  pl.BlockSpec((1, LANES), lambda i: (0, 0))],
        out_specs=[pl.BlockSpec((tm, d), lambda i: (i, 0)),
                   pl.BlockSpec((tm * TOKEN_ROWS, LANES), lambda i: (i, 0)),
                   pl.BlockSpec((ne, tm), lambda i: (0, i))],
        out_shape=[jax.ShapeDtypeStruct((r, d), F32),
                   jax.ShapeDtypeStruct((r * TOKEN_ROWS, LANES), F32),
                   jax.ShapeDtypeStruct((ne, r), F32)],
        compiler_params=_cparams(("arbitrary",)),
        name="outproj",
    )(merged, w_out, x2d, mod3, mod3, mod3, norm_g, rw_t, rb_col)


def _route_kernel(lg_ref, eid_ref, gate_ref, rank_ref, cnt_ref, carry_scr):
    i = pl.program_id(0)

    @pl.when(i == 0)
    def _():
        carry_scr[...] = jnp.zeros_like(carry_scr)

    l = lg_ref[...]
    ne, tm = l.shape
    e_iota = lax.broadcasted_iota(I32, (ne, tm), 0)
    vals, onehots = [], []
    for k in range(TOP_K):
        mx = jnp.max(l, axis=0, keepdims=True)
        idx = jnp.min(jnp.where(l == mx, e_iota, ne), axis=0, keepdims=True)
        oh = e_iota == idx
        l = jnp.where(oh, -jnp.inf, l)
        vals.append(mx)
        onehots.append(oh)
        eid_ref[k:k + 1, :] = idx
    ex = [jnp.exp(vk - vals[0]) for vk in vals]
    tot = ex[0] + ex[1] + ex[2] + ex[3]
    for k in range(TOP_K):
        gate_ref[k:k + 1, :] = ex[k] / tot

    oh_all = jnp.zeros((ne, tm), F32)
    for oh in onehots:
        oh_all = oh_all + jnp.where(oh, 1.0, 0.0)
    oh_all = oh_all.astype(BF16)
    r_i = lax.broadcasted_iota(I32, (LANES, LANES), 0)
    c_i = lax.broadcasted_iota(I32, (LANES, LANES), 1)
    tri_excl = jnp.where(r_i < c_i, 1.0, 0.0).astype(BF16)
    ones_m = jnp.ones((LANES, LANES), BF16)
    carry = carry_scr[...]
    for blk in range(tm // LANES):
        sl = slice(blk * LANES, (blk + 1) * LANES)
        ohb = oh_all[:, sl]
        cum = jnp.dot(ohb, tri_excl, preferred_element_type=F32) + carry
        for k in range(TOP_K):
            rk = jnp.sum(jnp.where(onehots[k][:, sl], cum, 0.0), axis=0, keepdims=True)
            rank_ref[k:k + 1, sl] = rk.astype(I32)
        carry = carry + jnp.dot(ohb, ones_m, preferred_element_type=F32)
    carry_scr[...] = carry
    cnt_ref[...] = carry


def _route(logits_t):
    ne, r = logits_t.shape
    tm = 1024
    return pl.pallas_call(
        _route_kernel,
        grid=(r // tm,),
        in_specs=[pl.BlockSpec((ne, tm), lambda i: (0, i))],
        out_specs=[pl.BlockSpec((TOP_K, tm), lambda i: (0, i)),
                   pl.BlockSpec((TOP_K, tm), lambda i: (0, i)),
                   pl.BlockSpec((TOP_K, tm), lambda i: (0, i)),
                   pl.BlockSpec((ne, LANES), lambda i: (0, 0))],
        out_shape=[jax.ShapeDtypeStruct((TOP_K, r), I32),
                   jax.ShapeDtypeStruct((TOP_K, r), F32),
                   jax.ShapeDtypeStruct((TOP_K, r), I32),
                   jax.ShapeDtypeStruct((ne, LANES), F32)],
        scratch_shapes=[pltpu.VMEM((ne, LANES), F32)],
        compiler_params=_cparams(("arbitrary",)),
        name="route",
    )(logits_t)


def _slot_kernel(eid_ref, rank_ref, cnt_ref, slot_ref):
    ne = cnt_ref.shape[0]
    nblk = jnp.floor((cnt_ref[...] + (MOE_TILE - 1.0)) * (1.0 / MOE_TILE))
    r_i = lax.broadcasted_iota(I32, (ne, ne), 0)
    c_i = lax.broadcasted_iota(I32, (ne, ne), 1)
    tri = jnp.where(c_i < r_i, 1.0, 0.0).astype(BF16)
    start = jnp.dot(tri, nblk.astype(BF16), preferred_element_type=F32) * MOE_TILE
    eid = eid_ref[...]
    acc = rank_ref[...]
    for e in range(ne):
        acc = acc + jnp.where(eid == e, start[e:e + 1, 0:1].astype(I32), 0)
    slot_ref[...] = acc


def _slots(eid, rank, cnt):
    return pl.pallas_call(
        _slot_kernel,
        out_shape=jax.ShapeDtypeStruct(eid.shape, I32),
        compiler_params=pltpu.CompilerParams(vmem_limit_bytes=V7X_VMEM_LIMIT_BYTES),
        name="slots",
    )(eid, rank, cnt)


def _dispatch_copy(slot_ref, h_ref, xs_ref, sem, k, t, base, n_tok):
    d0 = pl.multiple_of(slot_ref[k * n_tok + base + t] * TOKEN_ROWS, TOKEN_ROWS)
    s0 = pl.multiple_of(t * TOKEN_ROWS, TOKEN_ROWS)
    return pltpu.make_async_copy(h_ref.at[pl.ds(s0, TOKEN_ROWS), :], xs_ref.at[pl.ds(d0, TOKEN_ROWS), :], sem)


def _zero_copy(zbuf, xs_ref, zsem, row, n_rows):
    d0 = pl.multiple_of(row * TOKEN_ROWS, TOKEN_ROWS)
    return pltpu.make_async_copy(zbuf.at[pl.ds(0, n_rows * TOKEN_ROWS), :],
                                 xs_ref.at[pl.ds(d0, n_rows * TOKEN_ROWS), :], zsem)


def _zero_fill(start_ref, cnt_ref, zbuf, xs_ref, zsem, p_rows, begin):
    ne = start_ref.shape[0]

    def op(cp):
        cp.start() if begin else cp.wait()

    def per_expert(e, used):
        n = cnt_ref[e]
        s0 = start_ref[e]
        c_step = (n + MOE_STEP - 1) // MOE_STEP * MOE_STEP
        c_tile = (n + MOE_TILE - 1) // MOE_TILE * MOE_TILE

        def one_row(r, c):
            op(_zero_copy(zbuf, xs_ref, zsem, s0 + r, 1))
            return c

        def one_unit(u, c):
            op(_zero_copy(zbuf, xs_ref, zsem, s0 + u * MOE_STEP, MOE_STEP))
            return c

        lax.fori_loop(n, c_step, one_row, 0)
        lax.fori_loop(c_step // MOE_STEP, c_tile // MOE_STEP, one_unit, 0)
        return s0 + c_tile

    used = lax.fori_loop(0, ne, per_expert, 0)

    def trailing(u, c):
        op(_zero_copy(zbuf, xs_ref, zsem, u * MOE_STEP, MOE_STEP))
        return c

    lax.fori_loop(used // MOE_STEP, p_rows // MOE_STEP, trailing, 0)


def _dispatch_kernel(slot_ref, start_ref, cnt_ref, h_ref, xs_ref, zbuf, sem, zsem, *, tm, n_tok, p_rows):
    i = pl.program_id(0)
    base = i * tm

    @pl.when(i == 0)
    def _():
        zbuf[...] = jnp.zeros_like(zbuf)
        _zero_fill(start_ref, cnt_ref, zbuf, xs_ref, zsem, p_rows, True)

    def start(t, carry):
        for k in range(TOP_K):
            _dispatch_copy(slot_ref, h_ref, xs_ref, sem, k, t, base, n_tok).start(priority=k % 2)
        return carry

    def wait(t, carry):
        for k in range(TOP_K):
            _dispatch_copy(slot_ref, h_ref, xs_ref, sem, k, t, base, n_tok).wait()
        return carry

    lax.fori_loop(0, tm, start, 0, unroll=4)
    lax.fori_loop(0, tm, wait, 0, unroll=4)

    @pl.when(i == 0)
    def _():
        _zero_fill(start_ref, cnt_ref, zbuf, xs_ref, zsem, p_rows, False)


def _dispatch(h2p, slot, start_pad, counts, p_rows):
    n_tok = h2p.shape[0] // TOKEN_ROWS
    tm = 512
    grid_spec = pltpu.PrefetchScalarGridSpec(
        num_scalar_prefetch=3,
        grid=(n_tok // tm,),
        in_specs=[pl.BlockSpec((tm * TOKEN_ROWS, LANES), lambda i, *_: (i, 0))],
        out_specs=pl.BlockSpec(memory_space=pl.ANY),
        scratch_shapes=[pltpu.VMEM((MOE_STEP * TOKEN_ROWS, LANES), h2p.dtype),
                        pltpu.SemaphoreType.DMA(()), pltpu.SemaphoreType.DMA(())],
    )
    return pl.pallas_call(
        functools.partial(_dispatch_kernel, tm=tm, n_tok=n_tok, p_rows=p_rows),
        grid_spec=grid_spec,
        out_shape=jax.ShapeDtypeStruct((p_rows * TOKEN_ROWS, LANES), h2p.dtype),
        compiler_params=_cparams(("arbitrary",)),
        name="dispatch",
    )(slot, start_pad, counts, h2p)


def _weight_copies(w_hbm, wbuf, sems, e, col_starts, slot):
    tn = wbuf.shape[-1]
    aligned = lambda c: c if isinstance(c, int) else pl.multiple_of(c, LANES)
    return [pltpu.make_async_copy(w_hbm.at[e, :, pl.ds(aligned(c), tn)], wbuf.at[slot, i], sems.at[slot])
            for i, c in enumerate(col_starts)]


def _advance_weight_pipeline(w_hbm, wbuf, sems, sched, j, nj, it, col_starts_of):
    e_ref, run_ref, last_ref, nxt_ref, nruns_ref = sched
    slot = (j * nruns_ref[0] + run_ref[it]) % 2

    @pl.when((j == 0) & (it == 0))
    def _():
        for cp in _weight_copies(w_hbm, wbuf, sems, e_ref[it], col_starts_of(j), slot):
            cp.start()

    for cp in _weight_copies(w_hbm, wbuf, sems, e_ref[it], col_starts_of(j), slot):
        cp.wait()
    is_last = last_ref[it] == 1

    @pl.when(jnp.logical_not(is_last))
    def _():
        for cp in _weight_copies(w_hbm, wbuf, sems, nxt_ref[it], col_starts_of(j), 1 - slot):
            cp.start()

    @pl.when(is_last & (j + 1 < nj))
    def _():
        for cp in _weight_copies(w_hbm, wbuf, sems, nxt_ref[it], col_starts_of(j + 1), 1 - slot):
            cp.start()

    return slot


def _moe_up_kernel(blk_ref, oblk_ref, e_ref, nv_ref, new_ref, run_ref, last_ref, nxt_ref, nruns_ref,
                   xs_ref, w_hbm, bg_ref, bl_ref, act_ref, wbuf, sems, wg_bf, wl_bf):
    j = pl.program_id(0)
    nj = pl.num_programs(0)
    it = pl.program_id(1)
    tn = wg_bf.shape[1]

    @pl.when(new_ref[it] == 1)
    def _():
        slot = _advance_weight_pipeline(w_hbm, wbuf, sems, (e_ref, run_ref, last_ref, nxt_ref, nruns_ref), j, nj, it,
                                        lambda jj: (jj * tn, (nj + jj) * tn))
        wg_bf[...] = wbuf[slot, 0].astype(BF16)
        wl_bf[...] = wbuf[slot, 1].astype(BF16)

    def rows_chunk(r0, n):
        x = _load_token_major(xs_ref, r0, n).astype(BF16)
        hg = jnp.dot(x, wg_bf[...], preferred_element_type=F32) + bg_ref[...]
        hl = jnp.dot(x, wl_bf[...], preferred_element_type=F32) + bl_ref[...]
        glu = jnp.minimum(hg, SWIGLU_LIMIT)
        lin = jnp.clip(hl, -SWIGLU_LIMIT, SWIGLU_LIMIT)
        act_ref[r0:r0 + n, :] = ((0.5 * glu) * (1.0 + jnp.tanh((0.5 * SWIGLU_ALPHA) * glu)) * (lin + 1.0)).astype(BF16)

    def rows_zero(r0, n):
        act_ref[r0:r0 + n, :] = jnp.zeros((n, act_ref.shape[1]), BF16)

    _moe_row_variants(nv_ref[it], rows_chunk, rows_zero)


def _moe_row_variants(nv, rows_chunk, rows_zero):
    for units in range(MOE_TILE // MOE_STEP + 1):
        rows = units * MOE_STEP

        @pl.when((nv > rows - MOE_STEP) & (nv <= rows) if units else nv <= 0)
        def _():
            r0 = 0
            while r0 < rows:
                n = min(MOE_HALF, rows - r0)
                rows_chunk(r0, n)
                r0 += n
            if rows < MOE_TILE:
                rows_zero(rows, MOE_TILE - rows)


def _moe_down_kernel(blk_ref, oblk_ref, e_ref, nv_ref, new_ref, run_ref, last_ref, nxt_ref, nruns_ref,
                     a_ref, w_hbm, b_ref, y_ref, wbuf, sems, w_bf):
    it = pl.program_id(0)

    @pl.when(new_ref[it] == 1)
    def _():
        slot = _advance_weight_pipeline(w_hbm, wbuf, sems, (e_ref, run_ref, last_ref, nxt_ref, nruns_ref), 0, 1, it,
                                        lambda jj: (0,))
        w_bf[...] = wbuf[slot, 0].astype(BF16)

    def rows_chunk(r0, n):
        y = jnp.dot(a_ref[r0:r0 + n, :], w_bf[...], preferred_element_type=F32) + b_ref[...]
        _store_token_major(y_ref, r0, y)

    def rows_zero(r0, n):
        y_ref[r0 * TOKEN_ROWS:(r0 + n) * TOKEN_ROWS, :] = jnp.zeros((n * TOKEN_ROWS, LANES), F32)

    _moe_row_variants(nv_ref[it], rows_chunk, rows_zero)


def _moe_up(items, xs, w1, b1):
    n_items = items[0].shape[0]
    ne, d, two_f = w1.shape
    f = two_f // 2
    nj = f // MOE_TN
    p = xs.shape[0] // TOKEN_ROWS
    b1r = b1.reshape(ne, 1, two_f)
    grid_spec = pltpu.PrefetchScalarGridSpec(
        num_scalar_prefetch=len(items),
        grid=(nj, n_items),
        in_specs=[pl.BlockSpec((MOE_TILE * TOKEN_ROWS, LANES), lambda j, it, blk, *_: (blk[it], 0)),
                  pl.BlockSpec(memory_space=pl.ANY),
                  pl.BlockSpec((None, 1, MOE_TN), lambda j, it, blk, oblk, e, *_: (e[it], 0, j)),
                  pl.BlockSpec((None, 1, MOE_TN), lambda j, it, blk, oblk, e, *_: (e[it], 0, nj + j))],
        out_specs=pl.BlockSpec((MOE_TILE, MOE_TN), lambda j, it, blk, oblk, *_: (oblk[it], j)),
        scratch_shapes=[pltpu.VMEM((2, 2, d, MOE_TN), F32), pltpu.SemaphoreType.DMA((2,)),
                        pltpu.VMEM((d, MOE_TN), BF16), pltpu.VMEM((d, MOE_TN), BF16)],
    )
    return pl.pallas_call(
        _moe_up_kernel,
        grid_spec=grid_spec,
        out_shape=jax.ShapeDtypeStruct((p, f), BF16),
        compiler_params=_cparams(("arbitrary", "arbitrary")),
        name="moe_up",
    )(*items, xs, w1, b1r, b1r)


def _moe_down(items, act, w2, b2):
    n_items = items[0].shape[0]
    ne, f, d = w2.shape
    p = act.shape[0]
    b2r = b2.reshape(ne, 1, d)
    grid_spec = pltpu.PrefetchScalarGridSpec(
        num_scalar_prefetch=len(items),
        grid=(n_items,),
        in_specs=[pl.BlockSpec((MOE_TILE, f), lambda it, blk, *_: (blk[it], 0)),
                  pl.BlockSpec(memory_space=pl.ANY),
                  pl.BlockSpec((None, 1, d), lambda it, blk, oblk, e, *_: (e[it], 0, 0))],
        out_specs=pl.BlockSpec((MOE_TILE * TOKEN_ROWS, LANES), lambda it, blk, oblk, *_: (oblk[it], 0)),
        scratch_shapes=[pltpu.VMEM((2, 1, f, d), F32), pltpu.SemaphoreType.DMA((2,)), pltpu.VMEM((f, d), BF16)],
    )
    return pl.pallas_call(
        _moe_down_kernel,
        grid_spec=grid_spec,
        out_shape=jax.ShapeDtypeStruct((p * TOKEN_ROWS, LANES), F32),
        compiler_params=_cparams(("arbitrary",)),
        name="moe_down",
    )(*items, act, w2, b2r)


FINAL_ROWS = 64


def _combine_copy(slot_ref, y_ref, buf, sems, k, t, tile, par, tm, n_tok):
    s0 = pl.multiple_of(slot_ref[k * n_tok + tile * tm + t] * TOKEN_ROWS, TOKEN_ROWS)
    d0 = pl.multiple_of(t * TOKEN_ROWS, TOKEN_ROWS)
    return pltpu.make_async_copy(y_ref.at[pl.ds(s0, TOKEN_ROWS), :], buf.at[par, k, pl.ds(d0, TOKEN_ROWS), :],
                                 sems.at[par])


def _final_kernel(slot_ref, y_ref, gates_ref, x1_ref, gate_ref, g_ref, o_ref, buf, sems, *, n_tok):
    tm, d = x1_ref.shape
    i = pl.program_id(0)
    par = i % 2

    def gather(tile, par_, start):
        def body(t, carry):
            for k in range(TOP_K):
                cp = _combine_copy(slot_ref, y_ref, buf, sems, k, t, tile, par_, tm, n_tok)
                cp.start(priority=k % 2) if start else cp.wait()
            return carry
        lax.fori_loop(0, tm, body, 0, unroll=4)

    @pl.when(i == 0)
    def _():
        gather(0, 0, True)

    @pl.when(i + 1 < pl.num_programs(0))
    def _():
        gather(i + 1, 1 - par, True)

    gather(i, par, False)

    for grp in range(tm // FINAL_ROWS):
        r0 = grp * FINAL_ROWS
        rs = slice(r0, r0 + FINAL_ROWS)
        gk = [jnp.broadcast_to(gates_ref[rs, k:k + 1], (FINAL_ROWS, LANES)) for k in range(TOP_K)]
        sq = jnp.zeros((FINAL_ROWS, LANES), F32)
        for s in range(TOKEN_ROWS):
            cs = slice(s * LANES, (s + 1) * LANES)
            rows = pl.ds(r0 * TOKEN_ROWS + s, FINAL_ROWS, stride=TOKEN_ROWS)
            acc = gk[0] * buf[par, 0, rows, :]
            for k in range(1, TOP_K):
                acc = acc + gk[k] * buf[par, k, rows, :]
            xf = x1_ref[rs, cs] + gate_ref[:, cs] * acc
            o_ref[rs, cs] = xf
            sq = sq + xf * xf
        scale = lax.rsqrt(jnp.sum(sq, axis=-1, keepdims=True) * (1.0 / d) + EPS)
        for s in range(TOKEN_ROWS):
            cs = slice(s * LANES, (s + 1) * LANES)
            o_ref[rs, cs] = o_ref[rs, cs] * scale * g_ref[:, cs]


def _final(slot, y, gates_tk, x1, mod3, mod_row_of_tile, final_g, tm):
    r, d = x1.shape
    grid_spec = pltpu.PrefetchScalarGridSpec(
        num_scalar_prefetch=1,
        grid=(r // tm,),
        in_specs=[pl.BlockSpec(memory_space=pl.ANY),
                  pl.BlockSpec((tm, TOP_K), lambda i, slot: (i, 0)),
                  pl.BlockSpec((tm, d), lambda i, slot: (i, 0)),
                  pl.BlockSpec((None, 1, d), lambda i, slot: (mod_row_of_tile(i), 0, 5)),
                  pl.BlockSpec((1, d), lambda i, slot: (0, 0))],
        out_specs=pl.BlockSpec((tm, d), lambda i, slot: (i, 0)),
        scratch_shapes=[pltpu.VMEM((2, TOP_K, tm * TOKEN_ROWS, LANES), F32), pltpu.SemaphoreType.DMA((2,))],
    )
    return pl.pallas_call(
        functools.partial(_final_kernel, n_tok=r),
        grid_spec=grid_spec,
        out_shape=jax.ShapeDtypeStruct((r, d), F32),
        compiler_params=_cparams(("arbitrary",)),
        name="final",
    )(slot, y, gates_tk, x1, mod3, final_g)


def _moe_schedule(counts, n_items):
    ne = counts.shape[0]
    padded = (counts + MOE_TILE - 1) // MOE_TILE * MOE_TILE
    pad_end = jnp.cumsum(padded)
    start_pad = pad_end - padded
    n_real = pad_end[-1] // MOE_TILE
    b = jnp.arange(n_items, dtype=I32)
    blk = jnp.minimum(b, jnp.maximum(n_real - 1, 0))
    blk_start = blk * MOE_TILE
    e_of = jnp.minimum(jnp.sum((pad_end[None, :] <= blk_start[:, None]).astype(I32), axis=1), ne - 1)
    nv = jnp.clip(counts[e_of] - (blk_start - start_pad[e_of]), 0, MOE_TILE)
    nv = jnp.where(b < n_real, nv, 0)
    e_of = e_of.astype(I32)
    prev_e = jnp.concatenate([jnp.full((1,), -1, I32), e_of[:-1]])
    new = ((e_of != prev_e) | (b == 0)).astype(I32)
    run = jnp.cumsum(new) - 1
    n_runs = run[-1] + 1
    last = (run == n_runs - 1).astype(I32)
    first_pos = jnp.where(new == 1, b, n_items)
    nxt_pos = jnp.concatenate([lax.cummin(first_pos, reverse=True)[1:], jnp.full((1,), n_items, I32)])
    nxt = jnp.where(nxt_pos < n_items, e_of[jnp.minimum(nxt_pos, n_items - 1)], e_of[0])
    return start_pad.astype(I32), (blk.astype(I32), b, e_of, nv.astype(I32), new, run.astype(I32), last,
                                   nxt.astype(I32), n_runs.reshape(1).astype(I32))


def kernel(x, c, ctx, c_ctx, mod_w, mod_b, norm1_g, norm2_g, w_in, gate_b, conv_w, conv_b, mlstm_norm_g,
           sgu_norm_g, sgu_w, sgu_b, proj_a, proj_b, w_out, router_w, router_b, exp_w1, exp_b1, exp_w2,
           exp_b2, final_g):
    bsz, t, d = x.shape
    tc = ctx.shape[1]
    depth = mod_w.shape[0]
    assert depth == 1, "single-layer block"
    assert d == TOKEN_ROWS * LANES, "token-major rows hold TOKEN_ROWS*128 features"
    l = 0
    r = bsz * t
    qkw = 2 * M_HEADS * QK_DIM
    vw = M_HEADS * V_DIM
    off_gates = qkw + vw
    n_gates = 4 * M_HEADS
    off_o = off_gates + n_gates

    cc = jnp.zeros((8, d), F32).at[:bsz].set(c).at[bsz].set(c_ctx)
    mod3 = _mod(cc, mod_w[l], mod_b[l]).reshape(8, 1, N_MOD * d)

    in_cols = w_in.shape[2]
    assert off_gates % INPROJ_TN == 0 and (in_cols - off_o) % INPROJ_TN == 0
    n_aligned = off_gates // INPROJ_TN
    n_tiles = n_aligned + (in_cols - off_o) // INPROJ_TN
    w_in_t = jnp.swapaxes(w_in, 1, 2)
    w_gates = jnp.pad(w_in_t[l, off_gates:off_o, :].T, ((0, 0), (0, LANES - n_gates)))
    gb = jnp.pad(gate_b[l].reshape(1, n_gates), ((0, 0), (0, LANES - n_gates)))
    g1 = norm1_g[l].reshape(1, d)

    tm_in = 512
    h_lat, g_lat = _norm(x.reshape(r, d), mod3, lambda i: i // (t // tm_in), g1, w_gates, gb, tm_in)
    h_ctx, g_ctx = _norm(ctx.reshape(bsz * tc, d), mod3, lambda i: bsz, g1, w_gates, gb, tc)
    z_lat = _inproj(h_lat, w_in_t, l, n_aligned, n_gates, n_tiles, 2048)
    z_ctx = _inproj(h_ctx, w_in_t, l, n_aligned, n_gates, n_aligned, bsz * tc)

    z_lat3 = z_lat.reshape(bsz, t, z_lat.shape[1])
    z_ctx3 = z_ctx.reshape(bsz, tc, z_ctx.shape[1])
    qk_lat = _conv(z_lat3, conv_w[l], conv_b[l])
    qk_ctx = _conv(z_ctx3, conv_w[l], conv_b[l])

    hf, hb = _mlstm(qk_ctx, qk_lat, z_ctx3, z_lat3, g_ctx.reshape(bsz, tc, LANES), g_lat.reshape(bsz, t, LANES))

    merged = _merge(hf.reshape(r, vw), hb.reshape(r, vw), z_lat,
                    mlstm_norm_g[l].reshape(1, vw), sgu_norm_g[l].reshape(1, -1),
                    sgu_w[l].astype(BF16), sgu_b[l].T, proj_a[l].astype(BF16), proj_b[l].astype(BF16))

    tm_out = 512
    x1, h2p, logits_t = _outproj(merged, w_out[l].astype(BF16), x.reshape(r, d), mod3,
                                 lambda i: i // (t // tm_out), norm2_g[l].reshape(1, d),
                                 router_w[l], router_b[l], tm_out)
    tm_fin = 256

    eid, gates, rank, cnt = _route(logits_t)
    counts = cnt[:, 0].astype(I32)
    n_items = (r * TOP_K) // MOE_TILE + N_EXPERTS
    start_pad, items = _moe_schedule(counts, n_items)
    slot = _slots(eid, rank, cnt).reshape(-1)
    xs = _dispatch(h2p, slot, start_pad, counts, n_items * MOE_TILE)
    act = _moe_up(items, xs, exp_w1[l], exp_b1[l])
    y = _moe_down(items, act, exp_w2[l], exp_b2[l])
    out = _final(slot, y, gates.T, x1, mod3, lambda i: i // (t // tm_fin), final_g.reshape(1, d), tm_fin)
    return out.reshape(bsz, t, d)
```
